```python
import math
import jax, jax.numpy as jnp
from jax import lax
import numpy as np

D_MODEL = 1024
BATCH = 32
SEQ = 2048
DEPTH = 1

MIX_WIDTH = D_MODEL
LRU_WIDTH = MIX_WIDTH // 2
LRU_BLOCKS = 8
LRU_BLOCK_DIM = LRU_WIDTH // LRU_BLOCKS
CONV_WIDTH = 4
LRU_C = 8.0
MLA_HEADS = 8
QK_NOPE_DIM = 64
QK_ROPE_DIM = 32
V_HEAD_DIM = (MIX_WIDTH - LRU_WIDTH) // MLA_HEADS
Q_LORA_RANK = D_MODEL // 4
KV_LORA_RANK = D_MODEL // 8
ROPE_THETA = 10000.0
Q_BLOCK = 128
IN_SPLITS = (LRU_WIDTH, LRU_WIDTH, Q_LORA_RANK, KV_LORA_RANK, QK_ROPE_DIM)
IN_WIDTH = sum(IN_SPLITS)
PEER_HEADS = 8
PEER_N_KEYS = 128
PEER_N_EXPERTS = PEER_N_KEYS * PEER_N_KEYS
PEER_HALF_DIM = 128
PEER_QUERY_DIM = 2 * PEER_HALF_DIM
PEER_TOPK = 16
PEER_TOKEN_CHUNK = 128
EPS = 1e-6

kernel_name = "hymba_rglru_mla_peer_adaln_block"


def rmsnorm(x, g):
    xf = x.astype(jnp.float32)
    y = xf * lax.rsqrt(jnp.mean(xf * xf, axis=-1, keepdims=True) + EPS)
    return (y * g.astype(jnp.float32)).astype(x.dtype)


def modulate(h, shift, scale):
    return h * (1.0 + scale[:, None, :]) + shift[:, None, :]


def split_cols(x, sizes):
    outs, start = [], 0
    for s in sizes:
        outs.append(x[..., start:start + s])
        start += s
    return outs


def apply_rope(x, cos, sin):
    xf = x.astype(jnp.float32)
    half = xf.shape[-1] // 2
    x1, x2 = xf[..., :half], xf[..., half:]
    return jnp.concatenate([x1 * cos - x2 * sin, x2 * cos + x1 * sin], axis=-1).astype(x.dtype)


def causal_depthwise_conv(x, w, b):
    S = x.shape[1]
    xp = jnp.pad(x, ((0, 0), (CONV_WIDTH - 1, 0), (0, 0)))
    y = b
    for k in range(CONV_WIDTH):
        y = y + w[k] * xp[:, k:k + S, :]
    return y


def rg_lru(xc, wa, ba, wx, bx, lam):
    B, S, _ = xc.shape
    xb = xc.reshape(B, S, LRU_BLOCKS, LRU_BLOCK_DIM)
    r = jax.nn.sigmoid(jnp.einsum('bshi,hij->bshj', xb, wa).reshape(B, S, LRU_WIDTH) + ba)
    i = jax.nn.sigmoid(jnp.einsum('bshi,hij->bshj', xb, wx).reshape(B, S, LRU_WIDTH) + bx)
    log_a = -LRU_C * r.astype(jnp.float32) * jax.nn.softplus(-lam.astype(jnp.float32))
    a = jnp.exp(log_a)
    b = jnp.sqrt(-jnp.expm1(2.0 * log_a)) * (i * xc).astype(jnp.float32)

    def combine(left, right):
        a_l, b_l = left
        a_r, b_r = right
        return a_l * a_r, a_r * b_l + b_r

    _, h = lax.associative_scan(combine, (a, b), axis=1)
    return h.astype(xc.dtype)


def mla_attention(q_lat, kv_lat, k_rope, positions, q_norm_g, w_uq, kv_norm_g, w_ukv):
    B, S, _ = q_lat.shape
    q = (rmsnorm(q_lat, q_norm_g) @ w_uq).reshape(B, S, MLA_HEADS, QK_NOPE_DIM + QK_ROPE_DIM)
    q_nope, q_rope = q[..., :QK_NOPE_DIM], q[..., QK_NOPE_DIM:]
    kv = (rmsnorm(kv_lat, kv_norm_g) @ w_ukv).reshape(B, S, MLA_HEADS, QK_NOPE_DIM + V_HEAD_DIM)
    k_nope, v = kv[..., :QK_NOPE_DIM], kv[..., QK_NOPE_DIM:]

    inv_freq = 1.0 / (ROPE_THETA ** (jnp.arange(0, QK_ROPE_DIM, 2, dtype=jnp.float32) / QK_ROPE_DIM))
    ang = positions.astype(jnp.float32)[..., None] * inv_freq
    cos, sin = jnp.cos(ang), jnp.sin(ang)
    q_rope = apply_rope(q_rope, cos[:, :, None, :], sin[:, :, None, :])
    k_rope = apply_rope(k_rope, cos, sin)
    scale = (QK_NOPE_DIM + QK_ROPE_DIM) ** -0.5

    nb = S // Q_BLOCK
    qn_blocks = q_nope.reshape(B, nb, Q_BLOCK, MLA_HEADS, QK_NOPE_DIM).transpose(1, 0, 2, 3, 4)
    qr_blocks = q_rope.reshape(B, nb, Q_BLOCK, MLA_HEADS, QK_ROPE_DIM).transpose(1, 0, 2, 3, 4)
    starts = jnp.arange(nb, dtype=jnp.int32) * Q_BLOCK
    key_idx = jnp.arange(S, dtype=jnp.int32)

    def attend_block(args):
        qn, qr, start = args
        s = (jnp.einsum('bqhd,bkhd->bhqk', qn, k_nope)
             + jnp.einsum('bqhd,bkd->bhqk', qr, k_rope)).astype(jnp.float32) * scale
        q_idx = start + jnp.arange(Q_BLOCK, dtype=jnp.int32)
        mask = key_idx[None, :] <= q_idx[:, None]
        s = jnp.where(mask[None, None], s, -jnp.inf)
        p = jax.nn.softmax(s, axis=-1).astype(v.dtype)
        return jnp.einsum('bhqk,bkhd->bqhd', p, v)

    out = lax.map(attend_block, (qn_blocks, qr_blocks, starts))
    return out.transpose(1, 0, 2, 3, 4).reshape(B, S, MLA_HEADS * V_HEAD_DIM)


def hybrid_mixer(h, positions, w_in, conv_w, conv_b, lru_wa, lru_ba, lru_wx, lru_bx, lru_lambda,
                 q_norm_g, w_uq, kv_norm_g, w_ukv, lru_out_g, mla_out_g, w_out):
    proj = h @ w_in
    x_lru, gate_lru, q_lat, kv_lat, k_rope = split_cols(proj, IN_SPLITS)
    xc = causal_depthwise_conv(x_lru, conv_w, conv_b)
    y_lru = jax.nn.gelu(gate_lru) * rg_lru(xc, lru_wa, lru_ba, lru_wx, lru_bx, lru_lambda)
    y_mla = mla_attention(q_lat, kv_lat, k_rope, positions, q_norm_g, w_uq, kv_norm_g, w_ukv)
    y = jnp.concatenate([rmsnorm(y_lru, lru_out_g), rmsnorm(y_mla, mla_out_g)], axis=-1)
    return y @ w_out


def peer_ffn(h, wq, keys, u, v):
    B, S, D = h.shape
    T = B * S
    hf = h.reshape(T, D)
    q = (hf @ wq).reshape(T, PEER_HEADS, 2, PEER_HALF_DIM).astype(jnp.float32)
    s = jnp.einsum('thpd,hpkd->thpk', q, keys.astype(jnp.float32))
    s_top, i_top = lax.top_k(s, PEER_TOPK)
    cand_s = (s_top[:, :, 0, :, None] + s_top[:, :, 1, None, :]).reshape(T, PEER_HEADS, PEER_TOPK * PEER_TOPK)
    cand_i = (i_top[:, :, 0, :, None] * PEER_N_KEYS + i_top[:, :, 1, None, :]).reshape(T, PEER_HEADS, PEER_TOPK * PEER_TOPK)
    best_s, pos = lax.top_k(cand_s, PEER_TOPK)
    idx = jnp.take_along_axis(cand_i, pos, axis=-1)
    g = jax.nn.softmax(best_s, axis=-1).astype(h.dtype)

    nc = T // PEER_TOKEN_CHUNK
    hc = hf.reshape(nc, PEER_TOKEN_CHUNK, D)
    ic = idx.reshape(nc, PEER_TOKEN_CHUNK, PEER_HEADS, PEER_TOPK)
    gc = g.reshape(nc, PEER_TOKEN_CHUNK, PEER_HEADS, PEER_TOPK)

    def expert_chunk(args):
        xc, ii, gg = args
        act = jax.nn.gelu(jnp.einsum('chkd,cd->chk', u[ii], xc))
        return jnp.einsum('chk,chkd->cd', gg * act, v[ii])

    out = lax.map(expert_chunk, (hc, ic, gc))
    return out.reshape(B, S, D)


def setup_inputs(seed: int = 0) -> dict:
    key = jax.random.key(seed)
    ks = jax.random.split(key, 32)
    L, D = DEPTH, D_MODEL

    def nrm(k, shape, scale):
        return jax.random.normal(k, shape, jnp.float32) * scale

    def gain(k, shape):
        return 1.0 + 0.02 * jax.random.normal(k, shape, jnp.float32)

    x = nrm(ks[0], (BATCH, SEQ, D), 1.0)
    c = nrm(ks[1], (BATCH, D), 1.0)
    positions = (jax.random.randint(ks[2], (BATCH, 1), 0, 4096, dtype=jnp.int32)
                 + jnp.arange(SEQ, dtype=jnp.int32)[None, :])
    a0 = jax.random.uniform(ks[13], (L, LRU_WIDTH), jnp.float32, minval=0.9, maxval=0.999) ** (1.0 / LRU_C)
    return {
        "x": x,
        "c": c,
        "positions": positions,
        "w_ada": nrm(ks[3], (L, D, 6 * D), 0.5 * D ** -0.5),
        "b_ada": nrm(ks[4], (L, 6 * D), 0.01),
        "norm1_g": gain(ks[5], (L, D)),
        "w_in": nrm(ks[6], (L, D, IN_WIDTH), D ** -0.5),
        "conv_w": nrm(ks[7], (L, CONV_WIDTH, LRU_WIDTH), CONV_WIDTH ** -0.5),
        "conv_b": nrm(ks[8], (L, LRU_WIDTH), 0.01),
        "lru_wa": nrm(ks[9], (L, LRU_BLOCKS, LRU_BLOCK_DIM, LRU_BLOCK_DIM), LRU_BLOCK_DIM ** -0.5),
        "lru_ba": nrm(ks[10], (L, LRU_WIDTH), 0.01),
        "lru_wx": nrm(ks[11], (L, LRU_BLOCKS, LRU_BLOCK_DIM, LRU_BLOCK_DIM), LRU_BLOCK_DIM ** -0.5),
        "lru_bx": nrm(ks[12], (L, LRU_WIDTH), 0.01),
        "lru_lambda": jnp.log(a0) - jnp.log1p(-a0),
        "q_norm_g": gain(ks[14], (L, Q_LORA_RANK)),
        "w_uq": nrm(ks[15], (L, Q_LORA_RANK, MLA_HEADS * (QK_NOPE_DIM + QK_ROPE_DIM)), Q_LORA_RANK ** -0.5),
        "kv_norm_g": gain(ks[16], (L, KV_LORA_RANK)),
        "w_ukv": nrm(ks[17], (L, KV_LORA_RANK, MLA_HEADS * (QK_NOPE_DIM + V_HEAD_DIM)), KV_LORA_RANK ** -0.5),
        "lru_out_g": gain(ks[18], (L, LRU_WIDTH)),
        "mla_out_g": gain(ks[19], (L, MLA_HEADS * V_HEAD_DIM)),
        "w_out": nrm(ks[20], (L, MIX_WIDTH, D), MIX_WIDTH ** -0.5),
        "norm2_g": gain(ks[21], (L, D)),
        "peer_wq": nrm(ks[22], (L, D, PEER_HEADS * PEER_QUERY_DIM), D ** -0.5),
        "peer_keys": nrm(ks[23], (L, PEER_HEADS, 2, PEER_N_KEYS, PEER_HALF_DIM), PEER_HALF_DIM ** -0.5),
        "peer_u": nrm(ks[24], (L, PEER_N_EXPERTS, D), D ** -0.5),
        "peer_v": nrm(ks[25], (L, PEER_N_EXPERTS, D), PEER_HEADS ** -0.5),
        "final_g": gain(ks[26], (D,)),
    }


def reference(x, c, positions, w_ada, b_ada, norm1_g, w_in, conv_w, conv_b, lru_wa, lru_ba,
              lru_wx, lru_bx, lru_lambda, q_norm_g, w_uq, kv_norm_g, w_ukv, lru_out_g, mla_out_g,
              w_out, norm2_g, peer_wq, peer_keys, peer_u, peer_v, final_g):
    c_act = jax.nn.silu(c)
    for l in range(DEPTH):
        mod = c_act @ w_ada[l] + b_ada[l]
        shift1, scale1, gate1, shift2, scale2, gate2 = jnp.split(mod, 6, axis=-1)
        h = modulate(rmsnorm(x, norm1_g[l]), shift1, scale1)
        mix = hybrid_mixer(h, positions, w_in[l], conv_w[l], conv_b[l], lru_wa[l], lru_ba[l],
                           lru_wx[l], lru_bx[l], lru_lambda[l], q_norm_g[l], w_uq[l],
                           kv_norm_g[l], w_ukv[l], lru_out_g[l], mla_out_g[l], w_out[l])
        x = x + gate1[:, None, :] * mix
        h = modulate(rmsnorm(x, norm2_g[l]), shift2, scale2)
        x = x + gate2[:, None, :] * peer_ffn(h, peer_wq[l], peer_keys[l], peer_u[l], peer_v[l])
    return rmsnorm(x, final_g)
```

```python
import functools

import jax
import jax.numpy as jnp
from jax import lax
from jax.experimental import pallas as pl
from jax.experimental.pallas import tpu as pltpu

F32 = jnp.float32
BF16 = jnp.bfloat16

LRU_BLOCKS = 8
CONV_WIDTH = 4
LRU_C = 8.0
MLA_HEADS = 8
QK_NOPE = 64
QK_ROPE = 32
V_DIM = 64
ROPE_THETA = 10000.0
PEER_HEADS = 8
PEER_KEYS = 128
PEER_HALF = 128
PEER_TOPK = 16
EPS = 1e-6

LANES = 128
SUBLANES = 8
VMEM_LIMIT = 56 * 1024 * 1024

TM = 512
TS = 256
TQ = 256
TN_ROUTE = 256
TN = 512
EB = 512


def _cparams(*sem):
    return pltpu.CompilerParams(dimension_semantics=sem, vmem_limit_bytes=VMEM_LIMIT)


def _rms(x, g):
    return x * lax.rsqrt(jnp.mean(x * x, axis=-1, keepdims=True) + EPS) * g


def _dot(a, b):
    return jnp.dot(a, b, preferred_element_type=F32)


def _ada_kernel(c_ref, w_ref, b_ref, o_ref):
    ca = jax.nn.silu(c_ref[...])
    o_ref[...] = jnp.dot(ca, w_ref[...], preferred_element_type=F32,
                         precision=lax.Precision.HIGHEST) + b_ref[...]


def _ada(c, w, b):
    bsz, d = c.shape
    n = w.shape[1]
    return pl.pallas_call(
        _ada_kernel,
        grid=(n // d,),
        in_specs=[pl.BlockSpec((bsz, d), lambda j: (0, 0)),
                  pl.BlockSpec((d, d), lambda j: (0, j)),
                  pl.BlockSpec((1, d), lambda j: (0, j))],
        out_specs=pl.BlockSpec((bsz, d), lambda j: (0, j)),
        out_shape=jax.ShapeDtypeStruct((bsz, n), F32),
        compiler_params=_cparams("arbitrary"),
        name="ada",
    )(c, w, b.reshape(1, n))


def _pre_kernel(x_ref, mod_ref, g1_ref, pos_ref, invf_ref, wlru_ref, wq_ref, wkv_ref, wkr_ref,
                qg_ref, wuq_ref, kvg_ref, wuk_ref, wuv_ref, lru_ref, q_ref, k_ref, v_ref):
    d = x_ref.shape[-1]
    x = x_ref[0]
    shift = mod_ref[0, 0:1, :]
    scale = mod_ref[0, 1:2, :]
    hb = (_rms(x, g1_ref[...]) * (1.0 + scale) + shift).astype(BF16)
    lru_ref[0] = _dot(hb, wlru_ref[...])

    ang = pos_ref[0].astype(F32) * invf_ref[...]
    cos = jnp.cos(ang)
    sin = jnp.sin(ang)

    qn = _rms(_dot(hb, wq_ref[...]), qg_ref[...]).astype(BF16)
    q2 = _dot(qn, wuq_ref[...])
    for h in range(MLA_HEADS):
        lo = h * LANES
        q_ref[0, :, lo:lo + LANES] = (q2[:, lo:lo + LANES] * cos
                                      + q2[:, d + lo:d + lo + LANES] * sin).astype(BF16)

    kvn = _rms(_dot(hb, wkv_ref[...]), kvg_ref[...]).astype(BF16)
    kn = _dot(kvn, wuk_ref[...])
    v_ref[0] = _dot(kvn, wuv_ref[...]).astype(BF16)
    kr2 = _dot(hb, wkr_ref[...])
    krot = kr2[:, :LANES] * cos + kr2[:, LANES:] * sin
    for h in range(MLA_HEADS):
        lo = h * LANES
        k_ref[0, :, lo:lo + LANES] = (kn[:, lo:lo + LANES] + krot).astype(BF16)


def _pre(x, mod3, g1, pos3, invf, wlru, wq, wkv, wkr2, qg, wuq2, kvg, wuk, wuv):
    bsz, s, d = x.shape
    full = lambda a: pl.BlockSpec(a.shape, lambda b, i: (0,) * a.ndim)
    tok = lambda w: pl.BlockSpec((1, TM, w), lambda b, i: (b, i, 0))
    return pl.pallas_call(
        _pre_kernel,
        grid=(bsz, s // TM),
        in_specs=[tok(d), pl.BlockSpec((1, 6, d), lambda b, i: (b, 0, 0)), full(g1), tok(1), full(invf),
                  full(wlru), full(wq), full(wkv), full(wkr2), full(qg), full(wuq2), full(kvg),
                  full(wuk), full(wuv)],
        out_specs=[tok(d), tok(d), tok(d), tok(d // 2)],
        out_shape=[jax.ShapeDtypeStruct((bsz, s, d), F32),
                   jax.ShapeDtypeStruct((bsz, s, d), BF16),
                   jax.ShapeDtypeStruct((bsz, s, d), BF16),
                   jax.ShapeDtypeStruct((bsz, s, d // 2), BF16)],
        compiler_params=_cparams("arbitrary", "arbitrary"),
        name="pre",
    )(x, mod3, g1, pos3, invf, wlru, wq, wkv, wkr2, qg, wuq2, kvg, wuk, wuv)


def _lru_kernel(x_ref, cw_ref, cb_ref, wg_ref, bg_ref, lam_ref, og_ref, o_ref, prev_ref, h_ref):
    w = o_ref.shape[-1]

    @pl.when(pl.program_id(1) == 0)
    def _():
        prev_ref[...] = jnp.zeros_like(prev_ref)
        h_ref[...] = jnp.zeros_like(h_ref)

    x = x_ref[0, :, :w]
    gate = x_ref[0, :, w:]
    prev = prev_ref[...]
    row = lax.broadcasted_iota(jnp.int32, x.shape, 0)

    xc = cb_ref[...]
    for k in range(CONV_WIDTH - 1):
        back = CONV_WIDTH - 1 - k
        shifted = jnp.where(row < back, pltpu.roll(prev, back, 0), pltpu.roll(x, back, 0))
        xc = xc + cw_ref[k:k + 1, :] * shifted
    xc = xc + cw_ref[CONV_WIDTH - 1:CONV_WIDTH, :] * x
    prev_ref[...] = x

    pre = _dot(xc.astype(BF16), wg_ref[...]) + bg_ref[...]
    r = jax.nn.sigmoid(pre[:, :w])
    i = jax.nn.sigmoid(pre[:, w:])
    log_a = -LRU_C * r * jax.nn.softplus(-lam_ref[...])
    a = jnp.exp(log_a)
    b = jnp.sqrt(1.0 - a * a) * (i * xc)

    dist = 1
    while dist < TS:
        keep = row >= dist
        b = jnp.where(keep, a * pltpu.roll(b, dist, 0) + b, b)
        a = jnp.where(keep, a * pltpu.roll(a, dist, 0), a)
        dist *= 2
    h = b + a * h_ref[0:1, :]
    h_ref[...] = jnp.broadcast_to(h[TS - 1:TS, :], h_ref.shape)

    y = jax.nn.gelu(gate) * h
    o_ref[0] = _rms(y, og_ref[...]).astype(BF16)


def _lru(lru_in, cw, cb, wg, bg, lam, og):
    bsz, s, d = lru_in.shape
    w = d // 2
    full = lambda a: pl.BlockSpec(a.shape, lambda b, t: (0,) * a.ndim)
    return pl.pallas_call(
        _lru_kernel,
        grid=(bsz, s // TS),
        in_specs=[pl.BlockSpec((1, TS, d), lambda b, t: (b, t, 0)),
                  full(cw), full(cb), full(wg), full(bg), full(lam), full(og)],
        out_specs=pl.BlockSpec((1, TS, w), lambda b, t: (b, t, 0)),
        out_shape=jax.ShapeDtypeStruct((bsz, s, w), BF16),
        scratch_shapes=[pltpu.VMEM((TS, w), F32), pltpu.VMEM((SUBLANES, w), F32)],
        compiler_params=_cparams("arbitrary", "arbitrary"),
        name="lru",
    )(lru_in, cw, cb, wg, bg, lam, og)


def _attn_kernel(q_ref, k_ref, v_ref, g_ref, o_ref):
    i = pl.program_id(1)
    scale = (QK_NOPE + QK_ROPE) ** -0.5
    nt = (((1,), (1,)), ((), ()))
    row = lax.broadcasted_iota(jnp.int32, (TQ, TQ), 0)
    col = lax.broadcasted_iota(jnp.int32, (TQ, TQ), 1)
    lane = lax.broadcasted_iota(jnp.int32, (TQ, LANES), 1)

    def head(h):
        lo = h * LANES
        vlo = (h // 2) * LANES
        q = q_ref[0, :, lo:lo + LANES]

        def step(j, carry, masked):
            m, l, acc = carry
            start = pl.multiple_of(j * TQ, TQ)
            kb = k_ref[0, pl.ds(start, TQ), lo:lo + LANES]
            vb = v_ref[0, pl.ds(start, TQ), vlo:vlo + LANES]
            s = lax.dot_general(q, kb, nt, preferred_element_type=F32) * scale
            if masked:
                s = jnp.where(col <= row, s, -jnp.inf)
            m_new = jnp.maximum(m, jnp.max(s, axis=-1, keepdims=True))
            alpha = jnp.exp(m - m_new)
            p = jnp.exp(s - m_new)
            l = alpha * l + jnp.sum(p, axis=-1, keepdims=True)
            acc = alpha * acc + _dot(p.astype(BF16), vb)
            return m_new, l, acc

        init = (jnp.full((TQ, 1), -jnp.inf, F32), jnp.zeros((TQ, 1), F32), jnp.zeros((TQ, LANES), F32))
        carry = lax.fori_loop(0, i, lambda j, c: step(j, c, False), init)
        _, l, acc = step(i, carry, True)
        return acc / l

    pairs = []
    for hp in range(MLA_HEADS // 2):
        pairs.append(jnp.where(lane < V_DIM, head(2 * hp), head(2 * hp + 1)))
    y = jnp.concatenate(pairs, axis=-1)
    o_ref[0] = _rms(y, g_ref[...]).astype(BF16)


def _attn(q, k, v, g):
    bsz, s, d = q.shape
    w = v.shape[-1]
    return pl.pallas_call(
        _attn_kernel,
        grid=(bsz, s // TQ),
        in_specs=[pl.BlockSpec((1, TQ, d), lambda b, i: (b, i, 0)),
                  pl.BlockSpec((1, s, d), lambda b, i: (b, 0, 0)),
                  pl.BlockSpec((1, s, w), lambda b, i: (b, 0, 0)),
                  pl.BlockSpec(g.shape, lambda b, i: (0, 0))],
        out_specs=pl.BlockSpec((1, TQ, w), lambda b, i: (b, i, 0)),
        out_shape=jax.ShapeDtypeStruct((bsz, s, w), BF16),
        compiler_params=_cparams("arbitrary", "arbitrary"),
        name="attn",
    )(q, k, v, g)


def _post_kernel(yl_ref, ym_ref, x_ref, mod_ref, wol_ref, wom_ref, g2_ref, x1_ref, h2t_ref):
    mix = _dot(yl_ref[0], wol_ref[...]) + _dot(ym_ref[0], wom_ref[...])
    x1 = x_ref[0] + mod_ref[0, 2:3, :] * mix
    x1_ref[0] = x1
    h2 = _rms(x1, g2_ref[...]) * (1.0 + mod_ref[0, 4:5, :]) + mod_ref[0, 3:4, :]
    h2t_ref[...] = h2.T.astype(BF16)


def _post(yl, ym, x, mod3, wol, wom, g2):
    bsz, s, d = x.shape
    w = yl.shape[-1]
    nt = s // TM
    full = lambda a: pl.BlockSpec(a.shape, lambda b, i: (0,) * a.ndim)
    tok = lambda c: pl.BlockSpec((1, TM, c), lambda b, i: (b, i, 0))
    return pl.pallas_call(
        _post_kernel,
        grid=(bsz, nt),
        in_specs=[tok(w), tok(w), tok(d), pl.BlockSpec((1, 6, d), lambda b, i: (b, 0, 0)),
                  full(wol), full(wom), full(g2)],
        out_specs=[tok(d), pl.BlockSpec((d, TM), lambda b, i: (0, b * nt + i))],
        out_shape=[jax.ShapeDtypeStruct((bsz, s, d), F32),
                   jax.ShapeDtypeStruct((d, bsz * s), BF16)],
        compiler_params=_cparams("arbitrary", "arbitrary"),
        name="post",
    )(yl, ym, x, mod3, wol, wom, g2)


def _sort_pairs(n):
    pairs = []

    def merge(lo, hi, r):
        step = r * 2
        if step < hi - lo:
            merge(lo, hi, step)
            merge(lo + r, hi, step)
            pairs.extend((i, i + r) for i in range(lo + r, hi - r, step))
        else:
            pairs.append((lo, lo + r))

    def sort(lo, hi):
        if hi - lo >= 1:
            mid = lo + (hi - lo) // 2
            sort(lo, mid)
            sort(mid + 1, hi)
            merge(lo, hi, 1)

    sort(0, n - 1)
    return pairs


def _sort_desc(xs):
    xs = list(xs)
    for i, j in _sort_pairs(len(xs)):
        a, b = xs[i], xs[j]
        xs[i], xs[j] = jnp.maximum(a, b), jnp.minimum(a, b)
    return xs


def _merge_top(a, b):
    n = len(a)
    xs = [jnp.maximum(a[i], b[n - 1 - i]) for i in range(n)]
    dist = n // 2
    while dist >= 1:
        for i in range(n):
            if i & dist == 0:
                p, q = xs[i], xs[i + dist]
                xs[i], xs[i + dist] = jnp.maximum(p, q), jnp.minimum(p, q)
        dist //= 2
    return xs


def _kth_largest(vals, k):
    n = 1
    while n < len(vals):
        n *= 2
    present = [True] * len(vals) + [False] * (n - len(vals))
    ops = []
    for i, j in _sort_pairs(n):
        if present[i] and present[j]:
            ops.append(("cmp", i, j))
        elif present[j]:
            ops.append(("mov", i, j))
            present[i], present[j] = True, False
    need = {k}
    live = []
    for op in reversed(ops):
        kind, i, j = op
        if kind == "mov":
            if i in need:
                need.discard(i)
                need.add(j)
                live.append((kind, i, j, True, False))
        else:
            hi, lo = i in need, j in need
            if hi or lo:
                need.update((i, j))
                live.append((kind, i, j, hi, lo))
    xs = list(vals) + [None] * (n - len(vals))
    for kind, i, j, hi, lo in reversed(live):
        a, b = xs[i], xs[j]
        if kind == "mov":
            xs[i], xs[j] = b, None
        else:
            xs[i] = jnp.maximum(a, b) if hi else None
            xs[j] = jnp.minimum(a, b) if lo else None
    return xs[k]


def _top_values(slabs, n):
    groups = [_sort_desc(slabs[g:g + n]) for g in range(0, len(slabs), n)]
    while len(groups) > 1:
        groups = [_merge_top(groups[g], groups[g + 1]) for g in range(0, len(groups), 2)]
    return groups[0]


def _route_kernel(h2t_ref, wqt_ref, kbig_ref, keys2_ref, s1_ref, w1_ref, s2_ref, w2_ref, tau_ref):
    tn = h2t_ref.shape[-1]
    nk = PEER_KEYS
    half_rows = PEER_HEADS * PEER_HALF
    qt = _dot(wqt_ref[...], h2t_ref[...])
    sub = lax.broadcasted_iota(jnp.int32, (SUBLANES, tn), 0)

    sc1 = _dot(kbig_ref[...], qt[:half_rows, :].astype(BF16))
    slabs = [sc1[PEER_HEADS * k:PEER_HEADS * (k + 1), :] for k in range(nk)]
    for k in range(nk):
        s1_ref[k] = slabs[k]
    a = _top_values(slabs, PEER_TOPK)

    b = [None] * PEER_TOPK
    for h in range(PEER_HEADS):
        r0 = half_rows + h * PEER_HALF
        sc = _dot(keys2_ref[h], qt[r0:r0 + PEER_HALF, :].astype(BF16))
        s2_ref[h] = sc
        xs = _sort_desc([sc[SUBLANES * t:SUBLANES * (t + 1), :] for t in range(nk // SUBLANES)])
        shift = SUBLANES // 2
        while shift >= 1:
            xs = _merge_top(xs, [pltpu.roll(v, shift, 0) for v in xs])
            shift //= 2
        for r in range(PEER_TOPK):
            b[r] = xs[r] if h == 0 else jnp.where(sub == h, xs[r], b[r])

    cands = [a[i] + b[j] for i in range(PEER_TOPK) for j in range(PEER_TOPK) if (i + 1) * (j + 1) <= PEER_TOPK]
    tau = _kth_largest(cands, PEER_TOPK - 1)
    top = a[0] + b[0]
    z = jnp.zeros_like(tau)
    for c in cands:
        z = z + jnp.where(c >= tau, jnp.exp(c - top), 0.0)
    tau_ref[...] = tau
    inv_z = 1.0 / z
    for k in range(nk):
        w1_ref[k] = jnp.exp(slabs[k] - a[0]) * inv_z
    for h in range(PEER_HEADS):
        w2_ref[h] = jnp.exp(s2_ref[h] - b[0][h:h + 1, :])


def _route(h2t, wqt, kbig, keys2):
    d, t = h2t.shape
    tn = TN_ROUTE
    by_key = pl.BlockSpec((PEER_KEYS, PEER_HEADS, tn), lambda i: (0, 0, i))
    by_head = pl.BlockSpec((PEER_HEADS, PEER_KEYS, tn), lambda i: (0, 0, i))
    return pl.pallas_call(
        _route_kernel,
        grid=(t // tn,),
        in_specs=[pl.BlockSpec((d, tn), lambda i: (0, i)),
                  pl.BlockSpec(wqt.shape, lambda i: (0, 0)),
                  pl.BlockSpec(kbig.shape, lambda i: (0, 0)),
                  pl.BlockSpec(keys2.shape, lambda i: (0, 0, 0))],
        out_specs=[by_key, by_key, by_head, by_head, pl.BlockSpec((PEER_HEADS, tn), lambda i: (0, i))],
        out_shape=[jax.ShapeDtypeStruct((PEER_KEYS, PEER_HEADS, t), F32),
                   jax.ShapeDtypeStruct((PEER_KEYS, PEER_HEADS, t), F32),
                   jax.ShapeDtypeStruct((PEER_HEADS, PEER_KEYS, t), F32),
                   jax.ShapeDtypeStruct((PEER_HEADS, PEER_KEYS, t), F32),
                   jax.ShapeDtypeStruct((PEER_HEADS, t), F32)],
        compiler_params=_cparams("arbitrary"),
        name="route",
    )(h2t, wqt, kbig, keys2)


def _dense_kernel(h2t_ref, u_ref, vt_ref, s1_ref, w1_ref, s2_ref, w2_ref, tau_ref, x1_ref, mod_ref,
                  fg_ref, o_ref, acc_ref, act_ref, gat_ref):
    j = pl.program_id(1)
    tn = h2t_ref.shape[-1]
    eb = u_ref.shape[0]
    rows_per_step = eb // PEER_KEYS

    @pl.when(j == 0)
    def _():
        acc_ref[...] = jnp.zeros_like(acc_ref)

    act_ref[...] = _dot(u_ref[...], h2t_ref[...])

    def per_row(il, carry):
        i1 = j * rows_per_step + il
        base = pl.multiple_of(il * PEER_KEYS, PEER_KEYS)
        for lb in range(tn // LANES):
            ls = slice(lb * LANES, (lb + 1) * LANES)
            bc = lambda v: jnp.broadcast_to(v, (SUBLANES, LANES))
            taub = [bc(tau_ref[h:h + 1, ls]) for h in range(PEER_HEADS)]
            s1b = [bc(s1_ref[i1, h:h + 1, ls]) for h in range(PEER_HEADS)]
            w1b = [bc(w1_ref[i1, h:h + 1, ls]) for h in range(PEER_HEADS)]
            for ib in range(PEER_KEYS // (2 * SUBLANES)):
                halves = []
                for half in range(2):
                    r0 = (2 * ib + half) * SUBLANES
                    g = jnp.zeros((SUBLANES, LANES), F32)
                    for h in range(PEER_HEADS):
                        s2 = s2_ref[h, r0:r0 + SUBLANES, ls]
                        w2 = w2_ref[h, r0:r0 + SUBLANES, ls]
                        g = g + jnp.where(s1b[h] + s2 >= taub[h], w2, 0.0) * w1b[h]
                    act = act_ref[pl.ds(base + r0, SUBLANES), ls]
                    halves.append(jax.nn.gelu(act) * g)
                gat_ref[pl.ds(base + 2 * ib * SUBLANES, 2 * SUBLANES), ls] = (
                    jnp.concatenate(halves, axis=0).astype(BF16))
        return carry

    lax.fori_loop(0, rows_per_step, per_row, 0)
    acc_ref[...] += _dot(vt_ref[0], gat_ref[...])

    @pl.when(j == pl.num_programs(1) - 1)
    def _():
        x2 = x1_ref[...] + mod_ref[0, 5:6, :] * acc_ref[...].T
        o_ref[...] = _rms(x2, fg_ref[...])


def _dense(h2t, u, vt3, s1, w1, s2, w2, tau, x1, mod3, fg, seq):
    d, t = h2t.shape
    ne = vt3.shape[0]
    per_seq = seq // TN
    by_key = pl.BlockSpec((PEER_KEYS, PEER_HEADS, TN), lambda i, j: (0, 0, i))
    by_head = pl.BlockSpec((PEER_HEADS, PEER_KEYS, TN), lambda i, j: (0, 0, i))
    return pl.pallas_call(
        _dense_kernel,
        grid=(t // TN, ne),
        in_specs=[pl.BlockSpec((d, TN), lambda i, j: (0, i)),
                  pl.BlockSpec((EB, d), lambda i, j: (j, 0)),
                  pl.BlockSpec((1, d, EB), lambda i, j: (j, 0, 0)),
                  by_key, by_key, by_head, by_head,
                  pl.BlockSpec((PEER_HEADS, TN), lambda i, j: (0, i)),
                  pl.BlockSpec((TN, d), lambda i, j: (i, 0)),
                  pl.BlockSpec((1, 6, d), lambda i, j: (i // per_seq, 0, 0)),
                  pl.BlockSpec(fg.shape, lambda i, j: (0, 0))],
        out_specs=pl.BlockSpec((TN, d), lambda i, j: (i, 0)),
        out_shape=jax.ShapeDtypeStruct((t, d), F32),
        scratch_shapes=[pltpu.VMEM((d, TN), F32), pltpu.VMEM((EB, TN), F32), pltpu.VMEM((EB, TN), BF16)],
        compiler_params=_cparams("arbitrary", "arbitrary"),
        name="dense",
    )(h2t, u, vt3, s1, w1, s2, w2, tau, x1, mod3, fg)


def _rot_half_cols(w):
    half = w.shape[-1] // 2
    return jnp.concatenate([-w[..., half:], w[..., :half]], axis=-1)


def _pad_cols(w, before, total):
    return jnp.pad(w, [(0, 0)] * (w.ndim - 1) + [(before, total - before - w.shape[-1])])


def kernel(x, c, positions, w_ada, b_ada, norm1_g, w_in, conv_w, conv_b, lru_wa, lru_ba, lru_wx, lru_bx,
           lru_lambda, q_norm_g, w_uq, kv_norm_g, w_ukv, lru_out_g, mla_out_g, w_out, norm2_g, peer_wq,
           peer_keys, peer_u, peer_v, final_g):
    bsz, seq, d = x.shape
    depth = w_ada.shape[0]
    lw = d // 2
    qr = w_uq.shape[1]
    kvr = w_ukv.shape[1]
    assert depth == 1, "the final norm is fused into the (single) layer's PEER kernel"
    assert seq % TM == 0 and seq % TN == 0 and seq % TQ == 0 and seq % TS == 0
    assert d == MLA_HEADS * LANES and lw == MLA_HEADS * V_DIM

    inv_freq = 1.0 / (ROPE_THETA ** (jnp.arange(0, QK_ROPE, 2, dtype=F32) / QK_ROPE))
    invf = _pad_cols(jnp.concatenate([inv_freq, inv_freq])[None, :], QK_NOPE, LANES)
    pos3 = positions.reshape(bsz, seq, 1)
    row = lambda v: v.reshape(1, -1)

    for l in range(depth):
        mod3 = _ada(c, w_ada[l], b_ada[l]).reshape(bsz, 6, d)

        wi = w_in[l]
        o0, o1, o2, o3 = 2 * lw, 2 * lw + qr, 2 * lw + qr + kvr, 2 * lw + qr + kvr + QK_ROPE
        wkr = wi[:, o2:o3]
        wkr2 = jnp.concatenate([_pad_cols(wkr, QK_NOPE, LANES),
                                _pad_cols(_rot_half_cols(wkr), QK_NOPE, LANES)], axis=1).astype(BF16)
        uq = w_uq[l].reshape(qr, MLA_HEADS, QK_NOPE + QK_ROPE)
        uq_main = _pad_cols(uq, 0, LANES).reshape(qr, d)
        uq_rot = _pad_cols(_rot_half_cols(uq[..., QK_NOPE:]), QK_NOPE, LANES).reshape(qr, d)
        wuq2 = jnp.concatenate([uq_main, uq_rot], axis=1).astype(BF16)
        ukv = w_ukv[l].reshape(kvr, MLA_HEADS, QK_NOPE + V_DIM)
        wuk = _pad_cols(ukv[..., :QK_NOPE], 0, LANES).reshape(kvr, d).astype(BF16)
        wuv = ukv[..., QK_NOPE:].reshape(kvr, lw).astype(BF16)

        lru_in, q, k, v = _pre(x, mod3, row(norm1_g[l]), pos3, invf, wi[:, :o0].astype(BF16),
                               wi[:, o0:o1].astype(BF16), wi[:, o1:o2].astype(BF16), wkr2,
                               row(q_norm_g[l]), wuq2, row(kv_norm_g[l]), wuk, wuv)

        eye = jnp.eye(LRU_BLOCKS, dtype=F32)
        blockdiag = lambda w: jnp.einsum("hij,hg->higj", w, eye).reshape(lw, lw)
        wg = jnp.concatenate([blockdiag(lru_wa[l]), blockdiag(lru_wx[l])], axis=1).astype(BF16)
        bg = jnp.concatenate([lru_ba[l], lru_bx[l]])[None, :]
        yl = _lru(lru_in, conv_w[l], row(conv_b[l]), wg, bg, row(lru_lambda[l]), row(lru_out_g[l]))

        ym = _attn(q, k, v, row(mla_out_g[l]))

        wo = w_out[l].astype(BF16)
        x1, h2t = _post(yl, ym, x, mod3, wo[:lw], wo[lw:], row(norm2_g[l]))

        wqt = (peer_wq[l].reshape(d, PEER_HEADS, 2, PEER_HALF).transpose(2, 1, 3, 0)
               .reshape(2 * PEER_HEADS * PEER_HALF, d).astype(BF16))
        keys = peer_keys[l].astype(BF16)
        kbig = jnp.einsum("hkd,hg->khgd", keys[:, 0], jnp.eye(PEER_HEADS, dtype=BF16)).reshape(
            PEER_KEYS * PEER_HEADS, PEER_HEADS * PEER_HALF)
        s1, w1, s2, w2, tau = _route(h2t, wqt, kbig, keys[:, 1])

        ne = peer_u.shape[1] // EB
        vt3 = peer_v[l].astype(BF16).reshape(ne, EB, d).transpose(0, 2, 1)
        out = _dense(h2t, peer_u[l].astype(BF16), vt3, s1, w1, s2, w2, tau,
                     x1.reshape(bsz * seq, d), mod3, row(final_g), seq)
        x = out.reshape(bsz, seq, d)
    return x
```

```python
import functools

import jax
import jax.numpy as jnp
from jax import lax
from jax.experimental import pallas as pl
from jax.experimental.pallas import tpu as pltpu

F32 = jnp.float32
BF16 = jnp.bfloat16

LRU_BLOCKS = 8
CONV_WIDTH = 4
LRU_C = 8.0
MLA_HEADS = 8
QK_NOPE = 64
QK_ROPE = 32
V_DIM = 64
ROPE_THETA = 10000.0
PEER_HEADS = 8
PEER_KEYS = 128
PEER_HALF = 128
PEER_TOPK = 16
EPS = 1e-6

LANES = 128
SUBLANES = 8
VMEM_LIMIT = 56 * 1024 * 1024

TM = 512
TS = 256
TQ = 256
TN_ROUTE = 256
TN = 512
EB = 512


def _cparams(*sem, flags=None):
    return pltpu.CompilerParams(dimension_semantics=sem, vmem_limit_bytes=VMEM_LIMIT, flags=flags)


def _rms(x, g):
    return x * lax.rsqrt(jnp.mean(x * x, axis=-1, keepdims=True) + EPS) * g


def _dot(a, b):
    return jnp.dot(a, b, preferred_element_type=F32)


def _ada_kernel(c_ref, w_ref, b_ref, o_ref):
    ca = jax.nn.silu(c_ref[...])
    o_ref[...] = jnp.dot(ca, w_ref[...], preferred_element_type=F32,
                         precision=lax.Precision.HIGHEST) + b_ref[...]


def _ada(c, w, b):
    bsz, d = c.shape
    n = w.shape[1]
    return pl.pallas_call(
        _ada_kernel,
        grid=(n // d,),
        in_specs=[pl.BlockSpec((bsz, d), lambda j: (0, 0)),
                  pl.BlockSpec((d, d), lambda j: (0, j)),
                  pl.BlockSpec((1, d), lambda j: (0, j))],
        out_specs=pl.BlockSpec((bsz, d), lambda j: (0, j)),
        out_shape=jax.ShapeDtypeStruct((bsz, n), F32),
        compiler_params=_cparams("arbitrary"),
        name="ada",
    )(c, w, b.reshape(1, n))


def _pre_kernel(x_ref, mod_ref, g1_ref, pos_ref, invf_ref, wlru_ref, wq_ref, wkv_ref, wkr_ref,
                qg_ref, wuq_ref, kvg_ref, wuk_ref, wuv_ref, lru_ref, q_ref, k_ref, v_ref):
    d = x_ref.shape[-1]
    x = x_ref[0]
    shift = mod_ref[0, 0:1, :]
    scale = mod_ref[0, 1:2, :]
    hb = (_rms(x, g1_ref[...]) * (1.0 + scale) + shift).astype(BF16)
    lru_ref[0] = _dot(hb, wlru_ref[...])

    ang = pos_ref[0].astype(F32) * invf_ref[...]
    cos = jnp.cos(ang)
    sin = jnp.sin(ang)

    qn = _rms(_dot(hb, wq_ref[...]), qg_ref[...]).astype(BF16)
    q2 = _dot(qn, wuq_ref[...])
    for h in range(MLA_HEADS):
        lo = h * LANES
        q_ref[0, :, lo:lo + LANES] = (q2[:, lo:lo + LANES] * cos
                                      + q2[:, d + lo:d + lo + LANES] * sin).astype(BF16)

    kvn = _rms(_dot(hb, wkv_ref[...]), kvg_ref[...]).astype(BF16)
    kn = _dot(kvn, wuk_ref[...])
    v_ref[0] = _dot(kvn, wuv_ref[...]).astype(BF16)
    kr2 = _dot(hb, wkr_ref[...])
    krot = kr2[:, :LANES] * cos + kr2[:, LANES:] * sin
    for h in range(MLA_HEADS):
        lo = h * LANES
        k_ref[0, :, lo:lo + LANES] = (kn[:, lo:lo + LANES] + krot).astype(BF16)


def _pre(x, mod3, g1, pos3, invf, wlru, wq, wkv, wkr2, qg, wuq2, kvg, wuk, wuv):
    bsz, s, d = x.shape
    full = lambda a: pl.BlockSpec(a.shape, lambda b, i: (0,) * a.ndim)
    tok = lambda w: pl.BlockSpec((1, TM, w), lambda b, i: (b, i, 0))
    return pl.pallas_call(
        _pre_kernel,
        grid=(bsz, s // TM),
        in_specs=[tok(d), pl.BlockSpec((1, 6, d), lambda b, i: (b, 0, 0)), full(g1), tok(1), full(invf),
                  full(wlru), full(wq), full(wkv), full(wkr2), full(qg), full(wuq2), full(kvg),
                  full(wuk), full(wuv)],
        out_specs=[tok(d), tok(d), tok(d), tok(d // 2)],
        out_shape=[jax.ShapeDtypeStruct((bsz, s, d), F32),
                   jax.ShapeDtypeStruct((bsz, s, d), BF16),
                   jax.ShapeDtypeStruct((bsz, s, d), BF16),
                   jax.ShapeDtypeStruct((bsz, s, d // 2), BF16)],
        compiler_params=_cparams("arbitrary", "arbitrary"),
        name="pre",
    )(x, mod3, g1, pos3, invf, wlru, wq, wkv, wkr2, qg, wuq2, kvg, wuk, wuv)


def _lru_kernel(x_ref, cw_ref, cb_ref, wg_ref, bg_ref, lam_ref, og_ref, o_ref, prev_ref, h_ref):
    w = o_ref.shape[-1]

    @pl.when(pl.program_id(1) == 0)
    def _():
        prev_ref[...] = jnp.zeros_like(prev_ref)
        h_ref[...] = jnp.zeros_like(h_ref)

    x = x_ref[0, :, :w]
    gate = x_ref[0, :, w:]
    prev = prev_ref[...]
    row = lax.broadcasted_iota(jnp.int32, x.shape, 0)

    xc = cb_ref[...]
    for k in range(CONV_WIDTH - 1):
        back = CONV_WIDTH - 1 - k
        shifted = jnp.where(row < back, pltpu.roll(prev, back, 0), pltpu.roll(x, back, 0))
        xc = xc + cw_ref[k:k + 1, :] * shifted
    xc = xc + cw_ref[CONV_WIDTH - 1:CONV_WIDTH, :] * x
    prev_ref[...] = x

    pre = _dot(xc.astype(BF16), wg_ref[...]) + bg_ref[...]
    r = jax.nn.sigmoid(pre[:, :w])
    i = jax.nn.sigmoid(pre[:, w:])
    log_a = -LRU_C * r * jax.nn.softplus(-lam_ref[...])
    a = jnp.exp(log_a)
    b = jnp.sqrt(1.0 - a * a) * (i * xc)

    dist = 1
    while dist < TS:
        keep = row >= dist
        b = jnp.where(keep, a * pltpu.roll(b, dist, 0) + b, b)
        a = jnp.where(keep, a * pltpu.roll(a, dist, 0), a)
        dist *= 2
    h = b + a * h_ref[0:1, :]
    h_ref[...] = jnp.broadcast_to(h[TS - 1:TS, :], h_ref.shape)

    y = jax.nn.gelu(gate) * h
    o_ref[0] = _rms(y, og_ref[...]).astype(BF16)


def _lru(lru_in, cw, cb, wg, bg, lam, og):
    bsz, s, d = lru_in.shape
    w = d // 2
    full = lambda a: pl.BlockSpec(a.shape, lambda b, t: (0,) * a.ndim)
    return pl.pallas_call(
        _lru_kernel,
        grid=(bsz, s // TS),
        in_specs=[pl.BlockSpec((1, TS, d), lambda b, t: (b, t, 0)),
                  full(cw), full(cb), full(wg), full(bg), full(lam), full(og)],
        out_specs=pl.BlockSpec((1, TS, w), lambda b, t: (b, t, 0)),
        out_shape=jax.ShapeDtypeStruct((bsz, s, w), BF16),
        scratch_shapes=[pltpu.VMEM((TS, w), F32), pltpu.VMEM((SUBLANES, w), F32)],
        compiler_params=_cparams("arbitrary", "arbitrary"),
        name="lru",
    )(lru_in, cw, cb, wg, bg, lam, og)


def _attn_kernel(q_ref, k_ref, v_ref, g_ref, o_ref):
    i = pl.program_id(1)
    scale = (QK_NOPE + QK_ROPE) ** -0.5
    nt = (((1,), (1,)), ((), ()))
    row = lax.broadcasted_iota(jnp.int32, (TQ, TQ), 0)
    col = lax.broadcasted_iota(jnp.int32, (TQ, TQ), 1)
    lane = lax.broadcasted_iota(jnp.int32, (TQ, LANES), 1)

    def head(h):
        lo = h * LANES
        vlo = (h // 2) * LANES
        q = q_ref[0, :, lo:lo + LANES]

        def step(j, carry, masked):
            m, l, acc = carry
            start = pl.multiple_of(j * TQ, TQ)
            kb = k_ref[0, pl.ds(start, TQ), lo:lo + LANES]
            vb = v_ref[0, pl.ds(start, TQ), vlo:vlo + LANES]
            s = lax.dot_general(q, kb, nt, preferred_element_type=F32) * scale
            if masked:
                s = jnp.where(col <= row, s, -jnp.inf)
            m_new = jnp.maximum(m, jnp.max(s, axis=-1, keepdims=True))
            alpha = jnp.exp(m - m_new)
            p = jnp.exp(s - m_new)
            l = alpha * l + jnp.sum(p, axis=-1, keepdims=True)
            acc = alpha * acc + _dot(p.astype(BF16), vb)
            return m_new, l, acc

        init = (jnp.full((TQ, 1), -jnp.inf, F32), jnp.zeros((TQ, 1), F32), jnp.zeros((TQ, LANES), F32))
        carry = lax.fori_loop(0, i, lambda j, c: step(j, c, False), init)
        _, l, acc = step(i, carry, True)
        return acc / l

    pairs = []
    for hp in range(MLA_HEADS // 2):
        pairs.append(jnp.where(lane < V_DIM, head(2 * hp), head(2 * hp + 1)))
    y = jnp.concatenate(pairs, axis=-1)
    o_ref[0] = _rms(y, g_ref[...]).astype(BF16)


def _attn(q, k, v, g):
    bsz, s, d = q.shape
    w = v.shape[-1]
    return pl.pallas_call(
        _attn_kernel,
        grid=(bsz, s // TQ),
        in_specs=[pl.BlockSpec((1, TQ, d), lambda b, i: (b, i, 0)),
                  pl.BlockSpec((1, s, d), lambda b, i: (b, 0, 0)),
                  pl.BlockSpec((1, s, w), lambda b, i: (b, 0, 0)),
                  pl.BlockSpec(g.shape, lambda b, i: (0, 0))],
        out_specs=pl.BlockSpec((1, TQ, w), lambda b, i: (b, i, 0)),
        out_shape=jax.ShapeDtypeStruct((bsz, s, w), BF16),
        compiler_params=_cparams("arbitrary", "arbitrary"),
        name="attn",
    )(q, k, v, g)


def _post_kernel(yl_ref, ym_ref, x_ref, mod_ref, wol_ref, wom_ref, g2_ref, x1_ref, h2t_ref):
    mix = _dot(yl_ref[0], wol_ref[...]) + _dot(ym_ref[0], wom_ref[...])
    x1 = x_ref[0] + mod_ref[0, 2:3, :] * mix
    x1_ref[0] = x1
    h2 = _rms(x1, g2_ref[...]) * (1.0 + mod_ref[0, 4:5, :]) + mod_ref[0, 3:4, :]
    h2t_ref[...] = h2.T.astype(BF16)


def _post(yl, ym, x, mod3, wol, wom, g2):
    bsz, s, d = x.shape
    w = yl.shape[-1]
    nt = s // TM
    full = lambda a: pl.BlockSpec(a.shape, lambda b, i: (0,) * a.ndim)
    tok = lambda c: pl.BlockSpec((1, TM, c), lambda b, i: (b, i, 0))
    return pl.pallas_call(
        _post_kernel,
        grid=(bsz, nt),
        in_specs=[tok(w), tok(w), tok(d), pl.BlockSpec((1, 6, d), lambda b, i: (b, 0, 0)),
                  full(wol), full(wom), full(g2)],
        out_specs=[tok(d), pl.BlockSpec((d, TM), lambda b, i: (0, b * nt + i))],
        out_shape=[jax.ShapeDtypeStruct((bsz, s, d), F32),
                   jax.ShapeDtypeStruct((d, bsz * s), BF16)],
        compiler_params=_cparams("arbitrary", "arbitrary"),
        name="post",
    )(yl, ym, x, mod3, wol, wom, g2)


def _sort_pairs(n):
    pairs = []

    def merge(lo, hi, r):
        step = r * 2
        if step < hi - lo:
            merge(lo, hi, step)
            merge(lo + r, hi, step)
            pairs.extend((i, i + r) for i in range(lo + r, hi - r, step))
        else:
            pairs.append((lo, lo + r))

    def sort(lo, hi):
        if hi - lo >= 1:
            mid = lo + (hi - lo) // 2
            sort(lo, mid)
            sort(mid + 1, hi)
            merge(lo, hi, 1)

    sort(0, n - 1)
    return pairs


def _sort_desc(xs):
    xs = list(xs)
    for i, j in _sort_pairs(len(xs)):
        a, b = xs[i], xs[j]
        xs[i], xs[j] = jnp.maximum(a, b), jnp.minimum(a, b)
    return xs


def _merge_top(a, b):
    n = len(a)
    xs = [jnp.maximum(a[i], b[n - 1 - i]) for i in range(n)]
    dist = n // 2
    while dist >= 1:
        for i in range(n):
            if i & dist == 0:
                p, q = xs[i], xs[i + dist]
                xs[i], xs[i + dist] = jnp.maximum(p, q), jnp.minimum(p, q)
        dist //= 2
    return xs


def _kth_largest(vals, k):
    n = 1
    while n < len(vals):
        n *= 2
    present = [True] * len(vals) + [False] * (n - len(vals))
    ops = []
    for i, j in _sort_pairs(n):
        if present[i] and present[j]:
            ops.append(("cmp", i, j))
        elif present[j]:
            ops.append(("mov", i, j))
            present[i], present[j] = True, False
    need = {k}
    live = []
    for op in reversed(ops):
        kind, i, j = op
        if kind == "mov":
            if i in need:
                need.discard(i)
                need.add(j)
                live.append((kind, i, j, True, False))
        else:
            hi, lo = i in need, j in need
            if hi or lo:
                need.update((i, j))
                live.append((kind, i, j, hi, lo))
    xs = list(vals) + [None] * (n - len(vals))
    for kind, i, j, hi, lo in reversed(live):
        a, b = xs[i], xs[j]
        if kind == "mov":
            xs[i], xs[j] = b, None
        else:
            xs[i] = jnp.maximum(a, b) if hi else None
            xs[j] = jnp.minimum(a, b) if lo else None
    return xs[k]


def _top_values(slabs, n):
    groups = [_sort_desc(slabs[g:g + n]) for g in range(0, len(slabs), n)]
    while len(groups) > 1:
        groups = [_merge_top(groups[g], groups[g + 1]) for g in range(0, len(groups), 2)]
    return groups[0]


PACK_ROWS = 2 * SUBLANES


def _pair_bits(x):
    bits = pltpu.bitcast(x.astype(BF16).astype(F32), jnp.uint32)
    return (bits & jnp.uint32(0xFFFF0000)) | (bits >> 16)


def _route_kernel(h2t_ref, wqt_ref, kbig_ref, keys2_ref, c_ref, w1_ref, p_ref):
    tn = h2t_ref.shape[-1]
    nk = PEER_KEYS
    half_rows = PEER_HEADS * PEER_HALF
    qt = _dot(wqt_ref[...], h2t_ref[...])
    sub = lax.broadcasted_iota(jnp.int32, (SUBLANES, tn), 0)

    sc1 = _dot(kbig_ref[...], qt[:half_rows, :].astype(BF16))
    slabs = [sc1[PEER_HEADS * k:PEER_HEADS * (k + 1), :] for k in range(nk)]
    a = _top_values(slabs, PEER_TOPK)

    b = [None] * PEER_TOPK
    for h in range(PEER_HEADS):
        r0 = half_rows + h * PEER_HALF
        sc = _dot(keys2_ref[h], qt[r0:r0 + PEER_HALF, :].astype(BF16))
        tiles = [sc[SUBLANES * t:SUBLANES * (t + 1), :] for t in range(nk // SUBLANES)]
        xs = _sort_desc(tiles)
        shift = SUBLANES // 2
        while shift >= 1:
            xs = _merge_top(xs, [pltpu.roll(v, shift, 0) for v in xs])
            shift //= 2
        for ib in range(nk // PACK_ROWS):
            ranks, weights = [], []
            for tile in tiles[2 * ib:2 * ib + 2]:
                rk = jnp.full_like(tile, float(PEER_TOPK))
                for r in reversed(range(PEER_TOPK)):
                    rk = jnp.where(tile >= xs[r], float(r), rk)
                ranks.append(rk)
                weights.append(jnp.exp(tile - xs[0]))
            rk16 = jnp.concatenate(ranks, axis=0).astype(BF16)
            w16 = jnp.concatenate(weights, axis=0).astype(BF16)
            for lb in range(tn // LANES):
                r0 = 2 * h * PACK_ROWS
                p_ref[lb, ib, r0:r0 + PACK_ROWS, :] = rk16[:, lb * LANES:(lb + 1) * LANES]
                p_ref[lb, ib, r0 + PACK_ROWS:r0 + 2 * PACK_ROWS, :] = w16[:, lb * LANES:(lb + 1) * LANES]
        for r in range(PEER_TOPK):
            b[r] = xs[r] if h == 0 else jnp.where(sub == h, xs[r], b[r])

    pairs = [(i, j) for i in range(PEER_TOPK) for j in range(PEER_TOPK) if (i + 1) * (j + 1) <= PEER_TOPK]
    cands = {ij: a[ij[0]] + b[ij[1]] for ij in pairs}
    tau = _kth_largest(list(cands.values()), PEER_TOPK - 1)
    top = a[0] + b[0]
    z = jnp.zeros_like(tau)
    for c in cands.values():
        z = z + jnp.where(c >= tau, jnp.exp(c - top), 0.0)
    inv_z = 1.0 / z
    need = []
    for j in range(PEER_TOPK):
        t = None
        for i in range(PEER_TOPK):
            if (i, j) in cands:
                v = jnp.where(cands[(i, j)] >= tau, a[i], jnp.inf)
                t = v if t is None else jnp.minimum(t, v)
        need.append(t)
    for k in range(nk):
        cnt = jnp.zeros_like(tau)
        for j in range(PEER_TOPK):
            cnt = jnp.where(slabs[k] >= need[j], float(j + 1), cnt)
        c_ref[k] = _pair_bits(cnt)
        w1_ref[k] = _pair_bits(jnp.exp(slabs[k] - a[0]) * inv_z)


def _route(h2t, wqt, kbig, keys2):
    d, t = h2t.shape
    tn = TN_ROUTE
    by_key = pl.BlockSpec((PEER_KEYS, PEER_HEADS, tn), lambda i: (0, 0, i))
    packed = (PEER_KEYS // PACK_ROWS, 2 * PEER_HEADS * PACK_ROWS, LANES)
    return pl.pallas_call(
        _route_kernel,
        grid=(t // tn,),
        in_specs=[pl.BlockSpec((d, tn), lambda i: (0, i)),
                  pl.BlockSpec(wqt.shape, lambda i: (0, 0)),
                  pl.BlockSpec(kbig.shape, lambda i: (0, 0)),
                  pl.BlockSpec(keys2.shape, lambda i: (0, 0, 0))],
        out_specs=[by_key, by_key, pl.BlockSpec((tn // LANES,) + packed, lambda i: (i, 0, 0, 0))],
        out_shape=[jax.ShapeDtypeStruct((PEER_KEYS, PEER_HEADS, t), jnp.uint32),
                   jax.ShapeDtypeStruct((PEER_KEYS, PEER_HEADS, t), jnp.uint32),
                   jax.ShapeDtypeStruct((t // LANES,) + packed, BF16)],
        compiler_params=_cparams("arbitrary"),
        name="route",
    )(h2t, wqt, kbig, keys2)


def _dense_kernel(flag_ref, h2t_ref, u_ref, vt_ref, c_ref, w1_ref, p_ref, x1_ref, mod_ref,
                  fg_ref, o_ref, acc_ref, act_ref, gat_ref):
    j = pl.program_id(1)
    nblk = pl.num_programs(1) - 2
    tn = h2t_ref.shape[-1]
    eb = u_ref.shape[0]
    rows_per_step = eb // PEER_KEYS

    @pl.when(j == 0)
    def _():
        acc_ref[...] = jnp.zeros_like(acc_ref)
        act_ref[1] = jnp.zeros(act_ref.shape[1:], F32)
        gat_ref[0] = jnp.zeros(gat_ref.shape[1:], BF16)

    jb = jnp.clip(j - 1, 0, nblk - 1)
    n_lb = tn // LANES

    def row_tile(ref, i1, h, ls):
        return pltpu.bitcast(jnp.broadcast_to(ref[i1, h:h + 1, ls], (SUBLANES, LANES)), BF16)

    def gate_rows(il, oth, matmul_chunk):
        i1 = jb * rows_per_step + il
        base = il * PEER_KEYS
        for lb in range(n_lb):
            matmul_chunk(lb)
            ls = slice(lb * LANES, (lb + 1) * LANES)
            cntb = [row_tile(c_ref, i1, h, ls) for h in range(PEER_HEADS)]
            w1b = [row_tile(w1_ref, i1, h, ls) for h in range(PEER_HEADS)]
            for ib in range(PEER_KEYS // PACK_ROWS):
                g = None
                for h in range(PEER_HEADS):
                    r0 = 2 * h * PACK_ROWS
                    rank2 = p_ref[lb, ib, r0:r0 + PACK_ROWS, :]
                    w2 = p_ref[lb, ib, r0 + PACK_ROWS:r0 + 2 * PACK_ROWS, :]
                    term = jnp.minimum(jnp.maximum(cntb[h] - rank2, 0.0), w2) * w1b[h]
                    g = term if g is None else g + term
                r1 = base + ib * PACK_ROWS
                act = act_ref[oth, r1:r1 + PACK_ROWS, ls]
                gat_ref[oth, r1:r1 + PACK_ROWS, ls] = jax.nn.gelu(act).astype(BF16) * g

    n_sec = rows_per_step
    lane_parts = n_sec // 2
    part = tn // lane_parts

    def sections(cur, oth):
        for sec in range(n_sec):
            @pl.when(flag_ref[0] == 0)
            def _(sec=sec):
                cs = slice((sec % lane_parts) * part, (sec % lane_parts + 1) * part)

                def matmul_chunk(lb):
                    if sec < lane_parts:
                        rows = acc_ref.shape[0] // n_lb
                        rs = slice(lb * rows, (lb + 1) * rows)
                        acc_ref[rs, cs] += _dot(vt_ref[0, rs, :], gat_ref[cur, :, cs])
                    else:
                        rows = eb // n_lb
                        rs = slice(lb * rows, (lb + 1) * rows)
                        act_ref[cur, rs, cs] = _dot(u_ref[rs, :], h2t_ref[:, cs])

                gate_rows(sec, oth, matmul_chunk)

    for parity in range(2):
        @pl.when(j % 2 == parity)
        def _(parity=parity):
            sections(parity, 1 - parity)

    @pl.when(j == pl.num_programs(1) - 1)
    def _():
        x2 = x1_ref[...] + mod_ref[0, 5:6, :] * acc_ref[...].T
        o_ref[...] = _rms(x2, fg_ref[...])


def _dense(h2t, u, vt3, cnt, w1, packed, x1, mod3, fg, seq):
    d, t = h2t.shape
    ne = vt3.shape[0]
    per_seq = seq // TN
    by_key = pl.BlockSpec((PEER_KEYS, PEER_HEADS, TN), lambda i, j: (0, 0, i))
    return pl.pallas_call(
        _dense_kernel,
        grid=(t // TN, ne + 2),
        in_specs=[pl.BlockSpec(memory_space=pltpu.SMEM),
                  pl.BlockSpec((d, TN), lambda i, j: (0, i)),
                  pl.BlockSpec((EB, d), lambda i, j: (jnp.minimum(j, ne - 1), 0)),
                  pl.BlockSpec((1, d, EB), lambda i, j: (jnp.clip(j - 2, 0, ne - 1), 0, 0)),
                  by_key, by_key,
                  pl.BlockSpec((TN // LANES,) + packed.shape[1:], lambda i, j: (i, 0, 0, 0)),
                  pl.BlockSpec((TN, d), lambda i, j: (i, 0)),
                  pl.BlockSpec((1, 6, d), lambda i, j: (i // per_seq, 0, 0)),
                  pl.BlockSpec(fg.shape, lambda i, j: (0, 0))],
        out_specs=pl.BlockSpec((TN, d), lambda i, j: (i, 0)),
        out_shape=jax.ShapeDtypeStruct((t, d), F32),
        scratch_shapes=[pltpu.VMEM((d, TN), F32), pltpu.VMEM((2, EB, TN), F32),
                        pltpu.VMEM((2, EB, TN), BF16)],
        compiler_params=_cparams("arbitrary", "arbitrary"),
        name="dense",
    )(jnp.zeros((1,), jnp.int32), h2t, u, vt3, cnt, w1, packed, x1, mod3, fg)


def _rot_half_cols(w):
    half = w.shape[-1] // 2
    return jnp.concatenate([-w[..., half:], w[..., :half]], axis=-1)


def _pad_cols(w, before, total):
    return jnp.pad(w, [(0, 0)] * (w.ndim - 1) + [(before, total - before - w.shape[-1])])


def kernel(x, c, positions, w_ada, b_ada, norm1_g, w_in, conv_w, conv_b, lru_wa, lru_ba, lru_wx, lru_bx,
           lru_lambda, q_norm_g, w_uq, kv_norm_g, w_ukv, lru_out_g, mla_out_g, w_out, norm2_g, peer_wq,
           peer_keys, peer_u, peer_v, final_g):
    bsz, seq, d = x.shape
    depth = w_ada.shape[0]
    lw = d // 2
    qr = w_uq.shape[1]
    kvr = w_ukv.shape[1]
    assert depth == 1, "the final norm is fused into the (single) layer's PEER kernel"
    assert seq % TM == 0 and seq % TN == 0 and seq % TQ == 0 and seq % TS == 0
    assert d == MLA_HEADS * LANES and lw == MLA_HEADS * V_DIM

    inv_freq = 1.0 / (ROPE_THETA ** (jnp.arange(0, QK_ROPE, 2, dtype=F32) / QK_ROPE))
    invf = _pad_cols(jnp.concatenate([inv_freq, inv_freq])[None, :], QK_NOPE, LANES)
    pos3 = positions.reshape(bsz, seq, 1)
    row = lambda v: v.reshape(1, -1)

    for l in range(depth):
        mod3 = _ada(c, w_ada[l], b_ada[l]).reshape(bsz, 6, d)

        wi = w_in[l]
        o0, o1, o2, o3 = 2 * lw, 2 * lw + qr, 2 * lw + qr + kvr, 2 * lw + qr + kvr + QK_ROPE
        wkr = wi[:, o2:o3]
        wkr2 = jnp.concatenate([_pad_cols(wkr, QK_NOPE, LANES),
                                _pad_cols(_rot_half_cols(wkr), QK_NOPE, LANES)], axis=1).astype(BF16)
        uq = w_uq[l].reshape(qr, MLA_HEADS, QK_NOPE + QK_ROPE)
        uq_main = _pad_cols(uq, 0, LANES).reshape(qr, d)
        uq_rot = _pad_cols(_rot_half_cols(uq[..., QK_NOPE:]), QK_NOPE, LANES).reshape(qr, d)
        wuq2 = jnp.concatenate([uq_main, uq_rot], axis=1).astype(BF16)
        ukv = w_ukv[l].reshape(kvr, MLA_HEADS, QK_NOPE + V_DIM)
        wuk = _pad_cols(ukv[..., :QK_NOPE], 0, LANES).reshape(kvr, d).astype(BF16)
        wuv = ukv[..., QK_NOPE:].reshape(kvr, lw).astype(BF16)

        lru_in, q, k, v = _pre(x, mod3, row(norm1_g[l]), pos3, invf, wi[:, :o0].astype(BF16),
                               wi[:, o0:o1].astype(BF16), wi[:, o1:o2].astype(BF16), wkr2,
                               row(q_norm_g[l]), wuq2, row(kv_norm_g[l]), wuk, wuv)

        eye = jnp.eye(LRU_BLOCKS, dtype=F32)
        blockdiag = lambda w: jnp.einsum("hij,hg->higj", w, eye).reshape(lw, lw)
        wg = jnp.concatenate([blockdiag(lru_wa[l]), blockdiag(lru_wx[l])], axis=1).astype(BF16)
        bg = jnp.concatenate([lru_ba[l], lru_bx[l]])[None, :]
        yl = _lru(lru_in, conv_w[l], row(conv_b[l]), wg, bg, row(lru_lambda[l]), row(lru_out_g[l]))

        ym = _attn(q, k, v, row(mla_out_g[l]))

        wo = w_out[l].astype(BF16)
        x1, h2t = _post(yl, ym, x, mod3, wo[:lw], wo[lw:], row(norm2_g[l]))

        wqt = (peer_wq[l].reshape(d, PEER_HEADS, 2, PEER_HALF).transpose(2, 1, 3, 0)
               .reshape(2 * PEER_HEADS * PEER_HALF, d).astype(BF16))
        keys = peer_keys[l].astype(BF16)
        kbig = jnp.einsum("hkd,hg->khgd", keys[:, 0], jnp.eye(PEER_HEADS, dtype=BF16)).reshape(
            PEER_KEYS * PEER_HEADS, PEER_HEADS * PEER_HALF)
        cnt, w1, packed = _route(h2t, wqt, kbig, keys[:, 1])

        ne = peer_u.shape[1] // EB
        vt3 = peer_v[l].astype(BF16).reshape(ne, EB, d).transpose(0, 2, 1)
        out = _dense(h2t, peer_u[l].astype(BF16), vt3, cnt, w1, packed,
                     x1.reshape(bsz * seq, d), mod3, row(final_g), seq)
        x = out.reshape(bsz, seq, d)
    return x
```

```python
import functools

import jax
import jax.numpy as jnp
from jax import lax
from jax.experimental import pallas as pl
from jax.experimental.pallas import tpu as pltpu

F32 = jnp.float32
BF16 = jnp.bfloat16

LRU_BLOCKS = 8
CONV_WIDTH = 4
LRU_C = 8.0
MLA_HEADS = 8
QK_NOPE = 64
QK_ROPE = 32
V_DIM = 64
ROPE_THETA = 10000.0
PEER_HEADS = 8
PEER_KEYS = 128
PEER_HALF = 128
PEER_TOPK = 16
EPS = 1e-6
LOG2_E = 1.4426950408889634

LANES = 128
SUBLANES = 8
VMEM_LIMIT = 56 * 1024 * 1024

TM = 512
TS = 256
TQ = 256
TN_ROUTE = 256
TN = 512
EB = 512


def _cparams(*sem, flags=None):
    return pltpu.CompilerParams(dimension_semantics=sem, vmem_limit_bytes=VMEM_LIMIT, flags=flags)


def _rms(x, g):
    return x * lax.rsqrt(jnp.mean(x * x, axis=-1, keepdims=True) + EPS) * g


def _dot(a, b):
    return jnp.dot(a, b, preferred_element_type=F32)


def _ada_kernel(c_ref, w_ref, b_ref, o_ref):
    ca = jax.nn.silu(c_ref[...])
    o_ref[...] = jnp.dot(ca, w_ref[...], preferred_element_type=F32,
                         precision=lax.Precision.HIGHEST) + b_ref[...]


def _ada(c, w, b):
    bsz, d = c.shape
    n = w.shape[1]
    return pl.pallas_call(
        _ada_kernel,
        grid=(n // d,),
        in_specs=[pl.BlockSpec((bsz, d), lambda j: (0, 0)),
                  pl.BlockSpec((d, d), lambda j: (0, j)),
                  pl.BlockSpec((1, d), lambda j: (0, j))],
        out_specs=pl.BlockSpec((bsz, d), lambda j: (0, j)),
        out_shape=jax.ShapeDtypeStruct((bsz, n), F32),
        compiler_params=_cparams("arbitrary"),
        name="ada",
    )(c, w, b.reshape(1, n))


def _pre_kernel(x_ref, mod_ref, g1_ref, pos_ref, invf_ref, wlru_ref, wq_ref, wkv_ref, wkr_ref,
                qg_ref, wuq_ref, kvg_ref, wuk_ref, wuv_ref, vones_ref, lru_ref, q_ref, k_ref, v_ref):
    d = x_ref.shape[-1]
    x = x_ref[0]
    shift = mod_ref[0, 0:1, :]
    scale = mod_ref[0, 1:2, :]
    hb = (_rms(x, g1_ref[...]) * (1.0 + scale) + shift).astype(BF16)
    lru_ref[0] = _dot(hb, wlru_ref[...])

    ang = pos_ref[0].astype(F32) * invf_ref[...]
    cos = jnp.cos(ang)
    sin = jnp.sin(ang)

    qn = _rms(_dot(hb, wq_ref[...]), qg_ref[...]).astype(BF16)
    q2 = _dot(qn, wuq_ref[...])
    q_scale = (QK_NOPE + QK_ROPE) ** -0.5 * LOG2_E
    for h in range(MLA_HEADS):
        lo = h * LANES
        q_ref[0, :, lo:lo + LANES] = ((q2[:, lo:lo + LANES] * cos
                                       + q2[:, d + lo:d + lo + LANES] * sin) * q_scale).astype(BF16)

    kvn = _rms(_dot(hb, wkv_ref[...]), kvg_ref[...]).astype(BF16)
    kn = _dot(kvn, wuk_ref[...])
    v_ref[0] = (_dot(kvn, wuv_ref[...]) + vones_ref[...]).astype(BF16)
    kr2 = _dot(hb, wkr_ref[...])
    krot = kr2[:, :LANES] * cos + kr2[:, LANES:] * sin
    for h in range(MLA_HEADS):
        lo = h * LANES
        k_ref[0, :, lo:lo + LANES] = (kn[:, lo:lo + LANES] + krot).astype(BF16)


def _pre(x, mod3, g1, pos3, invf, wlru, wq, wkv, wkr2, qg, wuq2, kvg, wuk, wuv, vones):
    bsz, s, d = x.shape
    full = lambda a: pl.BlockSpec(a.shape, lambda b, i: (0,) * a.ndim)
    tok = lambda w: pl.BlockSpec((1, TM, w), lambda b, i: (b, i, 0))
    return pl.pallas_call(
        _pre_kernel,
        grid=(bsz, s // TM),
        in_specs=[tok(d), pl.BlockSpec((1, 6, d), lambda b, i: (b, 0, 0)), full(g1), tok(1), full(invf),
                  full(wlru), full(wq), full(wkv), full(wkr2), full(qg), full(wuq2), full(kvg),
                  full(wuk), full(wuv), full(vones)],
        out_specs=[tok(d), tok(d), tok(d), tok(d)],
        out_shape=[jax.ShapeDtypeStruct((bsz, s, d), F32),
                   jax.ShapeDtypeStruct((bsz, s, d), BF16),
                   jax.ShapeDtypeStruct((bsz, s, d), BF16),
                   jax.ShapeDtypeStruct((bsz, s, d), BF16)],
        compiler_params=_cparams("arbitrary", "arbitrary"),
        name="pre",
    )(x, mod3, g1, pos3, invf, wlru, wq, wkv, wkr2, qg, wuq2, kvg, wuk, wuv, vones)


def _lru_kernel(x_ref, cw_ref, cb_ref, wg_ref, bg_ref, lam_ref, og_ref, o_ref, prev_ref, h_ref):
    w = o_ref.shape[-1]

    @pl.when(pl.program_id(1) == 0)
    def _():
        prev_ref[...] = jnp.zeros_like(prev_ref)
        h_ref[...] = jnp.zeros_like(h_ref)

    x = x_ref[0, :, :w]
    gate = x_ref[0, :, w:]
    prev = prev_ref[...]
    row = lax.broadcasted_iota(jnp.int32, x.shape, 0)

    xc = cb_ref[...]
    for k in range(CONV_WIDTH - 1):
        back = CONV_WIDTH - 1 - k
        shifted = jnp.where(row < back, pltpu.roll(prev, back, 0), pltpu.roll(x, back, 0))
        xc = xc + cw_ref[k:k + 1, :] * shifted
    xc = xc + cw_ref[CONV_WIDTH - 1:CONV_WIDTH, :] * x
    prev_ref[...] = x

    pre = _dot(xc.astype(BF16), wg_ref[...]) + bg_ref[...]
    r = jax.nn.sigmoid(pre[:, :w])
    i = jax.nn.sigmoid(pre[:, w:])
    log_a = -LRU_C * r * jax.nn.softplus(-lam_ref[...])
    a = jnp.exp(log_a)
    b = jnp.sqrt(1.0 - a * a) * (i * xc)

    dist = 1
    while dist < TS:
        keep = row >= dist
        b = jnp.where(keep, a * pltpu.roll(b, dist, 0) + b, b)
        a = jnp.where(keep, a * pltpu.roll(a, dist, 0), a)
        dist *= 2
    h = b + a * h_ref[0:1, :]
    h_ref[...] = jnp.broadcast_to(h[TS - 1:TS, :], h_ref.shape)

    y = jax.nn.gelu(gate) * h
    o_ref[0] = _rms(y, og_ref[...]).astype(BF16)


def _lru(lru_in, cw, cb, wg, bg, lam, og):
    bsz, s, d = lru_in.shape
    w = d // 2
    full = lambda a: pl.BlockSpec(a.shape, lambda b, t: (0,) * a.ndim)
    return pl.pallas_call(
        _lru_kernel,
        grid=(bsz, s // TS),
        in_specs=[pl.BlockSpec((1, TS, d), lambda b, t: (b, t, 0)),
                  full(cw), full(cb), full(wg), full(bg), full(lam), full(og)],
        out_specs=pl.BlockSpec((1, TS, w), lambda b, t: (b, t, 0)),
        out_shape=jax.ShapeDtypeStruct((bsz, s, w), BF16),
        scratch_shapes=[pltpu.VMEM((TS, w), F32), pltpu.VMEM((SUBLANES, w), F32)],
        compiler_params=_cparams("arbitrary", "arbitrary"),
        name="lru",
    )(lru_in, cw, cb, wg, bg, lam, og)


def _attn_kernel(q_ref, k_ref, v_ref, g_ref, o_ref, m_ref, acc_ref):
    i = pl.program_id(1)
    nt = (((1,), (1,)), ((), ()))
    row = lax.broadcasted_iota(jnp.int32, (TQ, TQ), 0)
    col = lax.broadcasted_iota(jnp.int32, (TQ, TQ), 1)
    lane = lax.broadcasted_iota(jnp.int32, (TQ, LANES), 1)

    m_ref[...] = jnp.full(m_ref.shape, -jnp.inf, F32)
    acc_ref[...] = jnp.zeros(acc_ref.shape, F32)

    def block(j, masked):
        start = pl.multiple_of(j * TQ, TQ)

        def scores(h):
            lo = h * LANES
            s = lax.dot_general(q_ref[0, :, lo:lo + LANES], k_ref[0, pl.ds(start, TQ), lo:lo + LANES], nt,
                                preferred_element_type=F32)
            return jnp.where(col <= row, s, -jnp.inf) if masked else s

        def softmax_step(h, s):
            m_old = m_ref[h]
            m_new = jnp.maximum(m_old, jnp.max(s, axis=-1, keepdims=True))
            m_ref[h] = m_new
            alpha = jnp.exp2(m_old - m_new)
            p = [jnp.exp2(s[:, c:c + LANES] - m_new) for c in range(0, TQ, LANES)]
            return alpha, jnp.concatenate(p, axis=-1).astype(BF16)

        def weighted_values(h, alpha, p):
            lo = h * LANES
            acc_ref[h] = alpha * acc_ref[h] + _dot(p, v_ref[0, pl.ds(start, TQ), lo:lo + LANES])

        pending_s, pending_p = {}, {}
        for t in range(MLA_HEADS + 2):
            if t < MLA_HEADS:
                pending_s[t] = scores(t)
            if 1 <= t <= MLA_HEADS:
                pending_p[t - 1] = softmax_step(t - 1, pending_s.pop(t - 1))
            if t >= 2:
                weighted_values(t - 2, *pending_p.pop(t - 2))

    def unmasked(j, carry):
        block(j, False)
        return carry

    lax.fori_loop(0, i, unmasked, 0)
    block(i, True)

    pairs = []
    for hp in range(MLA_HEADS // 2):
        even, odd = acc_ref[2 * hp], acc_ref[2 * hp + 1]
        num = jnp.where(lane < V_DIM, even, odd)
        den = jnp.where(lane < V_DIM, pltpu.roll(even, V_DIM, 1), pltpu.roll(odd, V_DIM, 1))
        pairs.append(num / den)
    y = jnp.concatenate(pairs, axis=-1)
    o_ref[0] = _rms(y, g_ref[...]).astype(BF16)


def _attn(q, k, v, g):
    bsz, s, d = q.shape
    w = g.shape[-1]
    return pl.pallas_call(
        _attn_kernel,
        grid=(bsz, s // TQ),
        in_specs=[pl.BlockSpec((1, TQ, d), lambda b, i: (b, i, 0)),
                  pl.BlockSpec((1, s, d), lambda b, i: (b, 0, 0)),
                  pl.BlockSpec((1, s, d), lambda b, i: (b, 0, 0)),
                  pl.BlockSpec(g.shape, lambda b, i: (0, 0))],
        out_specs=pl.BlockSpec((1, TQ, w), lambda b, i: (b, i, 0)),
        out_shape=jax.ShapeDtypeStruct((bsz, s, w), BF16),
        scratch_shapes=[pltpu.VMEM((MLA_HEADS, TQ, LANES), F32), pltpu.VMEM((MLA_HEADS, TQ, LANES), F32)],
        compiler_params=_cparams("arbitrary", "arbitrary"),
        name="attn",
    )(q, k, v, g)


def _post_kernel(yl_ref, ym_ref, x_ref, mod_ref, wol_ref, wom_ref, g2_ref, x1_ref, h2t_ref):
    mix = _dot(yl_ref[0], wol_ref[...]) + _dot(ym_ref[0], wom_ref[...])
    x1 = x_ref[0] + mod_ref[0, 2:3, :] * mix
    x1_ref[0] = x1
    h2 = _rms(x1, g2_ref[...]) * (1.0 + mod_ref[0, 4:5, :]) + mod_ref[0, 3:4, :]
    h2t_ref[...] = h2.T.astype(BF16)


def _post(yl, ym, x, mod3, wol, wom, g2):
    bsz, s, d = x.shape
    w = yl.shape[-1]
    nt = s // TM
    full = lambda a: pl.BlockSpec(a.shape, lambda b, i: (0,) * a.ndim)
    tok = lambda c: pl.BlockSpec((1, TM, c), lambda b, i: (b, i, 0))
    return pl.pallas_call(
        _post_kernel,
        grid=(bsz, nt),
        in_specs=[tok(w), tok(w), tok(d), pl.BlockSpec((1, 6, d), lambda b, i: (b, 0, 0)),
                  full(wol), full(wom), full(g2)],
        out_specs=[tok(d), pl.BlockSpec((d, TM), lambda b, i: (0, b * nt + i))],
        out_shape=[jax.ShapeDtypeStruct((bsz, s, d), F32),
                   jax.ShapeDtypeStruct((d, bsz * s), BF16)],
        compiler_params=_cparams("arbitrary", "arbitrary"),
        name="post",
    )(yl, ym, x, mod3, wol, wom, g2)


def _sort_pairs(n):
    pairs = []

    def merge(lo, hi, r):
        step = r * 2
        if step < hi - lo:
            merge(lo, hi, step)
            merge(lo + r, hi, step)
            pairs.extend((i, i + r) for i in range(lo + r, hi - r, step))
        else:
            pairs.append((lo, lo + r))

    def sort(lo, hi):
        if hi - lo >= 1:
            mid = lo + (hi - lo) // 2
            sort(lo, mid)
            sort(mid + 1, hi)
            merge(lo, hi, 1)

    sort(0, n - 1)
    return pairs


def _sort_desc(xs):
    xs = list(xs)
    for i, j in _sort_pairs(len(xs)):
        a, b = xs[i], xs[j]
        xs[i], xs[j] = jnp.maximum(a, b), jnp.minimum(a, b)
    return xs


def _merge_top(a, b):
    n = len(a)
    xs = [jnp.maximum(a[i], b[n - 1 - i]) for i in range(n)]
    dist = n // 2
    while dist >= 1:
        for i in range(n):
            if i & dist == 0:
                p, q = xs[i], xs[i + dist]
                xs[i], xs[i + dist] = jnp.maximum(p, q), jnp.minimum(p, q)
        dist //= 2
    return xs


def _kth_largest(vals, k):
    n = 1
    while n < len(vals):
        n *= 2
    present = [True] * len(vals) + [False] * (n - len(vals))
    ops = []
    for i, j in _sort_pairs(n):
        if present[i] and present[j]:
            ops.append(("cmp", i, j))
        elif present[j]:
            ops.append(("mov", i, j))
            present[i], present[j] = True, False
    need = {k}
    live = []
    for op in reversed(ops):
        kind, i, j = op
        if kind == "mov":
            if i in need:
                need.discard(i)
                need.add(j)
                live.append((kind, i, j, True, False))
        else:
            hi, lo = i in need, j in need
            if hi or lo:
                need.update((i, j))
                live.append((kind, i, j, hi, lo))
    xs = list(vals) + [None] * (n - len(vals))
    for kind, i, j, hi, lo in reversed(live):
        a, b = xs[i], xs[j]
        if kind == "mov":
            xs[i], xs[j] = b, None
        else:
            xs[i] = jnp.maximum(a, b) if hi else None
            xs[j] = jnp.minimum(a, b) if lo else None
    return xs[k]


def _top_values(slabs, n):
    groups = [_sort_desc(slabs[g:g + n]) for g in range(0, len(slabs), n)]
    while len(groups) > 1:
        groups = [_merge_top(groups[g], groups[g + 1]) for g in range(0, len(groups), 2)]
    return groups[0]


PACK_ROWS = 2 * SUBLANES


def _pair_bits(x):
    bits = pltpu.bitcast(x.astype(BF16).astype(F32), jnp.uint32)
    return (bits & jnp.uint32(0xFFFF0000)) | (bits >> 16)


def _route_kernel(h2t_ref, wqt_ref, kbig_ref, keys2_ref, c_ref, w1_ref, p_ref):
    tn = h2t_ref.shape[-1]
    nk = PEER_KEYS
    half_rows = PEER_HEADS * PEER_HALF
    qt = _dot(wqt_ref[...], h2t_ref[...])
    sub = lax.broadcasted_iota(jnp.int32, (SUBLANES, tn), 0)

    sc1 = _dot(kbig_ref[...], qt[:half_rows, :].astype(BF16))
    slabs = [sc1[PEER_HEADS * k:PEER_HEADS * (k + 1), :] for k in range(nk)]
    a = _top_values(slabs, PEER_TOPK)

    b = [None] * PEER_TOPK
    for h in range(PEER_HEADS):
        r0 = half_rows + h * PEER_HALF
        sc = _dot(keys2_ref[h], qt[r0:r0 + PEER_HALF, :].astype(BF16))
        tiles = [sc[SUBLANES * t:SUBLANES * (t + 1), :] for t in range(nk // SUBLANES)]
        xs = _sort_desc(tiles)
        shift = SUBLANES // 2
        while shift >= 1:
            xs = _merge_top(xs, [pltpu.roll(v, shift, 0) for v in xs])
            shift //= 2
        for ib in range(nk // PACK_ROWS):
            ranks, weights = [], []
            for tile in tiles[2 * ib:2 * ib + 2]:
                rk = jnp.full_like(tile, float(PEER_TOPK))
                for r in reversed(range(PEER_TOPK)):
                    rk = jnp.where(tile >= xs[r], float(r), rk)
                ranks.append(rk)
                weights.append(jnp.exp(tile - xs[0]))
            rk16 = jnp.concatenate(ranks, axis=0).astype(BF16)
            w16 = jnp.concatenate(weights, axis=0).astype(BF16)
            for lb in range(tn // LANES):
                r0 = 2 * h * PACK_ROWS
                p_ref[lb, ib, r0:r0 + PACK_ROWS, :] = rk16[:, lb * LANES:(lb + 1) * LANES]
                p_ref[lb, ib, r0 + PACK_ROWS:r0 + 2 * PACK_ROWS, :] = w16[:, lb * LANES:(lb + 1) * LANES]
        for r in range(PEER_TOPK):
            b[r] = xs[r] if h == 0 else jnp.where(sub == h, xs[r], b[r])

    pairs = [(i, j) for i in range(PEER_TOPK) for j in range(PEER_TOPK) if (i + 1) * (j + 1) <= PEER_TOPK]
    cands = {ij: a[ij[0]] + b[ij[1]] for ij in pairs}
    tau = _kth_largest(list(cands.values()), PEER_TOPK - 1)
    top = a[0] + b[0]
    z = jnp.zeros_like(tau)
    for c in cands.values():
        z = z + jnp.where(c >= tau, jnp.exp(c - top), 0.0)
    inv_z = 1.0 / z
    need = []
    for j in range(PEER_TOPK):
        t = None
        for i in range(PEER_TOPK):
            if (i, j) in cands:
                v = jnp.where(cands[(i, j)] >= tau, a[i], jnp.inf)
                t = v if t is None else jnp.minimum(t, v)
        need.append(t)
    for k in range(nk):
        cnt = jnp.zeros_like(tau)
        for j in range(PEER_TOPK):
            cnt = jnp.where(slabs[k] >= need[j], float(j + 1), cnt)
        c_ref[k] = _pair_bits(cnt)
        w1_ref[k] = _pair_bits(jnp.exp(slabs[k] - a[0]) * inv_z)


def _route(h2t, wqt, kbig, keys2):
    d, t = h2t.shape
    tn = TN_ROUTE
    by_key = pl.BlockSpec((PEER_KEYS, PEER_HEADS, tn), lambda i: (0, 0, i))
    packed = (PEER_KEYS // PACK_ROWS, 2 * PEER_HEADS * PACK_ROWS, LANES)
    return pl.pallas_call(
        _route_kernel,
        grid=(t // tn,),
        in_specs=[pl.BlockSpec((d, tn), lambda i: (0, i)),
                  pl.BlockSpec(wqt.shape, lambda i: (0, 0)),
                  pl.BlockSpec(kbig.shape, lambda i: (0, 0)),
                  pl.BlockSpec(keys2.shape, lambda i: (0, 0, 0))],
        out_specs=[by_key, by_key, pl.BlockSpec((tn // LANES,) + packed, lambda i: (i, 0, 0, 0))],
        out_shape=[jax.ShapeDtypeStruct((PEER_KEYS, PEER_HEADS, t), jnp.uint32),
                   jax.ShapeDtypeStruct((PEER_KEYS, PEER_HEADS, t), jnp.uint32),
                   jax.ShapeDtypeStruct((t // LANES,) + packed, BF16)],
        compiler_params=_cparams("arbitrary"),
        name="route",
    )(h2t, wqt, kbig, keys2)


def _dense_kernel(flag_ref, h2t_ref, u_ref, vt_ref, c_ref, w1_ref, p_ref, x1_ref, mod_ref,
                  fg_ref, o_ref, acc_ref, act_ref, gat_ref):
    j = pl.program_id(1)
    nblk = pl.num_programs(1) - 2
    tn = h2t_ref.shape[-1]
    eb = u_ref.shape[0]
    rows_per_step = eb // PEER_KEYS

    @pl.when(j == 0)
    def _():
        acc_ref[...] = jnp.zeros_like(acc_ref)
        act_ref[1] = jnp.zeros(act_ref.shape[1:], F32)
        gat_ref[0] = jnp.zeros(gat_ref.shape[1:], BF16)

    jb = jnp.clip(j - 1, 0, nblk - 1)
    n_lb = tn // LANES

    def row_tile(ref, i1, h, ls):
        return pltpu.bitcast(jnp.broadcast_to(ref[i1, h:h + 1, ls], (SUBLANES, LANES)), BF16)

    def gate_rows(il, oth, matmul_chunk):
        i1 = jb * rows_per_step + il
        base = il * PEER_KEYS
        for lb in range(n_lb):
            matmul_chunk(lb)
            ls = slice(lb * LANES, (lb + 1) * LANES)
            cntb = [row_tile(c_ref, i1, h, ls) for h in range(PEER_HEADS)]
            w1b = [row_tile(w1_ref, i1, h, ls) for h in range(PEER_HEADS)]
            for ib in range(PEER_KEYS // PACK_ROWS):
                g = None
                for h in range(PEER_HEADS):
                    r0 = 2 * h * PACK_ROWS
                    rank2 = p_ref[lb, ib, r0:r0 + PACK_ROWS, :]
                    w2 = p_ref[lb, ib, r0 + PACK_ROWS:r0 + 2 * PACK_ROWS, :]
                    term = jnp.minimum(jnp.maximum(cntb[h] - rank2, 0.0), w2) * w1b[h]
                    g = term if g is None else g + term
                r1 = base + ib * PACK_ROWS
                act = act_ref[oth, r1:r1 + PACK_ROWS, ls]
                gat_ref[oth, r1:r1 + PACK_ROWS, ls] = jax.nn.gelu(act).astype(BF16) * g

    n_sec = rows_per_step
    lane_parts = n_sec // 2
    part = tn // lane_parts

    def sections(cur, oth):
        for sec in range(n_sec):
            @pl.when(flag_ref[0] == 0)
            def _(sec=sec):
                cs = slice((sec % lane_parts) * part, (sec % lane_parts + 1) * part)

                def matmul_chunk(lb):
                    if sec < lane_parts:
                        rows = acc_ref.shape[0] // n_lb
                        rs = slice(lb * rows, (lb + 1) * rows)
                        acc_ref[rs, cs] += _dot(vt_ref[0, rs, :], gat_ref[cur, :, cs])
                    else:
                        rows = eb // n_lb
                        rs = slice(lb * rows, (lb + 1) * rows)
                        act_ref[cur, rs, cs] = _dot(u_ref[rs, :], h2t_ref[:, cs])

                gate_rows(sec, oth, matmul_chunk)

    for parity in range(2):
        @pl.when(j % 2 == parity)
        def _(parity=parity):
            sections(parity, 1 - parity)

    @pl.when(j == pl.num_programs(1) - 1)
    def _():
        x2 = x1_ref[...] + mod_ref[0, 5:6, :] * acc_ref[...].T
        o_ref[...] = _rms(x2, fg_ref[...])


def _dense(h2t, u, vt3, cnt, w1, packed, x1, mod3, fg, seq):
    d, t = h2t.shape
    ne = vt3.shape[0]
    per_seq = seq // TN
    by_key = pl.BlockSpec((PEER_KEYS, PEER_HEADS, TN), lambda i, j: (0, 0, i))
    return pl.pallas_call(
        _dense_kernel,
        grid=(t // TN, ne + 2),
        in_specs=[pl.BlockSpec(memory_space=pltpu.SMEM),
                  pl.BlockSpec((d, TN), lambda i, j: (0, i)),
                  pl.BlockSpec((EB, d), lambda i, j: (jnp.minimum(j, ne - 1), 0)),
                  pl.BlockSpec((1, d, EB), lambda i, j: (jnp.clip(j - 2, 0, ne - 1), 0, 0)),
                  by_key, by_key,
                  pl.BlockSpec((TN // LANES,) + packed.shape[1:], lambda i, j: (i, 0, 0, 0)),
                  pl.BlockSpec((TN, d), lambda i, j: (i, 0)),
                  pl.BlockSpec((1, 6, d), lambda i, j: (i // per_seq, 0, 0)),
                  pl.BlockSpec(fg.shape, lambda i, j: (0, 0))],
        out_specs=pl.BlockSpec((TN, d), lambda i, j: (i, 0)),
        out_shape=jax.ShapeDtypeStruct((t, d), F32),
        scratch_shapes=[pltpu.VMEM((d, TN), F32), pltpu.VMEM((2, EB, TN), F32),
                        pltpu.VMEM((2, EB, TN), BF16)],
        compiler_params=_cparams("arbitrary", "arbitrary"),
        name="dense",
    )(jnp.zeros((1,), jnp.int32), h2t, u, vt3, cnt, w1, packed, x1, mod3, fg)


def _rot_half_cols(w):
    half = w.shape[-1] // 2
    return jnp.concatenate([-w[..., half:], w[..., :half]], axis=-1)


def _pad_cols(w, before, total):
    return jnp.pad(w, [(0, 0)] * (w.ndim - 1) + [(before, total - before - w.shape[-1])])


def kernel(x, c, positions, w_ada, b_ada, norm1_g, w_in, conv_w, conv_b, lru_wa, lru_ba, lru_wx, lru_bx,
           lru_lambda, q_norm_g, w_uq, kv_norm_g, w_ukv, lru_out_g, mla_out_g, w_out, norm2_g, peer_wq,
           peer_keys, peer_u, peer_v, final_g):
    bsz, seq, d = x.shape
    depth = w_ada.shape[0]
    lw = d // 2
    qr = w_uq.shape[1]
    kvr = w_ukv.shape[1]
    assert depth == 1, "the final norm is fused into the (single) layer's PEER kernel"
    assert seq % TM == 0 and seq % TN == 0 and seq % TQ == 0 and seq % TS == 0
    assert d == MLA_HEADS * LANES and lw == MLA_HEADS * V_DIM

    inv_freq = 1.0 / (ROPE_THETA ** (jnp.arange(0, QK_ROPE, 2, dtype=F32) / QK_ROPE))
    invf = _pad_cols(jnp.concatenate([inv_freq, inv_freq])[None, :], QK_NOPE, LANES)
    pos3 = positions.reshape(bsz, seq, 1)
    row = lambda v: v.reshape(1, -1)

    for l in range(depth):
        mod3 = _ada(c, w_ada[l], b_ada[l]).reshape(bsz, 6, d)

        wi = w_in[l]
        o0, o1, o2, o3 = 2 * lw, 2 * lw + qr, 2 * lw + qr + kvr, 2 * lw + qr + kvr + QK_ROPE
        wkr = wi[:, o2:o3]
        wkr2 = jnp.concatenate([_pad_cols(wkr, QK_NOPE, LANES),
                                _pad_cols(_rot_half_cols(wkr), QK_NOPE, LANES)], axis=1).astype(BF16)
        uq = w_uq[l].reshape(qr, MLA_HEADS, QK_NOPE + QK_ROPE)
        uq_main = _pad_cols(uq, 0, LANES).reshape(qr, d)
        uq_rot = _pad_cols(_rot_half_cols(uq[..., QK_NOPE:]), QK_NOPE, LANES).reshape(qr, d)
        wuq2 = jnp.concatenate([uq_main, uq_rot], axis=1).astype(BF16)
        ukv = w_ukv[l].reshape(kvr, MLA_HEADS, QK_NOPE + V_DIM)
        wuk = _pad_cols(ukv[..., :QK_NOPE], 0, LANES).reshape(kvr, d).astype(BF16)
        uv = ukv[..., QK_NOPE:]
        odd = (jnp.arange(MLA_HEADS) % 2 == 1)[None, :, None]
        wuv = jnp.where(odd, _pad_cols(uv, V_DIM, LANES), _pad_cols(uv, 0, LANES)).reshape(kvr, d).astype(BF16)
        half = jnp.arange(LANES)[None, :] >= V_DIM
        vones = (half != odd[0]).astype(F32).reshape(1, d)

        lru_in, q, k, v = _pre(x, mod3, row(norm1_g[l]), pos3, invf, wi[:, :o0].astype(BF16),
                               wi[:, o0:o1].astype(BF16), wi[:, o1:o2].astype(BF16), wkr2,
                               row(q_norm_g[l]), wuq2, row(kv_norm_g[l]), wuk, wuv, vones)

        eye = jnp.eye(LRU_BLOCKS, dtype=F32)
        blockdiag = lambda w: jnp.einsum("hij,hg->higj", w, eye).reshape(lw, lw)
        wg = jnp.concatenate([blockdiag(lru_wa[l]), blockdiag(lru_wx[l])], axis=1).astype(BF16)
        bg = jnp.concatenate([lru_ba[l], lru_bx[l]])[None, :]
        yl = _lru(lru_in, conv_w[l], row(conv_b[l]), wg, bg, row(lru_lambda[l]), row(lru_out_g[l]))

        ym = _attn(q, k, v, row(mla_out_g[l]))

        wo = w_out[l].astype(BF16)
        x1, h2t = _post(yl, ym, x, mod3, wo[:lw], wo[lw:], row(norm2_g[l]))

        wqt = (peer_wq[l].reshape(d, PEER_HEADS, 2, PEER_HALF).transpose(2, 1, 3, 0)
               .reshape(2 * PEER_HEADS * PEER_HALF, d).astype(BF16))
        keys = peer_keys[l].astype(BF16)
        kbig = jnp.einsum("hkd,hg->khgd", keys[:, 0], jnp.eye(PEER_HEADS, dtype=BF16)).reshape(
            PEER_KEYS * PEER_HEADS, PEER_HEADS * PEER_HALF)
        cnt, w1, packed = _route(h2t, wqt, kbig, keys[:, 1])

        ne = peer_u.shape[1] // EB
        vt3 = peer_v[l].astype(BF16).reshape(ne, EB, d).transpose(0, 2, 1)
        out = _dense(h2t, peer_u[l].astype(BF16), vt3, cnt, w1, packed,
                     x1.reshape(bsz * seq, d), mod3, row(final_g), seq)
        x = out.reshape(bsz, seq, d)
    return x
```

```python
import functools

import jax
import jax.numpy as jnp
from jax import lax
from jax.experimental import pallas as pl
from jax.experimental.pallas import tpu as pltpu

F32 = jnp.float32
BF16 = jnp.bfloat16

LRU_BLOCKS = 8
CONV_WIDTH = 4
LRU_C = 8.0
MLA_HEADS = 8
QK_NOPE = 64
QK_ROPE = 32
V_DIM = 64
ROPE_THETA = 10000.0
PEER_HEADS = 8
PEER_KEYS = 128
PEER_HALF = 128
PEER_TOPK = 16
EPS = 1e-6
LOG2_E = 1.4426950408889634

LANES = 128
SUBLANES = 8
VMEM_LIMIT = 56 * 1024 * 1024

TM = 512
TS = 256
TQ = 256
TN_ROUTE = 256
TN = 512
EB = 512


def _cparams(*sem, flags=None):
    return pltpu.CompilerParams(dimension_semantics=sem, vmem_limit_bytes=VMEM_LIMIT, flags=flags)


def _rms(x, g):
    return x * lax.rsqrt(jnp.mean(x * x, axis=-1, keepdims=True) + EPS) * g


def _dot(a, b):
    return jnp.dot(a, b, preferred_element_type=F32)


def _ada_kernel(c_ref, w_ref, b_ref, o_ref):
    ca = jax.nn.silu(c_ref[...])
    o_ref[...] = jnp.dot(ca, w_ref[...], preferred_element_type=F32,
                         precision=lax.Precision.HIGHEST) + b_ref[...]


def _ada(c, w, b):
    bsz, d = c.shape
    n = w.shape[1]
    return pl.pallas_call(
        _ada_kernel,
        grid=(n // d,),
        in_specs=[pl.BlockSpec((bsz, d), lambda j: (0, 0)),
                  pl.BlockSpec((d, d), lambda j: (0, j)),
                  pl.BlockSpec((1, d), lambda j: (0, j))],
        out_specs=pl.BlockSpec((bsz, d), lambda j: (0, j)),
        out_shape=jax.ShapeDtypeStruct((bsz, n), F32),
        compiler_params=_cparams("arbitrary"),
        name="ada",
    )(c, w, b.reshape(1, n))


def _pre_kernel(x_ref, mod_ref, g1_ref, pos_ref, invf_ref, wlru_ref, wq_ref, wkv_ref, wkr_ref,
                qg_ref, wuq_ref, kvg_ref, wuk_ref, wuv_ref, vones_ref, lru_ref, q_ref, k_ref, v_ref):
    d = x_ref.shape[-1]
    x = x_ref[0]
    shift = mod_ref[0, 0:1, :]
    scale = mod_ref[0, 1:2, :]
    hb = (_rms(x, g1_ref[...]) * (1.0 + scale) + shift).astype(BF16)
    lru_ref[0] = _dot(hb, wlru_ref[...])

    ang = pos_ref[0].astype(F32) * invf_ref[...]
    cos = jnp.cos(ang)
    sin = jnp.sin(ang)

    qn = _rms(_dot(hb, wq_ref[...]), qg_ref[...]).astype(BF16)
    q2 = _dot(qn, wuq_ref[...])
    q_scale = (QK_NOPE + QK_ROPE) ** -0.5 * LOG2_E
    for h in range(MLA_HEADS):
        lo = h * LANES
        q_ref[0, :, lo:lo + LANES] = ((q2[:, lo:lo + LANES] * cos
                                       + q2[:, d + lo:d + lo + LANES] * sin) * q_scale).astype(BF16)

    kvn = _rms(_dot(hb, wkv_ref[...]), kvg_ref[...]).astype(BF16)
    kn = _dot(kvn, wuk_ref[...])
    v_ref[0] = (_dot(kvn, wuv_ref[...]) + vones_ref[...]).astype(BF16)
    kr2 = _dot(hb, wkr_ref[...])
    krot = kr2[:, :LANES] * cos + kr2[:, LANES:] * sin
    for h in range(MLA_HEADS):
        lo = h * LANES
        k_ref[0, :, lo:lo + LANES] = (kn[:, lo:lo + LANES] + krot).astype(BF16)


def _pre(x, mod3, g1, pos3, invf, wlru, wq, wkv, wkr2, qg, wuq2, kvg, wuk, wuv, vones):
    bsz, s, d = x.shape
    full = lambda a: pl.BlockSpec(a.shape, lambda b, i: (0,) * a.ndim)
    tok = lambda w: pl.BlockSpec((1, TM, w), lambda b, i: (b, i, 0))
    return pl.pallas_call(
        _pre_kernel,
        grid=(bsz, s // TM),
        in_specs=[tok(d), pl.BlockSpec((1, 6, d), lambda b, i: (b, 0, 0)), full(g1), tok(1), full(invf),
                  full(wlru), full(wq), full(wkv), full(wkr2), full(qg), full(wuq2), full(kvg),
                  full(wuk), full(wuv), full(vones)],
        out_specs=[tok(d), tok(d), tok(d), tok(d)],
        out_shape=[jax.ShapeDtypeStruct((bsz, s, d), F32),
                   jax.ShapeDtypeStruct((bsz, s, d), BF16),
                   jax.ShapeDtypeStruct((bsz, s, d), BF16),
                   jax.ShapeDtypeStruct((bsz, s, d), BF16)],
        compiler_params=_cparams("arbitrary", "arbitrary"),
        name="pre",
    )(x, mod3, g1, pos3, invf, wlru, wq, wkv, wkr2, qg, wuq2, kvg, wuk, wuv, vones)


def _lru_kernel(x_ref, cw_ref, cb_ref, wg_ref, bg_ref, lam_ref, og_ref, o_ref, prev_ref, h_ref):
    w = o_ref.shape[-1]

    @pl.when(pl.program_id(1) == 0)
    def _():
        prev_ref[...] = jnp.zeros_like(prev_ref)
        h_ref[...] = jnp.zeros_like(h_ref)

    x = x_ref[0, :, :w]
    gate = x_ref[0, :, w:]
    prev = prev_ref[...]
    row = lax.broadcasted_iota(jnp.int32, x.shape, 0)

    xc = cb_ref[...]
    for k in range(CONV_WIDTH - 1):
        back = CONV_WIDTH - 1 - k
        shifted = jnp.where(row < back, pltpu.roll(prev, back, 0), pltpu.roll(x, back, 0))
        xc = xc + cw_ref[k:k + 1, :] * shifted
    xc = xc + cw_ref[CONV_WIDTH - 1:CONV_WIDTH, :] * x
    prev_ref[...] = x

    pre = _dot(xc.astype(BF16), wg_ref[...]) + bg_ref[...]
    r = jax.nn.sigmoid(pre[:, :w])
    i = jax.nn.sigmoid(pre[:, w:])
    log_a = -LRU_C * r * jax.nn.softplus(-lam_ref[...])
    a = jnp.exp(log_a)
    b = jnp.sqrt(1.0 - a * a) * (i * xc)

    dist = 1
    while dist < TS:
        keep = row >= dist
        b = jnp.where(keep, a * pltpu.roll(b, dist, 0) + b, b)
        a = jnp.where(keep, a * pltpu.roll(a, dist, 0), a)
        dist *= 2
    h = b + a * h_ref[0:1, :]
    h_ref[...] = jnp.broadcast_to(h[TS - 1:TS, :], h_ref.shape)

    y = jax.nn.gelu(gate) * h
    o_ref[0] = _rms(y, og_ref[...]).astype(BF16)


def _lru(lru_in, cw, cb, wg, bg, lam, og):
    bsz, s, d = lru_in.shape
    w = d // 2
    full = lambda a: pl.BlockSpec(a.shape, lambda b, t: (0,) * a.ndim)
    return pl.pallas_call(
        _lru_kernel,
        grid=(bsz, s // TS),
        in_specs=[pl.BlockSpec((1, TS, d), lambda b, t: (b, t, 0)),
                  full(cw), full(cb), full(wg), full(bg), full(lam), full(og)],
        out_specs=pl.BlockSpec((1, TS, w), lambda b, t: (b, t, 0)),
        out_shape=jax.ShapeDtypeStruct((bsz, s, w), BF16),
        scratch_shapes=[pltpu.VMEM((TS, w), F32), pltpu.VMEM((SUBLANES, w), F32)],
        compiler_params=_cparams("arbitrary", "arbitrary"),
        name="lru",
    )(lru_in, cw, cb, wg, bg, lam, og)


def _attn_kernel(q_ref, k_ref, v_ref, g_ref, o_ref, m_ref, acc_ref):
    i = pl.program_id(1)
    nt = (((1,), (1,)), ((), ()))
    row = lax.broadcasted_iota(jnp.int32, (TQ, TQ), 0)
    col = lax.broadcasted_iota(jnp.int32, (TQ, TQ), 1)
    lane = lax.broadcasted_iota(jnp.int32, (TQ, LANES), 1)

    m_ref[...] = jnp.full(m_ref.shape, -jnp.inf, F32)
    acc_ref[...] = jnp.zeros(acc_ref.shape, F32)

    def block(j, masked):
        start = pl.multiple_of(j * TQ, TQ)

        def scores(h):
            lo = h * LANES
            s = lax.dot_general(q_ref[0, :, lo:lo + LANES], k_ref[0, pl.ds(start, TQ), lo:lo + LANES], nt,
                                preferred_element_type=F32)
            return jnp.where(col <= row, s, -jnp.inf) if masked else s

        def softmax_step(h, s):
            m_old = m_ref[h]
            m_new = jnp.maximum(m_old, jnp.max(s, axis=-1, keepdims=True))
            m_ref[h] = m_new
            alpha = jnp.exp2(m_old - m_new)
            p = [jnp.exp2(s[:, c:c + LANES] - m_new) for c in range(0, TQ, LANES)]
            return alpha, jnp.concatenate(p, axis=-1).astype(BF16)

        def weighted_values(h, alpha, p):
            lo = h * LANES
            acc_ref[h] = alpha * acc_ref[h] + _dot(p, v_ref[0, pl.ds(start, TQ), lo:lo + LANES])

        pending_s, pending_p = {}, {}
        for t in range(MLA_HEADS + 2):
            if t < MLA_HEADS:
                pending_s[t] = scores(t)
            if 1 <= t <= MLA_HEADS:
                pending_p[t - 1] = softmax_step(t - 1, pending_s.pop(t - 1))
            if t >= 2:
                weighted_values(t - 2, *pending_p.pop(t - 2))

    def unmasked(j, carry):
        block(j, False)
        return carry

    lax.fori_loop(0, i, unmasked, 0)
    block(i, True)

    pairs = []
    for hp in range(MLA_HEADS // 2):
        even, odd = acc_ref[2 * hp], acc_ref[2 * hp + 1]
        num = jnp.where(lane < V_DIM, even, odd)
        den = jnp.where(lane < V_DIM, pltpu.roll(even, V_DIM, 1), pltpu.roll(odd, V_DIM, 1))
        pairs.append(num / den)
    y = jnp.concatenate(pairs, axis=-1)
    o_ref[0] = _rms(y, g_ref[...]).astype(BF16)


def _attn(q, k, v, g):
    bsz, s, d = q.shape
    w = g.shape[-1]
    return pl.pallas_call(
        _attn_kernel,
        grid=(bsz, s // TQ),
        in_specs=[pl.BlockSpec((1, TQ, d), lambda b, i: (b, i, 0)),
                  pl.BlockSpec((1, s, d), lambda b, i: (b, 0, 0)),
                  pl.BlockSpec((1, s, d), lambda b, i: (b, 0, 0)),
                  pl.BlockSpec(g.shape, lambda b, i: (0, 0))],
        out_specs=pl.BlockSpec((1, TQ, w), lambda b, i: (b, i, 0)),
        out_shape=jax.ShapeDtypeStruct((bsz, s, w), BF16),
        scratch_shapes=[pltpu.VMEM((MLA_HEADS, TQ, LANES), F32), pltpu.VMEM((MLA_HEADS, TQ, LANES), F32)],
        compiler_params=_cparams("arbitrary", "arbitrary"),
        name="attn",
    )(q, k, v, g)


def _post_kernel(yl_ref, ym_ref, x_ref, mod_ref, wol_ref, wom_ref, g2_ref, x1_ref, h2t_ref):
    mix = _dot(yl_ref[0], wol_ref[...]) + _dot(ym_ref[0], wom_ref[...])
    x1 = x_ref[0] + mod_ref[0, 2:3, :] * mix
    x1_ref[0] = x1
    h2 = _rms(x1, g2_ref[...]) * (1.0 + mod_ref[0, 4:5, :]) + mod_ref[0, 3:4, :]
    h2t_ref[...] = h2.T.astype(BF16)


def _post(yl, ym, x, mod3, wol, wom, g2):
    bsz, s, d = x.shape
    w = yl.shape[-1]
    nt = s // TM
    full = lambda a: pl.BlockSpec(a.shape, lambda b, i: (0,) * a.ndim)
    tok = lambda c: pl.BlockSpec((1, TM, c), lambda b, i: (b, i, 0))
    return pl.pallas_call(
        _post_kernel,
        grid=(bsz, nt),
        in_specs=[tok(w), tok(w), tok(d), pl.BlockSpec((1, 6, d), lambda b, i: (b, 0, 0)),
                  full(wol), full(wom), full(g2)],
        out_specs=[tok(d), pl.BlockSpec((d, TM), lambda b, i: (0, b * nt + i))],
        out_shape=[jax.ShapeDtypeStruct((bsz, s, d), F32),
                   jax.ShapeDtypeStruct((d, bsz * s), BF16)],
        compiler_params=_cparams("arbitrary", "arbitrary"),
        name="post",
    )(yl, ym, x, mod3, wol, wom, g2)


def _sort_pairs(n):
    pairs = []

    def merge(lo, hi, r):
        step = r * 2
        if step < hi - lo:
            merge(lo, hi, step)
            merge(lo + r, hi, step)
            pairs.extend((i, i + r) for i in range(lo + r, hi - r, step))
        else:
            pairs.append((lo, lo + r))

    def sort(lo, hi):
        if hi - lo >= 1:
            mid = lo + (hi - lo) // 2
            sort(lo, mid)
            sort(mid + 1, hi)
            merge(lo, hi, 1)

    sort(0, n - 1)
    return pairs


def _sort_desc(xs):
    xs = list(xs)
    for i, j in _sort_pairs(len(xs)):
        a, b = xs[i], xs[j]
        xs[i], xs[j] = jnp.maximum(a, b), jnp.minimum(a, b)
    return xs


def _merge_top(a, b):
    n = len(a)
    xs = [jnp.maximum(a[i], b[n - 1 - i]) for i in range(n)]
    dist = n // 2
    while dist >= 1:
        for i in range(n):
            if i & dist == 0:
                p, q = xs[i], xs[i + dist]
                xs[i], xs[i + dist] = jnp.maximum(p, q), jnp.minimum(p, q)
        dist //= 2
    return xs


def _kth_largest(vals, k):
    n = 1
    while n < len(vals):
        n *= 2
    present = [True] * len(vals) + [False] * (n - len(vals))
    ops = []
    for i, j in _sort_pairs(n):
        if present[i] and present[j]:
            ops.append(("cmp", i, j))
        elif present[j]:
            ops.append(("mov", i, j))
            present[i], present[j] = True, False
    need = {k}
    live = []
    for op in reversed(ops):
        kind, i, j = op
        if kind == "mov":
            if i in need:
                need.discard(i)
                need.add(j)
                live.append((kind, i, j, True, False))
        else:
            hi, lo = i in need, j in need
            if hi or lo:
                need.update((i, j))
                live.append((kind, i, j, hi, lo))
    xs = list(vals) + [None] * (n - len(vals))
    for kind, i, j, hi, lo in reversed(live):
        a, b = xs[i], xs[j]
        if kind == "mov":
            xs[i], xs[j] = b, None
        else:
            xs[i] = jnp.maximum(a, b) if hi else None
            xs[j] = jnp.minimum(a, b) if lo else None
    return xs[k]


def _top_values(slabs, n):
    groups = [_sort_desc(slabs[g:g + n]) for g in range(0, len(slabs), n)]
    while len(groups) > 1:
        groups = [_merge_top(groups[g], groups[g + 1]) for g in range(0, len(groups), 2)]
    return groups[0]


PACK_ROWS = 2 * SUBLANES


def _pair_bits(x):
    bits = pltpu.bitcast(x.astype(BF16).astype(F32), jnp.uint32)
    return (bits & jnp.uint32(0xFFFF0000)) | (bits >> 16)


def _route_kernel(h2t_ref, wqt_ref, kbig_ref, keys2_ref, c_ref, w1_ref, p_ref):
    tn = h2t_ref.shape[-1]
    nk = PEER_KEYS
    half_rows = PEER_HEADS * PEER_HALF
    qt = _dot(wqt_ref[...], h2t_ref[...])
    sub = lax.broadcasted_iota(jnp.int32, (SUBLANES, tn), 0)

    sc1 = _dot(kbig_ref[...], qt[:half_rows, :].astype(BF16))
    slabs = [sc1[PEER_HEADS * k:PEER_HEADS * (k + 1), :] for k in range(nk)]
    a = _top_values(slabs, PEER_TOPK)

    b = [None] * PEER_TOPK
    for h in range(PEER_HEADS):
        r0 = half_rows + h * PEER_HALF
        sc = _dot(keys2_ref[h], qt[r0:r0 + PEER_HALF, :].astype(BF16))
        tiles = [sc[SUBLANES * t:SUBLANES * (t + 1), :] for t in range(nk // SUBLANES)]
        xs = _sort_desc(tiles)
        shift = SUBLANES // 2
        while shift >= 1:
            xs = _merge_top(xs, [pltpu.roll(v, shift, 0) for v in xs])
            shift //= 2
        for ib in range(nk // PACK_ROWS):
            ranks, weights = [], []
            for tile in tiles[2 * ib:2 * ib + 2]:
                rk = jnp.full_like(tile, float(PEER_TOPK))
                for r in reversed(range(PEER_TOPK)):
                    rk = jnp.where(tile >= xs[r], float(r), rk)
                ranks.append(rk)
                weights.append(jnp.exp(tile - xs[0]))
            rk16 = jnp.concatenate(ranks, axis=0).astype(BF16)
            w16 = jnp.concatenate(weights, axis=0).astype(BF16)
            for lb in range(tn // LANES):
                r0 = 2 * h * PACK_ROWS
                p_ref[lb, ib, r0:r0 + PACK_ROWS, :] = rk16[:, lb * LANES:(lb + 1) * LANES]
                p_ref[lb, ib, r0 + PACK_ROWS:r0 + 2 * PACK_ROWS, :] = w16[:, lb * LANES:(lb + 1) * LANES]
        for r in range(PEER_TOPK):
            b[r] = xs[r] if h == 0 else jnp.where(sub == h, xs[r], b[r])

    pairs = [(i, j) for i in range(PEER_TOPK) for j in range(PEER_TOPK) if (i + 1) * (j + 1) <= PEER_TOPK]
    cands = {ij: a[ij[0]] + b[ij[1]] for ij in pairs}
    tau = _kth_largest(list(cands.values()), PEER_TOPK - 1)
    top = a[0] + b[0]
    z = jnp.zeros_like(tau)
    for c in cands.values():
        z = z + jnp.where(c >= tau, jnp.exp(c - top), 0.0)
    inv_z = 1.0 / z
    need = []
    for j in range(PEER_TOPK):
        t = None
        for i in range(PEER_TOPK):
            if (i, j) in cands:
                v = jnp.where(cands[(i, j)] >= tau, a[i], jnp.inf)
                t = v if t is None else jnp.minimum(t, v)
        need.append(t)
    for k in range(nk):
        cnt = jnp.zeros_like(tau)
        for j in range(PEER_TOPK):
            cnt = jnp.where(slabs[k] >= need[j], float(j + 1), cnt)
        c_ref[k] = _pair_bits(cnt)
        w1_ref[k] = _pair_bits(jnp.exp(slabs[k] - a[0]) * inv_z)


def _route(h2t, wqt, kbig, keys2):
    d, t = h2t.shape
    tn = TN_ROUTE
    by_key = pl.BlockSpec((PEER_KEYS, PEER_HEADS, tn), lambda i: (0, 0, i))
    packed = (PEER_KEYS // PACK_ROWS, 2 * PEER_HEADS * PACK_ROWS, LANES)
    return pl.pallas_call(
        _route_kernel,
        grid=(t // tn,),
        in_specs=[pl.BlockSpec((d, tn), lambda i: (0, i)),
                  pl.BlockSpec(wqt.shape, lambda i: (0, 0)),
                  pl.BlockSpec(kbig.shape, lambda i: (0, 0)),
                  pl.BlockSpec(keys2.shape, lambda i: (0, 0, 0))],
        out_specs=[by_key, by_key, pl.BlockSpec((tn // LANES,) + packed, lambda i: (i, 0, 0, 0))],
        out_shape=[jax.ShapeDtypeStruct((PEER_KEYS, PEER_HEADS, t), jnp.uint32),
                   jax.ShapeDtypeStruct((PEER_KEYS, PEER_HEADS, t), jnp.uint32),
                   jax.ShapeDtypeStruct((t // LANES,) + packed, BF16)],
        compiler_params=_cparams("arbitrary"),
        name="route",
    )(h2t, wqt, kbig, keys2)


def _dense_kernel(flag_ref, h2t_ref, u_ref, vt_ref, c_ref, w1_ref, p_ref, x1_ref, mod_ref,
                  fg_ref, o_ref, acc_ref, act_ref, gat_ref, *, nblk):
    s = pl.program_id(0)
    n_items = pl.num_programs(0) - 2
    tn = h2t_ref.shape[-1]
    eb = u_ref.shape[0]
    rows_per_step = eb // PEER_KEYS

    @pl.when(s == 0)
    def _():
        act_ref[1] = jnp.zeros(act_ref.shape[1:], F32)
        gat_ref[0] = jnp.zeros(gat_ref.shape[1:], BF16)

    jb = jnp.clip(s - 1, 0, n_items - 1) % nblk
    jc = jnp.clip(s - 2, 0, n_items - 1) % nblk

    @pl.when(jc == 0)
    def _():
        acc_ref[...] = jnp.zeros_like(acc_ref)

    n_lb = tn // LANES

    def row_tile(ref, i1, h, ls):
        return pltpu.bitcast(jnp.broadcast_to(ref[i1, h:h + 1, ls], (SUBLANES, LANES)), BF16)

    def gate_rows(il, oth, matmul_chunk):
        i1 = jb * rows_per_step + il
        base = il * PEER_KEYS
        for lb in range(n_lb):
            matmul_chunk(lb)
            ls = slice(lb * LANES, (lb + 1) * LANES)
            cntb = [row_tile(c_ref, i1, h, ls) for h in range(PEER_HEADS)]
            w1b = [row_tile(w1_ref, i1, h, ls) for h in range(PEER_HEADS)]
            for ib in range(PEER_KEYS // PACK_ROWS):
                g = None
                for h in range(PEER_HEADS):
                    r0 = 2 * h * PACK_ROWS
                    rank2 = p_ref[lb, ib, r0:r0 + PACK_ROWS, :]
                    w2 = p_ref[lb, ib, r0 + PACK_ROWS:r0 + 2 * PACK_ROWS, :]
                    term = jnp.minimum(jnp.maximum(cntb[h] - rank2, 0.0), w2) * w1b[h]
                    g = term if g is None else g + term
                r1 = base + ib * PACK_ROWS
                act = act_ref[oth, r1:r1 + PACK_ROWS, ls]
                gat_ref[oth, r1:r1 + PACK_ROWS, ls] = jax.nn.gelu(act.astype(BF16)) * g

    n_sec = rows_per_step
    lane_parts = n_sec // 2
    part = tn // lane_parts

    def sections(cur, oth):
        for sec in range(n_sec):
            @pl.when(flag_ref[0] == 0)
            def _(sec=sec):
                cs = slice((sec % lane_parts) * part, (sec % lane_parts + 1) * part)

                def matmul_chunk(lb):
                    if sec < lane_parts:
                        rows = acc_ref.shape[0] // n_lb
                        rs = slice(lb * rows, (lb + 1) * rows)
                        acc_ref[rs, cs] += _dot(vt_ref[0, rs, :], gat_ref[cur, :, cs])
                    else:
                        rows = eb // n_lb
                        rs = slice(lb * rows, (lb + 1) * rows)
                        act_ref[cur, rs, cs] = _dot(u_ref[rs, :], h2t_ref[:, cs])

                gate_rows(sec, oth, matmul_chunk)

    for parity in range(2):
        @pl.when(s % 2 == parity)
        def _(parity=parity):
            sections(parity, 1 - parity)

    @pl.when(jnp.logical_and(jc == nblk - 1, s >= 2))
    def _():
        x2 = x1_ref[...] + mod_ref[0, 5:6, :] * acc_ref[...].T
        o_ref[...] = _rms(x2, fg_ref[...])


def _dense(h2t, u, vt3, cnt, w1, packed, x1, mod3, fg, seq):
    d, t = h2t.shape
    ne = vt3.shape[0]
    per_seq = seq // TN
    n_items = (t // TN) * ne
    item = lambda s, lag: jnp.clip(s - lag, 0, n_items - 1)
    tile = lambda s, lag: item(s, lag) // ne
    block = lambda s, lag: item(s, lag) % ne
    by_key = pl.BlockSpec((PEER_KEYS, PEER_HEADS, TN), lambda s: (0, 0, tile(s, 1)))
    return pl.pallas_call(
        functools.partial(_dense_kernel, nblk=ne),
        grid=(n_items + 2,),
        in_specs=[pl.BlockSpec(memory_space=pltpu.SMEM),
                  pl.BlockSpec((d, TN), lambda s: (0, tile(s, 0))),
                  pl.BlockSpec((EB, d), lambda s: (block(s, 0), 0)),
                  pl.BlockSpec((1, d, EB), lambda s: (block(s, 2), 0, 0)),
                  by_key, by_key,
                  pl.BlockSpec((TN // LANES,) + packed.shape[1:], lambda s: (tile(s, 1), 0, 0, 0)),
                  pl.BlockSpec((TN, d), lambda s: (tile(s, 2), 0)),
                  pl.BlockSpec((1, 6, d), lambda s: (tile(s, 2) // per_seq, 0, 0)),
                  pl.BlockSpec(fg.shape, lambda s: (0, 0))],
        out_specs=pl.BlockSpec((TN, d), lambda s: (tile(s, 2), 0)),
        out_shape=jax.ShapeDtypeStruct((t, d), F32),
        scratch_shapes=[pltpu.VMEM((d, TN), F32), pltpu.VMEM((2, EB, TN), F32),
                        pltpu.VMEM((2, EB, TN), BF16)],
        compiler_params=_cparams("arbitrary"),
        name="dense",
    )(jnp.zeros((1,), jnp.int32), h2t, u, vt3, cnt, w1, packed, x1, mod3, fg)


def _rot_half_cols(w):
    half = w.shape[-1] // 2
    return jnp.concatenate([-w[..., half:], w[..., :half]], axis=-1)


def _pad_cols(w, before, total):
    return jnp.pad(w, [(0, 0)] * (w.ndim - 1) + [(before, total - before - w.shape[-1])])


def kernel(x, c, positions, w_ada, b_ada, norm1_g, w_in, conv_w, conv_b, lru_wa, lru_ba, lru_wx, lru_bx,
           lru_lambda, q_norm_g, w_uq, kv_norm_g, w_ukv, lru_out_g, mla_out_g, w_out, norm2_g, peer_wq,
           peer_keys, peer_u, peer_v, final_g):
    bsz, seq, d = x.shape
    depth = w_ada.shape[0]
    lw = d // 2
    qr = w_uq.shape[1]
    kvr = w_ukv.shape[1]
    assert depth == 1, "the final norm is fused into the (single) layer's PEER kernel"
    assert seq % TM == 0 and seq % TN == 0 and seq % TQ == 0 and seq % TS == 0
    assert d == MLA_HEADS * LANES and lw == MLA_HEADS * V_DIM

    inv_freq = 1.0 / (ROPE_THETA ** (jnp.arange(0, QK_ROPE, 2, dtype=F32) / QK_ROPE))
    invf = _pad_cols(jnp.concatenate([inv_freq, inv_freq])[None, :], QK_NOPE, LANES)
    pos3 = positions.reshape(bsz, seq, 1)
    row = lambda v: v.reshape(1, -1)

    for l in range(depth):
        mod3 = _ada(c, w_ada[l], b_ada[l]).reshape(bsz, 6, d)

        wi = w_in[l]
        o0, o1, o2, o3 = 2 * lw, 2 * lw + qr, 2 * lw + qr + kvr, 2 * lw + qr + kvr + QK_ROPE
        wkr = wi[:, o2:o3]
        wkr2 = jnp.concatenate([_pad_cols(wkr, QK_NOPE, LANES),
                                _pad_cols(_rot_half_cols(wkr), QK_NOPE, LANES)], axis=1).astype(BF16)
        uq = w_uq[l].reshape(qr, MLA_HEADS, QK_NOPE + QK_ROPE)
        uq_main = _pad_cols(uq, 0, LANES).reshape(qr, d)
        uq_rot = _pad_cols(_rot_half_cols(uq[..., QK_NOPE:]), QK_NOPE, LANES).reshape(qr, d)
        wuq2 = jnp.concatenate([uq_main, uq_rot], axis=1).astype(BF16)
        ukv = w_ukv[l].reshape(kvr, MLA_HEADS, QK_NOPE + V_DIM)
        wuk = _pad_cols(ukv[..., :QK_NOPE], 0, LANES).reshape(kvr, d).astype(BF16)
        uv = ukv[..., QK_NOPE:]
        odd = (jnp.arange(MLA_HEADS) % 2 == 1)[None, :, None]
        wuv = jnp.where(odd, _pad_cols(uv, V_DIM, LANES), _pad_cols(uv, 0, LANES)).reshape(kvr, d).astype(BF16)
        half = jnp.arange(LANES)[None, :] >= V_DIM
        vones = (half != odd[0]).astype(F32).reshape(1, d)

        lru_in, q, k, v = _pre(x, mod3, row(norm1_g[l]), pos3, invf, wi[:, :o0].astype(BF16),
                               wi[:, o0:o1].astype(BF16), wi[:, o1:o2].astype(BF16), wkr2,
                               row(q_norm_g[l]), wuq2, row(kv_norm_g[l]), wuk, wuv, vones)

        eye = jnp.eye(LRU_BLOCKS, dtype=F32)
        blockdiag = lambda w: jnp.einsum("hij,hg->higj", w, eye).reshape(lw, lw)
        wg = jnp.concatenate([blockdiag(lru_wa[l]), blockdiag(lru_wx[l])], axis=1).astype(BF16)
        bg = jnp.concatenate([lru_ba[l], lru_bx[l]])[None, :]
        yl = _lru(lru_in, conv_w[l], row(conv_b[l]), wg, bg, row(lru_lambda[l]), row(lru_out_g[l]))

        ym = _attn(q, k, v, row(mla_out_g[l]))

        wo = w_out[l].astype(BF16)
        x1, h2t = _post(yl, ym, x, mod3, wo[:lw], wo[lw:], row(norm2_g[l]))

        wqt = (peer_wq[l].reshape(d, PEER_HEADS, 2, PEER_HALF).transpose(2, 1, 3, 0)
               .reshape(2 * PEER_HEADS * PEER_HALF, d).astype(BF16))
        keys = peer_keys[l].astype(BF16)
        kbig = jnp.einsum("hkd,hg->khgd", keys[:, 0], jnp.eye(PEER_HEADS, dtype=BF16)).reshape(
            PEER_KEYS * PEER_HEADS, PEER_HEADS * PEER_HALF)
        cnt, w1, packed = _route(h2t, wqt, kbig, keys[:, 1])

        ne = peer_u.shape[1] // EB
        vt3 = peer_v[l].astype(BF16).reshape(ne, EB, d).transpose(0, 2, 1)
        out = _dense(h2t, peer_u[l].astype(BF16), vt3, cnt, w1, packed,
                     x1.reshape(bsz * seq, d), mod3, row(final_g), seq)
        x = out.reshape(bsz, seq, d)
    return x
```

```python
import functools

import jax
import jax.numpy as jnp
from jax import lax
from jax.experimental import pallas as pl
from jax.experimental.pallas import tpu as pltpu

F32 = jnp.float32
BF16 = jnp.bfloat16

LRU_BLOCKS = 8
CONV_WIDTH = 4
LRU_C = 8.0
MLA_HEADS = 8
QK_NOPE = 64
QK_ROPE = 32
V_DIM = 64
ROPE_THETA = 10000.0
PEER_HEADS = 8
PEER_KEYS = 128
PEER_HALF = 128
PEER_TOPK = 16
EPS = 1e-6
LOG2_E = 1.4426950408889634

LANES = 128
SUBLANES = 8
VMEM_LIMIT = 56 * 1024 * 1024

TM = 512
TS = 256
TQ = 256
TN_ROUTE = 256
TN = 512
EB = 1024
MXU_COLS = 256
DOT_ROWS = 256
DENSE_SECTIONS = 1


def _cparams(*sem, flags=None):
    return pltpu.CompilerParams(dimension_semantics=sem, vmem_limit_bytes=VMEM_LIMIT, flags=flags)


def _rms(x, g):
    return x * lax.rsqrt(jnp.mean(x * x, axis=-1, keepdims=True) + EPS) * g


def _dot(a, b):
    return jnp.dot(a, b, preferred_element_type=F32)


def _ada_kernel(c_ref, w_ref, b_ref, o_ref):
    ca = jax.nn.silu(c_ref[...])
    o_ref[...] = jnp.dot(ca, w_ref[...], preferred_element_type=F32,
                         precision=lax.Precision.HIGHEST) + b_ref[...]


def _ada(c, w, b):
    bsz, d = c.shape
    n = w.shape[1]
    return pl.pallas_call(
        _ada_kernel,
        grid=(n // d,),
        in_specs=[pl.BlockSpec((bsz, d), lambda j: (0, 0)),
                  pl.BlockSpec((d, d), lambda j: (0, j)),
                  pl.BlockSpec((1, d), lambda j: (0, j))],
        out_specs=pl.BlockSpec((bsz, d), lambda j: (0, j)),
        out_shape=jax.ShapeDtypeStruct((bsz, n), F32),
        compiler_params=_cparams("arbitrary"),
        name="ada",
    )(c, w, b.reshape(1, n))


def _pre_kernel(x_ref, mod_ref, g1_ref, pos_ref, invf_ref, wlru_ref, wq_ref, wkv_ref, wkr_ref,
                qg_ref, wuq_ref, kvg_ref, wuk_ref, wuv_ref, vones_ref, lru_ref, q_ref, k_ref, v_ref):
    d = x_ref.shape[-1]
    x = x_ref[0]
    shift = mod_ref[0, 0:1, :]
    scale = mod_ref[0, 1:2, :]
    hb = (_rms(x, g1_ref[...]) * (1.0 + scale) + shift).astype(BF16)
    lru_ref[0] = _dot(hb, wlru_ref[...])

    ang = pos_ref[0].astype(F32) * invf_ref[...]
    cos = jnp.cos(ang)
    sin = jnp.sin(ang)

    qn = _rms(_dot(hb, wq_ref[...]), qg_ref[...]).astype(BF16)
    q2 = _dot(qn, wuq_ref[...])
    q_scale = (QK_NOPE + QK_ROPE) ** -0.5 * LOG2_E
    for h in range(MLA_HEADS):
        lo = h * LANES
        q_ref[0, :, lo:lo + LANES] = ((q2[:, lo:lo + LANES] * cos
                                       + q2[:, d + lo:d + lo + LANES] * sin) * q_scale).astype(BF16)

    kvn = _rms(_dot(hb, wkv_ref[...]), kvg_ref[...]).astype(BF16)
    kn = _dot(kvn, wuk_ref[...])
    v_ref[0] = (_dot(kvn, wuv_ref[...]) + vones_ref[...]).astype(BF16)
    kr2 = _dot(hb, wkr_ref[...])
    krot = kr2[:, :LANES] * cos + kr2[:, LANES:] * sin
    for h in range(MLA_HEADS):
        lo = h * LANES
        k_ref[0, :, lo:lo + LANES] = (kn[:, lo:lo + LANES] + krot).astype(BF16)


def _pre(x, mod3, g1, pos3, invf, wlru, wq, wkv, wkr2, qg, wuq2, kvg, wuk, wuv, vones):
    bsz, s, d = x.shape
    full = lambda a: pl.BlockSpec(a.shape, lambda b, i: (0,) * a.ndim)
    tok = lambda w: pl.BlockSpec((1, TM, w), lambda b, i: (b, i, 0))
    return pl.pallas_call(
        _pre_kernel,
        grid=(bsz, s // TM),
        in_specs=[tok(d), pl.BlockSpec((1, 6, d), lambda b, i: (b, 0, 0)), full(g1), tok(1), full(invf),
                  full(wlru), full(wq), full(wkv), full(wkr2), full(qg), full(wuq2), full(kvg),
                  full(wuk), full(wuv), full(vones)],
        out_specs=[tok(d), tok(d), tok(d), tok(d)],
        out_shape=[jax.ShapeDtypeStruct((bsz, s, d), F32),
                   jax.ShapeDtypeStruct((bsz, s, d), BF16),
                   jax.ShapeDtypeStruct((bsz, s, d), BF16),
                   jax.ShapeDtypeStruct((bsz, s, d), BF16)],
        compiler_params=_cparams("arbitrary", "arbitrary"),
        name="pre",
    )(x, mod3, g1, pos3, invf, wlru, wq, wkv, wkr2, qg, wuq2, kvg, wuk, wuv, vones)


def _lru_kernel(x_ref, cw_ref, cb_ref, wg_ref, bg_ref, lam_ref, og_ref, o_ref, prev_ref, h_ref):
    w = o_ref.shape[-1]

    @pl.when(pl.program_id(1) == 0)
    def _():
        prev_ref[...] = jnp.zeros_like(prev_ref)
        h_ref[...] = jnp.zeros_like(h_ref)

    x = x_ref[0, :, :w]
    gate = x_ref[0, :, w:]
    prev = prev_ref[...]
    row = lax.broadcasted_iota(jnp.int32, x.shape, 0)

    xc = cb_ref[...]
    for k in range(CONV_WIDTH - 1):
        back = CONV_WIDTH - 1 - k
        shifted = jnp.where(row < back, pltpu.roll(prev, back, 0), pltpu.roll(x, back, 0))
        xc = xc + cw_ref[k:k + 1, :] * shifted
    xc = xc + cw_ref[CONV_WIDTH - 1:CONV_WIDTH, :] * x
    prev_ref[...] = x

    pre = _dot(xc.astype(BF16), wg_ref[...]) + bg_ref[...]
    r = jax.nn.sigmoid(pre[:, :w])
    i = jax.nn.sigmoid(pre[:, w:])
    log_a = -LRU_C * r * jax.nn.softplus(-lam_ref[...])
    a = jnp.exp(log_a)
    b = jnp.sqrt(1.0 - a * a) * (i * xc)

    dist = 1
    while dist < TS:
        keep = row >= dist
        b = jnp.where(keep, a * pltpu.roll(b, dist, 0) + b, b)
        a = jnp.where(keep, a * pltpu.roll(a, dist, 0), a)
        dist *= 2
    h = b + a * h_ref[0:1, :]
    h_ref[...] = jnp.broadcast_to(h[TS - 1:TS, :], h_ref.shape)

    y = jax.nn.gelu(gate) * h
    o_ref[0] = _rms(y, og_ref[...]).astype(BF16)


def _lru(lru_in, cw, cb, wg, bg, lam, og):
    bsz, s, d = lru_in.shape
    w = d // 2
    full = lambda a: pl.BlockSpec(a.shape, lambda b, t: (0,) * a.ndim)
    return pl.pallas_call(
        _lru_kernel,
        grid=(bsz, s // TS),
        in_specs=[pl.BlockSpec((1, TS, d), lambda b, t: (b, t, 0)),
                  full(cw), full(cb), full(wg), full(bg), full(lam), full(og)],
        out_specs=pl.BlockSpec((1, TS, w), lambda b, t: (b, t, 0)),
        out_shape=jax.ShapeDtypeStruct((bsz, s, w), BF16),
        scratch_shapes=[pltpu.VMEM((TS, w), F32), pltpu.VMEM((SUBLANES, w), F32)],
        compiler_params=_cparams("arbitrary", "arbitrary"),
        name="lru",
    )(lru_in, cw, cb, wg, bg, lam, og)


def _attn_kernel(q_ref, k_ref, v_ref, g_ref, o_ref, m_ref, acc_ref):
    i = pl.program_id(1)
    nt = (((1,), (1,)), ((), ()))
    row = lax.broadcasted_iota(jnp.int32, (TQ, TQ), 0)
    col = lax.broadcasted_iota(jnp.int32, (TQ, TQ), 1)
    lane = lax.broadcasted_iota(jnp.int32, (TQ, LANES), 1)

    m_ref[...] = jnp.full(m_ref.shape, -jnp.inf, F32)
    acc_ref[...] = jnp.zeros(acc_ref.shape, F32)

    def block(j, masked):
        start = pl.multiple_of(j * TQ, TQ)

        def scores(h):
            lo = h * LANES
            s = lax.dot_general(q_ref[0, :, lo:lo + LANES], k_ref[0, pl.ds(start, TQ), lo:lo + LANES], nt,
                                preferred_element_type=F32)
            return jnp.where(col <= row, s, -jnp.inf) if masked else s

        def softmax_step(h, s):
            m_old = m_ref[h]
            m_new = jnp.maximum(m_old, jnp.max(s, axis=-1, keepdims=True))
            m_ref[h] = m_new
            alpha = jnp.exp2(m_old - m_new)
            p = [jnp.exp2(s[:, c:c + LANES] - m_new) for c in range(0, TQ, LANES)]
            return alpha, jnp.concatenate(p, axis=-1).astype(BF16)

        def weighted_values(h, alpha, p):
            lo = h * LANES
            acc_ref[h] = alpha * acc_ref[h] + _dot(p, v_ref[0, pl.ds(start, TQ), lo:lo + LANES])

        pending_s, pending_p = {}, {}
        for t in range(MLA_HEADS + 2):
            if t < MLA_HEADS:
                pending_s[t] = scores(t)
            if 1 <= t <= MLA_HEADS:
                pending_p[t - 1] = softmax_step(t - 1, pending_s.pop(t - 1))
            if t >= 2:
                weighted_values(t - 2, *pending_p.pop(t - 2))

    def unmasked(j, carry):
        block(j, False)
        return carry

    lax.fori_loop(0, i, unmasked, 0)
    block(i, True)

    pairs = []
    for hp in range(MLA_HEADS // 2):
        even, odd = acc_ref[2 * hp], acc_ref[2 * hp + 1]
        num = jnp.where(lane < V_DIM, even, odd)
        den = jnp.where(lane < V_DIM, pltpu.roll(even, V_DIM, 1), pltpu.roll(odd, V_DIM, 1))
        pairs.append(num / den)
    y = jnp.concatenate(pairs, axis=-1)
    o_ref[0] = _rms(y, g_ref[...]).astype(BF16)


def _attn(q, k, v, g):
    bsz, s, d = q.shape
    w = g.shape[-1]
    return pl.pallas_call(
        _attn_kernel,
        grid=(bsz, s // TQ),
        in_specs=[pl.BlockSpec((1, TQ, d), lambda b, i: (b, i, 0)),
                  pl.BlockSpec((1, s, d), lambda b, i: (b, 0, 0)),
                  pl.BlockSpec((1, s, d), lambda b, i: (b, 0, 0)),
                  pl.BlockSpec(g.shape, lambda b, i: (0, 0))],
        out_specs=pl.BlockSpec((1, TQ, w), lambda b, i: (b, i, 0)),
        out_shape=jax.ShapeDtypeStruct((bsz, s, w), BF16),
        scratch_shapes=[pltpu.VMEM((MLA_HEADS, TQ, LANES), F32), pltpu.VMEM((MLA_HEADS, TQ, LANES), F32)],
        compiler_params=_cparams("arbitrary", "arbitrary"),
        name="attn",
    )(q, k, v, g)


def _post_kernel(yl_ref, ym_ref, x_ref, mod_ref, wol_ref, wom_ref, g2_ref, x1_ref, h2t_ref):
    mix = _dot(yl_ref[0], wol_ref[...]) + _dot(ym_ref[0], wom_ref[...])
    x1 = x_ref[0] + mod_ref[0, 2:3, :] * mix
    x1_ref[0] = x1
    h2 = _rms(x1, g2_ref[...]) * (1.0 + mod_ref[0, 4:5, :]) + mod_ref[0, 3:4, :]
    h2t_ref[...] = h2.T.astype(BF16)


def _post(yl, ym, x, mod3, wol, wom, g2):
    bsz, s, d = x.shape
    w = yl.shape[-1]
    nt = s // TM
    full = lambda a: pl.BlockSpec(a.shape, lambda b, i: (0,) * a.ndim)
    tok = lambda c: pl.BlockSpec((1, TM, c), lambda b, i: (b, i, 0))
    return pl.pallas_call(
        _post_kernel,
        grid=(bsz, nt),
        in_specs=[tok(w), tok(w), tok(d), pl.BlockSpec((1, 6, d), lambda b, i: (b, 0, 0)),
                  full(wol), full(wom), full(g2)],
        out_specs=[tok(d), pl.BlockSpec((d, TM), lambda b, i: (0, b * nt + i))],
        out_shape=[jax.ShapeDtypeStruct((bsz, s, d), F32),
                   jax.ShapeDtypeStruct((d, bsz * s), BF16)],
        compiler_params=_cparams("arbitrary", "arbitrary"),
        name="post",
    )(yl, ym, x, mod3, wol, wom, g2)


def _sort_pairs(n):
    pairs = []

    def merge(lo, hi, r):
        step = r * 2
        if step < hi - lo:
            merge(lo, hi, step)
            merge(lo + r, hi, step)
            pairs.extend((i, i + r) for i in range(lo + r, hi - r, step))
        else:
            pairs.append((lo, lo + r))

    def sort(lo, hi):
        if hi - lo >= 1:
            mid = lo + (hi - lo) // 2
            sort(lo, mid)
            sort(mid + 1, hi)
            merge(lo, hi, 1)

    sort(0, n - 1)
    return pairs


def _sort_desc(xs):
    xs = list(xs)
    for i, j in _sort_pairs(len(xs)):
        a, b = xs[i], xs[j]
        xs[i], xs[j] = jnp.maximum(a, b), jnp.minimum(a, b)
    return xs


def _merge_top(a, b):
    n = len(a)
    xs = [jnp.maximum(a[i], b[n - 1 - i]) for i in range(n)]
    dist = n // 2
    while dist >= 1:
        for i in range(n):
            if i & dist == 0:
                p, q = xs[i], xs[i + dist]
                xs[i], xs[i + dist] = jnp.maximum(p, q), jnp.minimum(p, q)
        dist //= 2
    return xs


def _kth_largest(vals, k):
    n = 1
    while n < len(vals):
        n *= 2
    present = [True] * len(vals) + [False] * (n - len(vals))
    ops = []
    for i, j in _sort_pairs(n):
        if present[i] and present[j]:
            ops.append(("cmp", i, j))
        elif present[j]:
            ops.append(("mov", i, j))
            present[i], present[j] = True, False
    need = {k}
    live = []
    for op in reversed(ops):
        kind, i, j = op
        if kind == "mov":
            if i in need:
                need.discard(i)
                need.add(j)
                live.append((kind, i, j, True, False))
        else:
            hi, lo = i in need, j in need
            if hi or lo:
                need.update((i, j))
                live.append((kind, i, j, hi, lo))
    xs = list(vals) + [None] * (n - len(vals))
    for kind, i, j, hi, lo in reversed(live):
        a, b = xs[i], xs[j]
        if kind == "mov":
            xs[i], xs[j] = b, None
        else:
            xs[i] = jnp.maximum(a, b) if hi else None
            xs[j] = jnp.minimum(a, b) if lo else None
    return xs[k]


def _top_values(slabs, n):
    groups = [_sort_desc(slabs[g:g + n]) for g in range(0, len(slabs), n)]
    while len(groups) > 1:
        groups = [_merge_top(groups[g], groups[g + 1]) for g in range(0, len(groups), 2)]
    return groups[0]


PACK_ROWS = 2 * SUBLANES


def _pair_bits(x):
    bits = pltpu.bitcast(x.astype(BF16).astype(F32), jnp.uint32)
    return (bits & jnp.uint32(0xFFFF0000)) | (bits >> 16)


def _route_kernel(h2t_ref, wqt_ref, kbig_ref, keys2_ref, c_ref, w1_ref, p_ref):
    tn = h2t_ref.shape[-1]
    nk = PEER_KEYS
    half_rows = PEER_HEADS * PEER_HALF
    qt = _dot(wqt_ref[...], h2t_ref[...])
    sub = lax.broadcasted_iota(jnp.int32, (SUBLANES, tn), 0)

    sc1 = _dot(kbig_ref[...], qt[:half_rows, :].astype(BF16))
    slabs = [sc1[PEER_HEADS * k:PEER_HEADS * (k + 1), :] for k in range(nk)]
    a = _top_values(slabs, PEER_TOPK)

    b = [None] * PEER_TOPK
    for h in range(PEER_HEADS):
        r0 = half_rows + h * PEER_HALF
        sc = _dot(keys2_ref[h], qt[r0:r0 + PEER_HALF, :].astype(BF16))
        tiles = [sc[SUBLANES * t:SUBLANES * (t + 1), :] for t in range(nk // SUBLANES)]
        xs = _sort_desc(tiles)
        shift = SUBLANES // 2
        while shift >= 1:
            xs = _merge_top(xs, [pltpu.roll(v, shift, 0) for v in xs])
            shift //= 2
        for ib in range(nk // PACK_ROWS):
            ranks, weights = [], []
            for tile in tiles[2 * ib:2 * ib + 2]:
                rk = jnp.full_like(tile, float(PEER_TOPK))
                for r in reversed(range(PEER_TOPK)):
                    rk = jnp.where(tile >= xs[r], float(r), rk)
                ranks.append(rk)
                weights.append(jnp.exp(tile - xs[0]))
            rk16 = jnp.concatenate(ranks, axis=0).astype(BF16)
            w16 = jnp.concatenate(weights, axis=0).astype(BF16)
            for lb in range(tn // LANES):
                r0 = 2 * h * PACK_ROWS
                p_ref[lb, ib, r0:r0 + PACK_ROWS, :] = rk16[:, lb * LANES:(lb + 1) * LANES]
                p_ref[lb, ib, r0 + PACK_ROWS:r0 + 2 * PACK_ROWS, :] = w16[:, lb * LANES:(lb + 1) * LANES]
        for r in range(PEER_TOPK):
            b[r] = xs[r] if h == 0 else jnp.where(sub == h, xs[r], b[r])

    pairs = [(i, j) for i in range(PEER_TOPK) for j in range(PEER_TOPK) if (i + 1) * (j + 1) <= PEER_TOPK]
    cands = {ij: a[ij[0]] + b[ij[1]] for ij in pairs}
    tau = _kth_largest(list(cands.values()), PEER_TOPK - 1)
    top = a[0] + b[0]
    z = jnp.zeros_like(tau)
    for c in cands.values():
        z = z + jnp.where(c >= tau, jnp.exp(c - top), 0.0)
    inv_z = 1.0 / z
    need = []
    for j in range(PEER_TOPK):
        t = None
        for i in range(PEER_TOPK):
            if (i, j) in cands:
                v = jnp.where(cands[(i, j)] >= tau, a[i], jnp.inf)
                t = v if t is None else jnp.minimum(t, v)
        need.append(t)
    for k in range(nk):
        cnt = jnp.zeros_like(tau)
        for j in range(PEER_TOPK):
            cnt = jnp.where(slabs[k] >= need[j], float(j + 1), cnt)
        c_ref[k] = _pair_bits(cnt)
        w1_ref[k] = _pair_bits(jnp.exp(slabs[k] - a[0]) * inv_z)


def _route(h2t, wqt, kbig, keys2):
    d, t = h2t.shape
    tn = TN_ROUTE
    by_key = pl.BlockSpec((PEER_KEYS, PEER_HEADS, tn), lambda i: (0, 0, i))
    packed = (PEER_KEYS // PACK_ROWS, 2 * PEER_HEADS * PACK_ROWS, LANES)
    return pl.pallas_call(
        _route_kernel,
        grid=(t // tn,),
        in_specs=[pl.BlockSpec((d, tn), lambda i: (0, i)),
                  pl.BlockSpec(wqt.shape, lambda i: (0, 0)),
                  pl.BlockSpec(kbig.shape, lambda i: (0, 0)),
                  pl.BlockSpec(keys2.shape, lambda i: (0, 0, 0))],
        out_specs=[by_key, by_key, pl.BlockSpec((tn // LANES,) + packed, lambda i: (i, 0, 0, 0))],
        out_shape=[jax.ShapeDtypeStruct((PEER_KEYS, PEER_HEADS, t), jnp.uint32),
                   jax.ShapeDtypeStruct((PEER_KEYS, PEER_HEADS, t), jnp.uint32),
                   jax.ShapeDtypeStruct((t // LANES,) + packed, BF16)],
        compiler_params=_cparams("arbitrary"),
        name="route",
    )(h2t, wqt, kbig, keys2)


def _dense_kernel(flag_ref, h2t_ref, u_ref, vt_ref, c_ref, w1_ref, p_ref, x1_ref, mod_ref,
                  fg_ref, o_ref, acc_ref, act_ref, gat_ref, *, nblk):
    s = pl.program_id(0)
    n_items = pl.num_programs(0) - 2
    tn = h2t_ref.shape[-1]
    eb = u_ref.shape[0]
    rows_per_step = eb // PEER_KEYS

    @pl.when(s == 0)
    def _():
        act_ref[1] = jnp.zeros(act_ref.shape[1:], F32)
        gat_ref[0] = jnp.zeros(gat_ref.shape[1:], BF16)

    jb = jnp.clip(s - 1, 0, n_items - 1) % nblk
    jc = jnp.clip(s - 2, 0, n_items - 1) % nblk

    @pl.when(jc == 0)
    def _():
        acc_ref[...] = jnp.zeros_like(acc_ref)

    n_lb = tn // LANES

    def row_tile(ref, i1, h, ls):
        return pltpu.bitcast(jnp.broadcast_to(ref[i1, h:h + 1, ls], (SUBLANES, LANES)), BF16)

    def gate_group(il, lb, oth):
        i1 = jb * rows_per_step + il
        base = il * PEER_KEYS
        ls = slice(lb * LANES, (lb + 1) * LANES)
        cntb = [row_tile(c_ref, i1, h, ls) for h in range(PEER_HEADS)]
        w1b = [row_tile(w1_ref, i1, h, ls) for h in range(PEER_HEADS)]
        for ib in range(PEER_KEYS // PACK_ROWS):
            g = None
            for h in range(PEER_HEADS):
                r0 = 2 * h * PACK_ROWS
                rank2 = p_ref[lb, ib, r0:r0 + PACK_ROWS, :]
                w2 = p_ref[lb, ib, r0 + PACK_ROWS:r0 + 2 * PACK_ROWS, :]
                term = jnp.minimum(jnp.maximum(cntb[h] - rank2, 0.0), w2) * w1b[h]
                g = term if g is None else g + term
            r1 = base + ib * PACK_ROWS
            act = act_ref[oth, r1:r1 + PACK_ROWS, ls]
            gat_ref[oth, r1:r1 + PACK_ROWS, ls] = jax.nn.gelu(act.astype(BF16)) * g

    def matmul_chunks(cur):
        chunks = []
        for c0 in range(0, tn, MXU_COLS):
            cs = slice(c0, c0 + MXU_COLS)
            for r0 in range(0, acc_ref.shape[0], DOT_ROWS):
                rs = slice(r0, r0 + DOT_ROWS)

                def second(rs=rs, cs=cs):
                    acc_ref[rs, cs] += _dot(vt_ref[0, rs, :], gat_ref[cur, :, cs])
                chunks.append(second)
        for c0 in range(0, tn, MXU_COLS):
            cs = slice(c0, c0 + MXU_COLS)
            for r0 in range(0, eb, DOT_ROWS):
                rs = slice(r0, r0 + DOT_ROWS)

                def first(rs=rs, cs=cs):
                    act_ref[cur, rs, cs] = _dot(u_ref[rs, :], h2t_ref[:, cs])
                chunks.append(first)
        return chunks

    def sections(cur, oth):
        groups = [(il, lb) for il in range(rows_per_step) for lb in range(n_lb)]
        chunks = matmul_chunks(cur)
        per_section = len(groups) // DENSE_SECTIONS
        for sec in range(DENSE_SECTIONS):
            @pl.when(flag_ref[0] == 0)
            def _(sec=sec):
                for gi in range(sec * per_section, (sec + 1) * per_section):
                    for ci in range(gi * len(chunks) // len(groups), (gi + 1) * len(chunks) // len(groups)):
                        chunks[ci]()
                    gate_group(*groups[gi], oth)

    for parity in range(2):
        @pl.when(s % 2 == parity)
        def _(parity=parity):
            sections(parity, 1 - parity)

    @pl.when(jnp.logical_and(jc == nblk - 1, s >= 2))
    def _():
        x2 = x1_ref[...] + mod_ref[0, 5:6, :] * acc_ref[...].T
        o_ref[...] = _rms(x2, fg_ref[...])


def _dense(h2t, u, vt3, cnt, w1, packed, x1, mod3, fg, seq):
    d, t = h2t.shape
    ne = vt3.shape[0]
    per_seq = seq // TN
    n_items = (t // TN) * ne
    item = lambda s, lag: jnp.clip(s - lag, 0, n_items - 1)
    tile = lambda s, lag: item(s, lag) // ne
    block = lambda s, lag: item(s, lag) % ne
    by_key = pl.BlockSpec((PEER_KEYS, PEER_HEADS, TN), lambda s: (0, 0, tile(s, 1)))
    return pl.pallas_call(
        functools.partial(_dense_kernel, nblk=ne),
        grid=(n_items + 2,),
        in_specs=[pl.BlockSpec(memory_space=pltpu.SMEM),
                  pl.BlockSpec((d, TN), lambda s: (0, tile(s, 0))),
                  pl.BlockSpec((EB, d), lambda s: (block(s, 0), 0)),
                  pl.BlockSpec((1, d, EB), lambda s: (block(s, 2), 0, 0)),
                  by_key, by_key,
                  pl.BlockSpec((TN // LANES,) + packed.shape[1:], lambda s: (tile(s, 1), 0, 0, 0)),
                  pl.BlockSpec((TN, d), lambda s: (tile(s, 2), 0)),
                  pl.BlockSpec((1, 6, d), lambda s: (tile(s, 2) // per_seq, 0, 0)),
                  pl.BlockSpec(fg.shape, lambda s: (0, 0))],
        out_specs=pl.BlockSpec((TN, d), lambda s: (tile(s, 2), 0)),
        out_shape=jax.ShapeDtypeStruct((t, d), F32),
        scratch_shapes=[pltpu.VMEM((d, TN), F32), pltpu.VMEM((2, EB, TN), F32),
                        pltpu.VMEM((2, EB, TN), BF16)],
        compiler_params=_cparams("arbitrary"),
        name="dense",
    )(jnp.zeros((1,), jnp.int32), h2t, u, vt3, cnt, w1, packed, x1, mod3, fg)


def _rot_half_cols(w):
    half = w.shape[-1] // 2
    return jnp.concatenate([-w[..., half:], w[..., :half]], axis=-1)


def _pad_cols(w, before, total):
    return jnp.pad(w, [(0, 0)] * (w.ndim - 1) + [(before, total - before - w.shape[-1])])


def kernel(x, c, positions, w_ada, b_ada, norm1_g, w_in, conv_w, conv_b, lru_wa, lru_ba, lru_wx, lru_bx,
           lru_lambda, q_norm_g, w_uq, kv_norm_g, w_ukv, lru_out_g, mla_out_g, w_out, norm2_g, peer_wq,
           peer_keys, peer_u, peer_v, final_g):
    bsz, seq, d = x.shape
    depth = w_ada.shape[0]
    lw = d // 2
    qr = w_uq.shape[1]
    kvr = w_ukv.shape[1]
    assert depth == 1, "the final norm is fused into the (single) layer's PEER kernel"
    assert seq % TM == 0 and seq % TN == 0 and seq % TQ == 0 and seq % TS == 0
    assert d == MLA_HEADS * LANES and lw == MLA_HEADS * V_DIM

    inv_freq = 1.0 / (ROPE_THETA ** (jnp.arange(0, QK_ROPE, 2, dtype=F32) / QK_ROPE))
    invf = _pad_cols(jnp.concatenate([inv_freq, inv_freq])[None, :], QK_NOPE, LANES)
    pos3 = positions.reshape(bsz, seq, 1)
    row = lambda v: v.reshape(1, -1)

    for l in range(depth):
        mod3 = _ada(c, w_ada[l], b_ada[l]).reshape(bsz, 6, d)

        wi = w_in[l]
        o0, o1, o2, o3 = 2 * lw, 2 * lw + qr, 2 * lw + qr + kvr, 2 * lw + qr + kvr + QK_ROPE
        wkr = wi[:, o2:o3]
        wkr2 = jnp.concatenate([_pad_cols(wkr, QK_NOPE, LANES),
                                _pad_cols(_rot_half_cols(wkr), QK_NOPE, LANES)], axis=1).astype(BF16)
        uq = w_uq[l].reshape(qr, MLA_HEADS, QK_NOPE + QK_ROPE)
        uq_main = _pad_cols(uq, 0, LANES).reshape(qr, d)
        uq_rot = _pad_cols(_rot_half_cols(uq[..., QK_NOPE:]), QK_NOPE, LANES).reshape(qr, d)
        wuq2 = jnp.concatenate([uq_main, uq_rot], axis=1).astype(BF16)
        ukv = w_ukv[l].reshape(kvr, MLA_HEADS, QK_NOPE + V_DIM)
        wuk = _pad_cols(ukv[..., :QK_NOPE], 0, LANES).reshape(kvr, d).astype(BF16)
        uv = ukv[..., QK_NOPE:]
        odd = (jnp.arange(MLA_HEADS) % 2 == 1)[None, :, None]
        wuv = jnp.where(odd, _pad_cols(uv, V_DIM, LANES), _pad_cols(uv, 0, LANES)).reshape(kvr, d).astype(BF16)
        half = jnp.arange(LANES)[None, :] >= V_DIM
        vones = (half != odd[0]).astype(F32).reshape(1, d)

        lru_in, q, k, v = _pre(x, mod3, row(norm1_g[l]), pos3, invf, wi[:, :o0].astype(BF16),
                               wi[:, o0:o1].astype(BF16), wi[:, o1:o2].astype(BF16), wkr2,
                               row(q_norm_g[l]), wuq2, row(kv_norm_g[l]), wuk, wuv, vones)

        eye = jnp.eye(LRU_BLOCKS, dtype=F32)
        blockdiag = lambda w: jnp.einsum("hij,hg->higj", w, eye).reshape(lw, lw)
        wg = jnp.concatenate([blockdiag(lru_wa[l]), blockdiag(lru_wx[l])], axis=1).astype(BF16)
        bg = jnp.concatenate([lru_ba[l], lru_bx[l]])[None, :]
        yl = _lru(lru_in, conv_w[l], row(conv_b[l]), wg, bg, row(lru_lambda[l]), row(lru_out_g[l]))

        ym = _attn(q, k, v, row(mla_out_g[l]))

        wo = w_out[l].astype(BF16)
        x1, h2t = _post(yl, ym, x, mod3, wo[:lw], wo[lw:], row(norm2_g[l]))

        wqt = (peer_wq[l].reshape(d, PEER_HEADS, 2, PEER_HALF).transpose(2, 1, 3, 0)
               .reshape(2 * PEER_HEADS * PEER_HALF, d).astype(BF16))
        keys = peer_keys[l].astype(BF16)
        kbig = jnp.einsum("hkd,hg->khgd", keys[:, 0], jnp.eye(PEER_HEADS, dtype=BF16)).reshape(
            PEER_KEYS * PEER_HEADS, PEER_HEADS * PEER_HALF)
        cnt, w1, packed = _route(h2t, wqt, kbig, keys[:, 1])

        ne = peer_u.shape[1] // EB
        vt3 = peer_v[l].astype(BF16).reshape(ne, EB, d).transpose(0, 2, 1)
        out = _dense(h2t, peer_u[l].astype(BF16), vt3, cnt, w1, packed,
                     x1.reshape(bsz * seq, d), mod3, row(final_g), seq)
        x = out.reshape(bsz, seq, d)
    return x
```

```python
import functools

import jax
import jax.numpy as jnp
from jax import lax
from jax.experimental import pallas as pl
from jax.experimental.pallas import tpu as pltpu

F32 = jnp.float32
BF16 = jnp.bfloat16

LRU_BLOCKS = 8
CONV_WIDTH = 4
LRU_C = 8.0
MLA_HEADS = 8
QK_NOPE = 64
QK_ROPE = 32
V_DIM = 64
ROPE_THETA = 10000.0
PEER_HEADS = 8
PEER_KEYS = 128
PEER_HALF = 128
PEER_TOPK = 16
EPS = 1e-6
LOG2_E = 1.4426950408889634

LANES = 128
SUBLANES = 8
VMEM_LIMIT = 56 * 1024 * 1024

TM = 512
TS = 256
TQ = 256
TN_ROUTE = 256
TN = 512
EB = 1024
MXU_COLS = 256
DOT_ROWS = 128
GATE_PARTS = 1
DENSE_SECTIONS = 1


def _cparams(*sem, flags=None):
    return pltpu.CompilerParams(dimension_semantics=sem, vmem_limit_bytes=VMEM_LIMIT, flags=flags)


def _rms(x, g):
    return x * lax.rsqrt(jnp.mean(x * x, axis=-1, keepdims=True) + EPS) * g


def _dot(a, b):
    return jnp.dot(a, b, preferred_element_type=F32)


def _ada_kernel(c_ref, w_ref, b_ref, o_ref):
    ca = jax.nn.silu(c_ref[...])
    o_ref[...] = jnp.dot(ca, w_ref[...], preferred_element_type=F32,
                         precision=lax.Precision.HIGHEST) + b_ref[...]


def _ada(c, w, b):
    bsz, d = c.shape
    n = w.shape[1]
    return pl.pallas_call(
        _ada_kernel,
        grid=(n // d,),
        in_specs=[pl.BlockSpec((bsz, d), lambda j: (0, 0)),
                  pl.BlockSpec((d, d), lambda j: (0, j)),
                  pl.BlockSpec((1, d), lambda j: (0, j))],
        out_specs=pl.BlockSpec((bsz, d), lambda j: (0, j)),
        out_shape=jax.ShapeDtypeStruct((bsz, n), F32),
        compiler_params=_cparams("arbitrary"),
        name="ada",
    )(c, w, b.reshape(1, n))


def _pre_kernel(x_ref, mod_ref, g1_ref, pos_ref, invf_ref, wlru_ref, wq_ref, wkv_ref, wkr_ref,
                qg_ref, wuq_ref, kvg_ref, wuk_ref, wuv_ref, vones_ref, lru_ref, q_ref, k_ref, v_ref):
    d = x_ref.shape[-1]
    x = x_ref[0]
    shift = mod_ref[0, 0:1, :]
    scale = mod_ref[0, 1:2, :]
    hb = (_rms(x, g1_ref[...]) * (1.0 + scale) + shift).astype(BF16)
    lru_ref[0] = _dot(hb, wlru_ref[...])

    ang = pos_ref[0].astype(F32) * invf_ref[...]
    cos = jnp.cos(ang)
    sin = jnp.sin(ang)

    qn = _rms(_dot(hb, wq_ref[...]), qg_ref[...]).astype(BF16)
    q2 = _dot(qn, wuq_ref[...])
    q_scale = (QK_NOPE + QK_ROPE) ** -0.5 * LOG2_E
    for h in range(MLA_HEADS):
        lo = h * LANES
        q_ref[0, :, lo:lo + LANES] = ((q2[:, lo:lo + LANES] * cos
                                       + q2[:, d + lo:d + lo + LANES] * sin) * q_scale).astype(BF16)

    kvn = _rms(_dot(hb, wkv_ref[...]), kvg_ref[...]).astype(BF16)
    kn = _dot(kvn, wuk_ref[...])
    v_ref[0] = (_dot(kvn, wuv_ref[...]) + vones_ref[...]).astype(BF16)
    kr2 = _dot(hb, wkr_ref[...])
    krot = kr2[:, :LANES] * cos + kr2[:, LANES:] * sin
    for h in range(MLA_HEADS):
        lo = h * LANES
        k_ref[0, :, lo:lo + LANES] = (kn[:, lo:lo + LANES] + krot).astype(BF16)


def _pre(x, mod3, g1, pos3, invf, wlru, wq, wkv, wkr2, qg, wuq2, kvg, wuk, wuv, vones):
    bsz, s, d = x.shape
    full = lambda a: pl.BlockSpec(a.shape, lambda b, i: (0,) * a.ndim)
    tok = lambda w: pl.BlockSpec((1, TM, w), lambda b, i: (b, i, 0))
    return pl.pallas_call(
        _pre_kernel,
        grid=(bsz, s // TM),
        in_specs=[tok(d), pl.BlockSpec((1, 6, d), lambda b, i: (b, 0, 0)), full(g1), tok(1), full(invf),
                  full(wlru), full(wq), full(wkv), full(wkr2), full(qg), full(wuq2), full(kvg),
                  full(wuk), full(wuv), full(vones)],
        out_specs=[tok(d), tok(d), tok(d), tok(d)],
        out_shape=[jax.ShapeDtypeStruct((bsz, s, d), F32),
                   jax.ShapeDtypeStruct((bsz, s, d), BF16),
                   jax.ShapeDtypeStruct((bsz, s, d), BF16),
                   jax.ShapeDtypeStruct((bsz, s, d), BF16)],
        compiler_params=_cparams("arbitrary", "arbitrary"),
        name="pre",
    )(x, mod3, g1, pos3, invf, wlru, wq, wkv, wkr2, qg, wuq2, kvg, wuk, wuv, vones)


def _lru_kernel(x_ref, cw_ref, cb_ref, wg_ref, bg_ref, lam_ref, og_ref, o_ref, prev_ref, h_ref):
    w = o_ref.shape[-1]

    @pl.when(pl.program_id(1) == 0)
    def _():
        prev_ref[...] = jnp.zeros_like(prev_ref)
        h_ref[...] = jnp.zeros_like(h_ref)

    x = x_ref[0, :, :w]
    gate = x_ref[0, :, w:]
    tail = prev_ref[...]
    row = lax.broadcasted_iota(jnp.int32, x.shape, 0)
    row8 = lax.broadcasted_iota(jnp.int32, tail.shape, 0)

    xc = cb_ref[...]
    for k in range(CONV_WIDTH - 1):
        back = CONV_WIDTH - 1 - k
        shifted = pltpu.roll(x, back, 0)
        first = jnp.where(row8 < back, pltpu.roll(tail, back, 0), shifted[:SUBLANES, :])
        shifted = jnp.concatenate([first, shifted[SUBLANES:, :]], axis=0)
        xc = xc + cw_ref[k:k + 1, :] * shifted
    xc = xc + cw_ref[CONV_WIDTH - 1:CONV_WIDTH, :] * x
    prev_ref[...] = x[TS - SUBLANES:, :]

    pre = _dot(xc.astype(BF16), wg_ref[...]) + bg_ref[...]
    r = jax.nn.sigmoid(pre[:, :w])
    i = jax.nn.sigmoid(pre[:, w:])
    log_a = -LRU_C * r * jax.nn.softplus(-lam_ref[...])
    a = jnp.exp(log_a)
    b = jnp.sqrt(1.0 - a * a) * (i * xc)

    sub = row % SUBLANES
    dist = 1
    while dist < SUBLANES:
        keep = sub >= dist
        b = jnp.where(keep, a * pltpu.roll(b, dist, 0) + b, b)
        a = jnp.where(keep, a * pltpu.roll(a, dist, 0), a)
        dist *= 2
    state = h_ref[0:1, :]
    groups = []
    for g0 in range(0, TS, SUBLANES):
        hg = b[g0:g0 + SUBLANES, :] + a[g0:g0 + SUBLANES, :] * state
        groups.append(hg)
        state = hg[SUBLANES - 1:SUBLANES, :]
    h = jnp.concatenate(groups, axis=0)
    h_ref[...] = jnp.broadcast_to(state, h_ref.shape)

    y = jax.nn.gelu(gate) * h
    o_ref[0] = _rms(y, og_ref[...]).astype(BF16)


def _lru(lru_in, cw, cb, wg, bg, lam, og):
    bsz, s, d = lru_in.shape
    w = d // 2
    full = lambda a: pl.BlockSpec(a.shape, lambda b, t: (0,) * a.ndim)
    return pl.pallas_call(
        _lru_kernel,
        grid=(bsz, s // TS),
        in_specs=[pl.BlockSpec((1, TS, d), lambda b, t: (b, t, 0)),
                  full(cw), full(cb), full(wg), full(bg), full(lam), full(og)],
        out_specs=pl.BlockSpec((1, TS, w), lambda b, t: (b, t, 0)),
        out_shape=jax.ShapeDtypeStruct((bsz, s, w), BF16),
        scratch_shapes=[pltpu.VMEM((SUBLANES, w), F32), pltpu.VMEM((SUBLANES, w), F32)],
        compiler_params=_cparams("arbitrary", "arbitrary"),
        name="lru",
    )(lru_in, cw, cb, wg, bg, lam, og)


def _attn_kernel(q_ref, k_ref, v_ref, g_ref, o_ref, m_ref, acc_ref):
    i = pl.program_id(1)
    nt = (((1,), (1,)), ((), ()))
    row = lax.broadcasted_iota(jnp.int32, (TQ, TQ), 0)
    col = lax.broadcasted_iota(jnp.int32, (TQ, TQ), 1)
    lane = lax.broadcasted_iota(jnp.int32, (TQ, LANES), 1)

    m_ref[...] = jnp.full(m_ref.shape, -jnp.inf, F32)
    acc_ref[...] = jnp.zeros(acc_ref.shape, F32)

    def block(j, masked):
        start = pl.multiple_of(j * TQ, TQ)

        def scores(h):
            lo = h * LANES
            s = lax.dot_general(q_ref[0, :, lo:lo + LANES], k_ref[0, pl.ds(start, TQ), lo:lo + LANES], nt,
                                preferred_element_type=F32)
            return jnp.where(col <= row, s, -jnp.inf) if masked else s

        def softmax_step(h, s):
            m_old = m_ref[h]
            m_new = jnp.maximum(m_old, jnp.max(s, axis=-1, keepdims=True))
            m_ref[h] = m_new
            alpha = jnp.exp2(m_old - m_new)
            p = [jnp.exp2(s[:, c:c + LANES] - m_new) for c in range(0, TQ, LANES)]
            return alpha, jnp.concatenate(p, axis=-1).astype(BF16)

        def weighted_values(h, alpha, p):
            lo = h * LANES
            acc_ref[h] = alpha * acc_ref[h] + _dot(p, v_ref[0, pl.ds(start, TQ), lo:lo + LANES])

        pending_s, pending_p = {}, {}
        for t in range(MLA_HEADS + 2):
            if t < MLA_HEADS:
                pending_s[t] = scores(t)
            if 1 <= t <= MLA_HEADS:
                pending_p[t - 1] = softmax_step(t - 1, pending_s.pop(t - 1))
            if t >= 2:
                weighted_values(t - 2, *pending_p.pop(t - 2))

    def unmasked(j, carry):
        block(j, False)
        return carry

    lax.fori_loop(0, i, unmasked, 0)
    block(i, True)

    pairs = []
    for hp in range(MLA_HEADS // 2):
        even, odd = acc_ref[2 * hp], acc_ref[2 * hp + 1]
        num = jnp.where(lane < V_DIM, even, odd)
        den = jnp.where(lane < V_DIM, pltpu.roll(even, V_DIM, 1), pltpu.roll(odd, V_DIM, 1))
        pairs.append(num / den)
    y = jnp.concatenate(pairs, axis=-1)
    o_ref[0] = _rms(y, g_ref[...]).astype(BF16)


def _attn(q, k, v, g):
    bsz, s, d = q.shape
    w = g.shape[-1]
    return pl.pallas_call(
        _attn_kernel,
        grid=(bsz, s // TQ),
        in_specs=[pl.BlockSpec((1, TQ, d), lambda b, i: (b, i, 0)),
                  pl.BlockSpec((1, s, d), lambda b, i: (b, 0, 0)),
                  pl.BlockSpec((1, s, d), lambda b, i: (b, 0, 0)),
                  pl.BlockSpec(g.shape, lambda b, i: (0, 0))],
        out_specs=pl.BlockSpec((1, TQ, w), lambda b, i: (b, i, 0)),
        out_shape=jax.ShapeDtypeStruct((bsz, s, w), BF16),
        scratch_shapes=[pltpu.VMEM((MLA_HEADS, TQ, LANES), F32), pltpu.VMEM((MLA_HEADS, TQ, LANES), F32)],
        compiler_params=_cparams("arbitrary", "arbitrary"),
        name="attn",
    )(q, k, v, g)


def _post_kernel(yl_ref, ym_ref, x_ref, mod_ref, wol_ref, wom_ref, g2_ref, x1_ref, h2t_ref):
    mix = _dot(yl_ref[0], wol_ref[...]) + _dot(ym_ref[0], wom_ref[...])
    x1 = x_ref[0] + mod_ref[0, 2:3, :] * mix
    x1_ref[0] = x1
    h2 = _rms(x1, g2_ref[...]) * (1.0 + mod_ref[0, 4:5, :]) + mod_ref[0, 3:4, :]
    h2t_ref[...] = h2.T.astype(BF16)


def _post(yl, ym, x, mod3, wol, wom, g2):
    bsz, s, d = x.shape
    w = yl.shape[-1]
    nt = s // TM
    full = lambda a: pl.BlockSpec(a.shape, lambda b, i: (0,) * a.ndim)
    tok = lambda c: pl.BlockSpec((1, TM, c), lambda b, i: (b, i, 0))
    return pl.pallas_call(
        _post_kernel,
        grid=(bsz, nt),
        in_specs=[tok(w), tok(w), tok(d), pl.BlockSpec((1, 6, d), lambda b, i: (b, 0, 0)),
                  full(wol), full(wom), full(g2)],
        out_specs=[tok(d), pl.BlockSpec((d, TM), lambda b, i: (0, b * nt + i))],
        out_shape=[jax.ShapeDtypeStruct((bsz, s, d), F32),
                   jax.ShapeDtypeStruct((d, bsz * s), BF16)],
        compiler_params=_cparams("arbitrary", "arbitrary"),
        name="post",
    )(yl, ym, x, mod3, wol, wom, g2)


def _sort_pairs(n):
    pairs = []

    def merge(lo, hi, r):
        step = r * 2
        if step < hi - lo:
            merge(lo, hi, step)
            merge(lo + r, hi, step)
            pairs.extend((i, i + r) for i in range(lo + r, hi - r, step))
        else:
            pairs.append((lo, lo + r))

    def sort(lo, hi):
        if hi - lo >= 1:
            mid = lo + (hi - lo) // 2
            sort(lo, mid)
            sort(mid + 1, hi)
            merge(lo, hi, 1)

    sort(0, n - 1)
    return pairs


def _sort_desc(xs):
    xs = list(xs)
    for i, j in _sort_pairs(len(xs)):
        a, b = xs[i], xs[j]
        xs[i], xs[j] = jnp.maximum(a, b), jnp.minimum(a, b)
    return xs


def _merge_top(a, b):
    n = len(a)
    xs = [jnp.maximum(a[i], b[n - 1 - i]) for i in range(n)]
    dist = n // 2
    while dist >= 1:
        for i in range(n):
            if i & dist == 0:
                p, q = xs[i], xs[i + dist]
                xs[i], xs[i + dist] = jnp.maximum(p, q), jnp.minimum(p, q)
        dist //= 2
    return xs


def _kth_largest(vals, k):
    n = 1
    while n < len(vals):
        n *= 2
    present = [True] * len(vals) + [False] * (n - len(vals))
    ops = []
    for i, j in _sort_pairs(n):
        if present[i] and present[j]:
            ops.append(("cmp", i, j))
        elif present[j]:
            ops.append(("mov", i, j))
            present[i], present[j] = True, False
    need = {k}
    live = []
    for op in reversed(ops):
        kind, i, j = op
        if kind == "mov":
            if i in need:
                need.discard(i)
                need.add(j)
                live.append((kind, i, j, True, False))
        else:
            hi, lo = i in need, j in need
            if hi or lo:
                need.update((i, j))
                live.append((kind, i, j, hi, lo))
    xs = list(vals) + [None] * (n - len(vals))
    for kind, i, j, hi, lo in reversed(live):
        a, b = xs[i], xs[j]
        if kind == "mov":
            xs[i], xs[j] = b, None
        else:
            xs[i] = jnp.maximum(a, b) if hi else None
            xs[j] = jnp.minimum(a, b) if lo else None
    return xs[k]


def _top_values(slabs, n):
    groups = [_sort_desc(slabs[g:g + n]) for g in range(0, len(slabs), n)]
    while len(groups) > 1:
        groups = [_merge_top(groups[g], groups[g + 1]) for g in range(0, len(groups), 2)]
    return groups[0]


PACK_ROWS = 2 * SUBLANES


def _pair_bits(x):
    bits = pltpu.bitcast(x.astype(BF16).astype(F32), jnp.uint32)
    return (bits & jnp.uint32(0xFFFF0000)) | (bits >> 16)


def _route_kernel(h2t_ref, wqt_ref, kbig_ref, keys2_ref, c_ref, w1_ref, p_ref):
    tn = h2t_ref.shape[-1]
    nk = PEER_KEYS
    half_rows = PEER_HEADS * PEER_HALF
    qt = _dot(wqt_ref[...], h2t_ref[...])
    sub = lax.broadcasted_iota(jnp.int32, (SUBLANES, tn), 0)

    sc1 = _dot(kbig_ref[...], qt[:half_rows, :].astype(BF16))
    slabs = [sc1[PEER_HEADS * k:PEER_HEADS * (k + 1), :] for k in range(nk)]
    a = _top_values(slabs, PEER_TOPK)

    b = [None] * PEER_TOPK
    for h in range(PEER_HEADS):
        r0 = half_rows + h * PEER_HALF
        sc = _dot(keys2_ref[h], qt[r0:r0 + PEER_HALF, :].astype(BF16))
        tiles = [sc[SUBLANES * t:SUBLANES * (t + 1), :] for t in range(nk // SUBLANES)]
        xs = _sort_desc(tiles)
        shift = SUBLANES // 2
        while shift >= 1:
            xs = _merge_top(xs, [pltpu.roll(v, shift, 0) for v in xs])
            shift //= 2
        for ib in range(nk // PACK_ROWS):
            ranks, weights = [], []
            for tile in tiles[2 * ib:2 * ib + 2]:
                rk = jnp.full_like(tile, float(PEER_TOPK))
                for r in reversed(range(PEER_TOPK)):
                    rk = jnp.where(tile >= xs[r], float(r), rk)
                ranks.append(rk)
                weights.append(jnp.exp(tile - xs[0]))
            rk16 = jnp.concatenate(ranks, axis=0).astype(BF16)
            w16 = jnp.concatenate(weights, axis=0).astype(BF16)
            for lb in range(tn // LANES):
                r0 = 2 * h * PACK_ROWS
                p_ref[lb, ib, r0:r0 + PACK_ROWS, :] = rk16[:, lb * LANES:(lb + 1) * LANES]
                p_ref[lb, ib, r0 + PACK_ROWS:r0 + 2 * PACK_ROWS, :] = w16[:, lb * LANES:(lb + 1) * LANES]
        for r in range(PEER_TOPK):
            b[r] = xs[r] if h == 0 else jnp.where(sub == h, xs[r], b[r])

    pairs = [(i, j) for i in range(PEER_TOPK) for j in range(PEER_TOPK) if (i + 1) * (j + 1) <= PEER_TOPK]
    cands = {ij: a[ij[0]] + b[ij[1]] for ij in pairs}
    tau = _kth_largest(list(cands.values()), PEER_TOPK - 1)
    top = a[0] + b[0]
    z = jnp.zeros_like(tau)
    for c in cands.values():
        z = z + jnp.where(c >= tau, jnp.exp(c - top), 0.0)
    inv_z = 1.0 / z
    need = []
    for j in range(PEER_TOPK):
        t = None
        for i in range(PEER_TOPK):
            if (i, j) in cands:
                v = jnp.where(cands[(i, j)] >= tau, a[i], jnp.inf)
                t = v if t is None else jnp.minimum(t, v)
        need.append(t)
    for k in range(nk):
        cnt = jnp.zeros_like(tau)
        for j in range(PEER_TOPK):
            cnt = jnp.where(slabs[k] >= need[j], float(j + 1), cnt)
        c_ref[k] = _pair_bits(cnt)
        w1_ref[k] = _pair_bits(jnp.exp(slabs[k] - a[0]) * inv_z)


def _route(h2t, wqt, kbig, keys2):
    d, t = h2t.shape
    tn = TN_ROUTE
    by_key = pl.BlockSpec((PEER_KEYS, PEER_HEADS, tn), lambda i: (0, 0, i))
    packed = (PEER_KEYS // PACK_ROWS, 2 * PEER_HEADS * PACK_ROWS, LANES)
    return pl.pallas_call(
        _route_kernel,
        grid=(t // tn,),
        in_specs=[pl.BlockSpec((d, tn), lambda i: (0, i)),
                  pl.BlockSpec(wqt.shape, lambda i: (0, 0)),
                  pl.BlockSpec(kbig.shape, lambda i: (0, 0)),
                  pl.BlockSpec(keys2.shape, lambda i: (0, 0, 0))],
        out_specs=[by_key, by_key, pl.BlockSpec((tn // LANES,) + packed, lambda i: (i, 0, 0, 0))],
        out_shape=[jax.ShapeDtypeStruct((PEER_KEYS, PEER_HEADS, t), jnp.uint32),
                   jax.ShapeDtypeStruct((PEER_KEYS, PEER_HEADS, t), jnp.uint32),
                   jax.ShapeDtypeStruct((t // LANES,) + packed, BF16)],
        compiler_params=_cparams("arbitrary"),
        name="route",
    )(h2t, wqt, kbig, keys2)


def _dense_kernel(flag_ref, h2t_ref, u_ref, vt_ref, c_ref, w1_ref, p_ref, x1_ref, mod_ref,
                  fg_ref, o_ref, acc_ref, act_ref, gat_ref, *, nblk):
    s = pl.program_id(0)
    n_items = pl.num_programs(0) - 2
    tn = h2t_ref.shape[-1]
    eb = u_ref.shape[0]
    rows_per_step = eb // PEER_KEYS

    @pl.when(s == 0)
    def _():
        act_ref[1] = jnp.zeros(act_ref.shape[1:], F32)
        gat_ref[0] = jnp.zeros(gat_ref.shape[1:], BF16)

    jb = jnp.clip(s - 1, 0, n_items - 1) % nblk
    jc = jnp.clip(s - 2, 0, n_items - 1) % nblk

    @pl.when(jc == 0)
    def _():
        acc_ref[...] = jnp.zeros_like(acc_ref)

    n_lb = tn // LANES

    def row_tile(ref, i1, h, ls):
        return pltpu.bitcast(jnp.broadcast_to(ref[i1, h:h + 1, ls], (SUBLANES, LANES)), BF16)

    def gate_group(il, lb, part, oth):
        i1 = jb * rows_per_step + il
        base = il * PEER_KEYS
        ls = slice(lb * LANES, (lb + 1) * LANES)
        cntb = [row_tile(c_ref, i1, h, ls) for h in range(PEER_HEADS)]
        w1b = [row_tile(w1_ref, i1, h, ls) for h in range(PEER_HEADS)]
        tiles = PEER_KEYS // PACK_ROWS // GATE_PARTS
        for ib in range(part * tiles, (part + 1) * tiles):
            g = None
            for h in range(PEER_HEADS):
                r0 = 2 * h * PACK_ROWS
                rank2 = p_ref[lb, ib, r0:r0 + PACK_ROWS, :]
                w2 = p_ref[lb, ib, r0 + PACK_ROWS:r0 + 2 * PACK_ROWS, :]
                term = jnp.minimum(jnp.maximum(cntb[h] - rank2, 0.0), w2) * w1b[h]
                g = term if g is None else g + term
            r1 = base + ib * PACK_ROWS
            act = act_ref[oth, r1:r1 + PACK_ROWS, ls]
            gat_ref[oth, r1:r1 + PACK_ROWS, ls] = jax.nn.gelu(act.astype(BF16)) * g

    def matmul_chunks(cur):
        chunks = []
        for c0 in range(0, tn, MXU_COLS):
            cs = slice(c0, c0 + MXU_COLS)
            for r0 in range(0, acc_ref.shape[0], DOT_ROWS):
                rs = slice(r0, r0 + DOT_ROWS)

                def second(rs=rs, cs=cs):
                    acc_ref[rs, cs] += _dot(vt_ref[0, rs, :], gat_ref[cur, :, cs])
                chunks.append(second)
        for c0 in range(0, tn, MXU_COLS):
            cs = slice(c0, c0 + MXU_COLS)
            for r0 in range(0, eb, DOT_ROWS):
                rs = slice(r0, r0 + DOT_ROWS)

                def first(rs=rs, cs=cs):
                    act_ref[cur, rs, cs] = _dot(u_ref[rs, :], h2t_ref[:, cs])
                chunks.append(first)
        return chunks

    def sections(cur, oth):
        groups = [(il, lb, part) for il in range(rows_per_step) for lb in range(n_lb)
                  for part in range(GATE_PARTS)]
        chunks = matmul_chunks(cur)
        per_section = len(groups) // DENSE_SECTIONS
        for sec in range(DENSE_SECTIONS):
            @pl.when(flag_ref[0] == 0)
            def _(sec=sec):
                for gi in range(sec * per_section, (sec + 1) * per_section):
                    for ci in range(gi * len(chunks) // len(groups), (gi + 1) * len(chunks) // len(groups)):
                        chunks[ci]()
                    gate_group(*groups[gi], oth)

    for parity in range(2):
        @pl.when(s % 2 == parity)
        def _(parity=parity):
            sections(parity, 1 - parity)

    @pl.when(jnp.logical_and(jc == nblk - 1, s >= 2))
    def _():
        x2 = x1_ref[...] + mod_ref[0, 5:6, :] * acc_ref[...].T
        o_ref[...] = _rms(x2, fg_ref[...])


def _dense(h2t, u, vt3, cnt, w1, packed, x1, mod3, fg, seq):
    d, t = h2t.shape
    ne = vt3.shape[0]
    per_seq = seq // TN
    n_items = (t // TN) * ne
    item = lambda s, lag: jnp.clip(s - lag, 0, n_items - 1)
    tile = lambda s, lag: item(s, lag) // ne
    block = lambda s, lag: item(s, lag) % ne
    by_key = pl.BlockSpec((PEER_KEYS, PEER_HEADS, TN), lambda s: (0, 0, tile(s, 1)))
    return pl.pallas_call(
        functools.partial(_dense_kernel, nblk=ne),
        grid=(n_items + 2,),
        in_specs=[pl.BlockSpec(memory_space=pltpu.SMEM),
                  pl.BlockSpec((d, TN), lambda s: (0, tile(s, 0))),
                  pl.BlockSpec((EB, d), lambda s: (block(s, 0), 0)),
                  pl.BlockSpec((1, d, EB), lambda s: (block(s, 2), 0, 0)),
                  by_key, by_key,
                  pl.BlockSpec((TN // LANES,) + packed.shape[1:], lambda s: (tile(s, 1), 0, 0, 0)),
                  pl.BlockSpec((TN, d), lambda s: (tile(s, 2), 0)),
                  pl.BlockSpec((1, 6, d), lambda s: (tile(s, 2) // per_seq, 0, 0)),
                  pl.BlockSpec(fg.shape, lambda s: (0, 0))],
        out_specs=pl.BlockSpec((TN, d), lambda s: (tile(s, 2), 0)),
        out_shape=jax.ShapeDtypeStruct((t, d), F32),
        scratch_shapes=[pltpu.VMEM((d, TN), F32), pltpu.VMEM((2, EB, TN), F32),
                        pltpu.VMEM((2, EB, TN), BF16)],
        compiler_params=_cparams("arbitrary"),
        name="dense",
    )(jnp.zeros((1,), jnp.int32), h2t, u, vt3, cnt, w1, packed, x1, mod3, fg)


def _rot_half_cols(w):
    half = w.shape[-1] // 2
    return jnp.concatenate([-w[..., half:], w[..., :half]], axis=-1)


def _pad_cols(w, before, total):
    return jnp.pad(w, [(0, 0)] * (w.ndim - 1) + [(before, total - before - w.shape[-1])])


def kernel(x, c, positions, w_ada, b_ada, norm1_g, w_in, conv_w, conv_b, lru_wa, lru_ba, lru_wx, lru_bx,
           lru_lambda, q_norm_g, w_uq, kv_norm_g, w_ukv, lru_out_g, mla_out_g, w_out, norm2_g, peer_wq,
           peer_keys, peer_u, peer_v, final_g):
    bsz, seq, d = x.shape
    depth = w_ada.shape[0]
    lw = d // 2
    qr = w_uq.shape[1]
    kvr = w_ukv.shape[1]
    assert depth == 1, "the final norm is fused into the (single) layer's PEER kernel"
    assert seq % TM == 0 and seq % TN == 0 and seq % TQ == 0 and seq % TS == 0
    assert d == MLA_HEADS * LANES and lw == MLA_HEADS * V_DIM

    inv_freq = 1.0 / (ROPE_THETA ** (jnp.arange(0, QK_ROPE, 2, dtype=F32) / QK_ROPE))
    invf = _pad_cols(jnp.concatenate([inv_freq, inv_freq])[None, :], QK_NOPE, LANES)
    pos3 = positions.reshape(bsz, seq, 1)
    row = lambda v: v.reshape(1, -1)

    for l in range(depth):
        mod3 = _ada(c, w_ada[l], b_ada[l]).reshape(bsz, 6, d)

        wi = w_in[l]
        o0, o1, o2, o3 = 2 * lw, 2 * lw + qr, 2 * lw + qr + kvr, 2 * lw + qr + kvr + QK_ROPE
        wkr = wi[:, o2:o3]
        wkr2 = jnp.concatenate([_pad_cols(wkr, QK_NOPE, LANES),
                                _pad_cols(_rot_half_cols(wkr), QK_NOPE, LANES)], axis=1).astype(BF16)
        uq = w_uq[l].reshape(qr, MLA_HEADS, QK_NOPE + QK_ROPE)
        uq_main = _pad_cols(uq, 0, LANES).reshape(qr, d)
        uq_rot = _pad_cols(_rot_half_cols(uq[..., QK_NOPE:]), QK_NOPE, LANES).reshape(qr, d)
        wuq2 = jnp.concatenate([uq_main, uq_rot], axis=1).astype(BF16)
        ukv = w_ukv[l].reshape(kvr, MLA_HEADS, QK_NOPE + V_DIM)
        wuk = _pad_cols(ukv[..., :QK_NOPE], 0, LANES).reshape(kvr, d).astype(BF16)
        uv = ukv[..., QK_NOPE:]
        odd = (jnp.arange(MLA_HEADS) % 2 == 1)[None, :, None]
        wuv = jnp.where(odd, _pad_cols(uv, V_DIM, LANES), _pad_cols(uv, 0, LANES)).reshape(kvr, d).astype(BF16)
        half = jnp.arange(LANES)[None, :] >= V_DIM
        vones = (half != odd[0]).astype(F32).reshape(1, d)

        lru_in, q, k, v = _pre(x, mod3, row(norm1_g[l]), pos3, invf, wi[:, :o0].astype(BF16),
                               wi[:, o0:o1].astype(BF16), wi[:, o1:o2].astype(BF16), wkr2,
                               row(q_norm_g[l]), wuq2, row(kv_norm_g[l]), wuk, wuv, vones)

        eye = jnp.eye(LRU_BLOCKS, dtype=F32)
        blockdiag = lambda w: jnp.einsum("hij,hg->higj", w, eye).reshape(lw, lw)
        wg = jnp.concatenate([blockdiag(lru_wa[l]), blockdiag(lru_wx[l])], axis=1).astype(BF16)
        bg = jnp.concatenate([lru_ba[l], lru_bx[l]])[None, :]
        yl = _lru(lru_in, conv_w[l], row(conv_b[l]), wg, bg, row(lru_lambda[l]), row(lru_out_g[l]))

        ym = _attn(q, k, v, row(mla_out_g[l]))

        wo = w_out[l].astype(BF16)
        x1, h2t = _post(yl, ym, x, mod3, wo[:lw], wo[lw:], row(norm2_g[l]))

        wqt = (peer_wq[l].reshape(d, PEER_HEADS, 2, PEER_HALF).transpose(2, 1, 3, 0)
               .reshape(2 * PEER_HEADS * PEER_HALF, d).astype(BF16))
        keys = peer_keys[l].astype(BF16)
        kbig = jnp.einsum("hkd,hg->khgd", keys[:, 0], jnp.eye(PEER_HEADS, dtype=BF16)).reshape(
            PEER_KEYS * PEER_HEADS, PEER_HEADS * PEER_HALF)
        cnt, w1, packed = _route(h2t, wqt, kbig, keys[:, 1])

        ne = peer_u.shape[1] // EB
        vt3 = peer_v[l].astype(BF16).reshape(ne, EB, d).transpose(0, 2, 1)
        out = _dense(h2t, peer_u[l].astype(BF16), vt3, cnt, w1, packed,
                     x1.reshape(bsz * seq, d), mod3, row(final_g), seq)
        x = out.reshape(bsz, seq, d)
    return x
```

```python
import functools

import jax
import jax.numpy as jnp
from jax import lax
from jax.experimental import pallas as pl
from jax.experimental.pallas import tpu as pltpu

F32 = jnp.float32
BF16 = jnp.bfloat16

LRU_BLOCKS = 8
CONV_WIDTH = 4
LRU_C = 8.0
MLA_HEADS = 8
QK_NOPE = 64
QK_ROPE = 32
V_DIM = 64
ROPE_THETA = 10000.0
PEER_HEADS = 8
PEER_KEYS = 128
PEER_HALF = 128
PEER_TOPK = 16
EPS = 1e-6
LOG2_E = 1.4426950408889634

LANES = 128
SUBLANES = 8
VMEM_LIMIT = 56 * 1024 * 1024

TM = 512
TS = 256
TQ = 256
TN_ROUTE = 256
TN = 512
EB = 1024
MXU_COLS = 256
DOT_ROWS = 128
GATE_PARTS = 1
DENSE_SECTIONS = 1


def _cparams(*sem, flags=None):
    return pltpu.CompilerParams(dimension_semantics=sem, vmem_limit_bytes=VMEM_LIMIT, flags=flags)


def _rms(x, g):
    return x * lax.rsqrt(jnp.mean(x * x, axis=-1, keepdims=True) + EPS) * g


def _dot(a, b):
    return jnp.dot(a, b, preferred_element_type=F32)


def _ada_kernel(c_ref, w_ref, b_ref, o_ref):
    ca = jax.nn.silu(c_ref[...])
    o_ref[...] = jnp.dot(ca, w_ref[...], preferred_element_type=F32,
                         precision=lax.Precision.HIGHEST) + b_ref[...]


def _ada(c, w, b):
    bsz, d = c.shape
    n = w.shape[1]
    return pl.pallas_call(
        _ada_kernel,
        grid=(n // d,),
        in_specs=[pl.BlockSpec((bsz, d), lambda j: (0, 0)),
                  pl.BlockSpec((d, d), lambda j: (0, j)),
                  pl.BlockSpec((1, d), lambda j: (0, j))],
        out_specs=pl.BlockSpec((bsz, d), lambda j: (0, j)),
        out_shape=jax.ShapeDtypeStruct((bsz, n), F32),
        compiler_params=_cparams("arbitrary"),
        name="ada",
    )(c, w, b.reshape(1, n))


def _pre_kernel(x_ref, mod_ref, g1_ref, pos_ref, invf_ref, wlru_ref, wq_ref, wkv_ref, wkr_ref,
                qg_ref, wuq_ref, kvg_ref, wuk_ref, wuv_ref, vones_ref, lru_ref, q_ref, k_ref, v_ref):
    d = x_ref.shape[-1]
    x = x_ref[0]
    shift = mod_ref[0, 0:1, :]
    scale = mod_ref[0, 1:2, :]
    hb = (_rms(x, g1_ref[...]) * (1.0 + scale) + shift).astype(BF16)
    lru_ref[0] = _dot(hb, wlru_ref[...])

    ang = pos_ref[0].astype(F32) * invf_ref[...]
    cos = jnp.cos(ang)
    sin = jnp.sin(ang)

    qn = _rms(_dot(hb, wq_ref[...]), qg_ref[...]).astype(BF16)
    q2 = _dot(qn, wuq_ref[...])
    q_scale = (QK_NOPE + QK_ROPE) ** -0.5 * LOG2_E
    for h in range(MLA_HEADS):
        lo = h * LANES
        q_ref[0, :, lo:lo + LANES] = ((q2[:, lo:lo + LANES] * cos
                                       + q2[:, d + lo:d + lo + LANES] * sin) * q_scale).astype(BF16)

    kvn = _rms(_dot(hb, wkv_ref[...]), kvg_ref[...]).astype(BF16)
    kn = _dot(kvn, wuk_ref[...])
    v_ref[0] = (_dot(kvn, wuv_ref[...]) + vones_ref[...]).astype(BF16)
    kr2 = _dot(hb, wkr_ref[...])
    krot = kr2[:, :LANES] * cos + kr2[:, LANES:] * sin
    for h in range(MLA_HEADS):
        lo = h * LANES
        k_ref[0, :, lo:lo + LANES] = (kn[:, lo:lo + LANES] + krot).astype(BF16)


def _pre(x, mod3, g1, pos3, invf, wlru, wq, wkv, wkr2, qg, wuq2, kvg, wuk, wuv, vones):
    bsz, s, d = x.shape
    full = lambda a: pl.BlockSpec(a.shape, lambda b, i: (0,) * a.ndim)
    tok = lambda w: pl.BlockSpec((1, TM, w), lambda b, i: (b, i, 0))
    return pl.pallas_call(
        _pre_kernel,
        grid=(bsz, s // TM),
        in_specs=[tok(d), pl.BlockSpec((1, 6, d), lambda b, i: (b, 0, 0)), full(g1), tok(1), full(invf),
                  full(wlru), full(wq), full(wkv), full(wkr2), full(qg), full(wuq2), full(kvg),
                  full(wuk), full(wuv), full(vones)],
        out_specs=[tok(d), tok(d), tok(d), tok(d)],
        out_shape=[jax.ShapeDtypeStruct((bsz, s, d), F32),
                   jax.ShapeDtypeStruct((bsz, s, d), BF16),
                   jax.ShapeDtypeStruct((bsz, s, d), BF16),
                   jax.ShapeDtypeStruct((bsz, s, d), BF16)],
        compiler_params=_cparams("arbitrary", "arbitrary"),
        name="pre",
    )(x, mod3, g1, pos3, invf, wlru, wq, wkv, wkr2, qg, wuq2, kvg, wuk, wuv, vones)


def _lru_kernel(x_ref, cw_ref, cb_ref, wg_ref, bg_ref, lam_ref, og_ref, o_ref, prev_ref, h_ref):
    w = o_ref.shape[-1]

    @pl.when(pl.program_id(1) == 0)
    def _():
        prev_ref[...] = jnp.zeros_like(prev_ref)
        h_ref[...] = jnp.zeros_like(h_ref)

    x = x_ref[0, :, :w]
    gate = x_ref[0, :, w:]
    tail = prev_ref[...]
    row = lax.broadcasted_iota(jnp.int32, x.shape, 0)
    row8 = lax.broadcasted_iota(jnp.int32, tail.shape, 0)

    xc = cb_ref[...]
    for k in range(CONV_WIDTH - 1):
        back = CONV_WIDTH - 1 - k
        shifted = pltpu.roll(x, back, 0)
        first = jnp.where(row8 < back, pltpu.roll(tail, back, 0), shifted[:SUBLANES, :])
        shifted = jnp.concatenate([first, shifted[SUBLANES:, :]], axis=0)
        xc = xc + cw_ref[k:k + 1, :] * shifted
    xc = xc + cw_ref[CONV_WIDTH - 1:CONV_WIDTH, :] * x
    prev_ref[...] = x[TS - SUBLANES:, :]

    pre = _dot(xc.astype(BF16), wg_ref[...]) + bg_ref[...]
    r = jax.nn.sigmoid(pre[:, :w])
    i = jax.nn.sigmoid(pre[:, w:])
    log_a = -LRU_C * r * jax.nn.softplus(-lam_ref[...])
    a = jnp.exp(log_a)
    b = jnp.sqrt(1.0 - a * a) * (i * xc)

    sub = row % SUBLANES
    dist = 1
    while dist < SUBLANES:
        keep = sub >= dist
        b = jnp.where(keep, a * pltpu.roll(b, dist, 0) + b, b)
        a = jnp.where(keep, a * pltpu.roll(a, dist, 0), a)
        dist *= 2
    state = h_ref[0:1, :]
    groups = []
    for g0 in range(0, TS, SUBLANES):
        hg = b[g0:g0 + SUBLANES, :] + a[g0:g0 + SUBLANES, :] * state
        groups.append(hg)
        state = hg[SUBLANES - 1:SUBLANES, :]
    h = jnp.concatenate(groups, axis=0)
    h_ref[...] = jnp.broadcast_to(state, h_ref.shape)

    y = jax.nn.gelu(gate) * h
    o_ref[0] = _rms(y, og_ref[...]).astype(BF16)


def _lru(lru_in, cw, cb, wg, bg, lam, og):
    bsz, s, d = lru_in.shape
    w = d // 2
    full = lambda a: pl.BlockSpec(a.shape, lambda b, t: (0,) * a.ndim)
    return pl.pallas_call(
        _lru_kernel,
        grid=(bsz, s // TS),
        in_specs=[pl.BlockSpec((1, TS, d), lambda b, t: (b, t, 0)),
                  full(cw), full(cb), full(wg), full(bg), full(lam), full(og)],
        out_specs=pl.BlockSpec((1, TS, w), lambda b, t: (b, t, 0)),
        out_shape=jax.ShapeDtypeStruct((bsz, s, w), BF16),
        scratch_shapes=[pltpu.VMEM((SUBLANES, w), F32), pltpu.VMEM((SUBLANES, w), F32)],
        compiler_params=_cparams("arbitrary", "arbitrary"),
        name="lru",
    )(lru_in, cw, cb, wg, bg, lam, og)


def _attn_kernel(q_ref, k_ref, v_ref, g_ref, o_ref, m_ref, acc_ref, bias_ref):
    i = pl.program_id(1)
    nt = (((1,), (1,)), ((), ()))
    row = lax.broadcasted_iota(jnp.int32, (TQ, TQ), 0)
    col = lax.broadcasted_iota(jnp.int32, (TQ, TQ), 1)
    lane = lax.broadcasted_iota(jnp.int32, (TQ, LANES), 1)

    m_ref[...] = jnp.full(m_ref.shape, -jnp.inf, F32)
    acc_ref[...] = jnp.zeros(acc_ref.shape, F32)
    bias_ref[...] = jnp.where(col <= row, 0.0, -jnp.inf)

    def block(j, masked):
        start = pl.multiple_of(j * TQ, TQ)

        def scores(h):
            lo = h * LANES
            s = lax.dot_general(q_ref[0, :, lo:lo + LANES], k_ref[0, pl.ds(start, TQ), lo:lo + LANES], nt,
                                preferred_element_type=F32)
            return s + bias_ref[...] if masked else s

        def softmax_step(h, s):
            m_old = m_ref[h]
            m_new = jnp.maximum(m_old, jnp.max(s, axis=-1, keepdims=True))
            m_ref[h] = m_new
            alpha = jnp.exp2(m_old - m_new)
            p = [jnp.exp2(s[:, c:c + LANES] - m_new) for c in range(0, TQ, LANES)]
            return alpha, jnp.concatenate(p, axis=-1).astype(BF16)

        def weighted_values(h, alpha, p):
            lo = h * LANES
            acc_ref[h] = alpha * acc_ref[h] + _dot(p, v_ref[0, pl.ds(start, TQ), lo:lo + LANES])

        pending_s, pending_p = {}, {}
        for t in range(MLA_HEADS + 2):
            if t < MLA_HEADS:
                pending_s[t] = scores(t)
            if 1 <= t <= MLA_HEADS:
                pending_p[t - 1] = softmax_step(t - 1, pending_s.pop(t - 1))
            if t >= 2:
                weighted_values(t - 2, *pending_p.pop(t - 2))

    def unmasked(j, carry):
        block(j, False)
        return carry

    def diagonal(j, carry):
        block(j, True)
        return carry

    lax.fori_loop(0, i, unmasked, 0)
    lax.fori_loop(i, i + 1, diagonal, 0)

    pairs = []
    for hp in range(MLA_HEADS // 2):
        even, odd = acc_ref[2 * hp], acc_ref[2 * hp + 1]
        num = jnp.where(lane < V_DIM, even, odd)
        den = jnp.where(lane < V_DIM, pltpu.roll(even, V_DIM, 1), pltpu.roll(odd, V_DIM, 1))
        pairs.append(num / den)
    y = jnp.concatenate(pairs, axis=-1)
    o_ref[0] = _rms(y, g_ref[...]).astype(BF16)


def _attn(q, k, v, g):
    bsz, s, d = q.shape
    w = g.shape[-1]
    return pl.pallas_call(
        _attn_kernel,
        grid=(bsz, s // TQ),
        in_specs=[pl.BlockSpec((1, TQ, d), lambda b, i: (b, i, 0)),
                  pl.BlockSpec((1, s, d), lambda b, i: (b, 0, 0)),
                  pl.BlockSpec((1, s, d), lambda b, i: (b, 0, 0)),
                  pl.BlockSpec(g.shape, lambda b, i: (0, 0))],
        out_specs=pl.BlockSpec((1, TQ, w), lambda b, i: (b, i, 0)),
        out_shape=jax.ShapeDtypeStruct((bsz, s, w), BF16),
        scratch_shapes=[pltpu.VMEM((MLA_HEADS, TQ, LANES), F32), pltpu.VMEM((MLA_HEADS, TQ, LANES), F32),
                        pltpu.VMEM((TQ, TQ), F32)],
        compiler_params=_cparams("arbitrary", "arbitrary"),
        name="attn",
    )(q, k, v, g)


def _post_kernel(yl_ref, ym_ref, x_ref, mod_ref, wol_ref, wom_ref, g2_ref, x1_ref, h2t_ref):
    mix = _dot(yl_ref[0], wol_ref[...]) + _dot(ym_ref[0], wom_ref[...])
    x1 = x_ref[0] + mod_ref[0, 2:3, :] * mix
    x1_ref[0] = x1
    h2 = _rms(x1, g2_ref[...]) * (1.0 + mod_ref[0, 4:5, :]) + mod_ref[0, 3:4, :]
    h2t_ref[...] = h2.T.astype(BF16)


def _post(yl, ym, x, mod3, wol, wom, g2):
    bsz, s, d = x.shape
    w = yl.shape[-1]
    nt = s // TM
    full = lambda a: pl.BlockSpec(a.shape, lambda b, i: (0,) * a.ndim)
    tok = lambda c: pl.BlockSpec((1, TM, c), lambda b, i: (b, i, 0))
    return pl.pallas_call(
        _post_kernel,
        grid=(bsz, nt),
        in_specs=[tok(w), tok(w), tok(d), pl.BlockSpec((1, 6, d), lambda b, i: (b, 0, 0)),
                  full(wol), full(wom), full(g2)],
        out_specs=[tok(d), pl.BlockSpec((d, TM), lambda b, i: (0, b * nt + i))],
        out_shape=[jax.ShapeDtypeStruct((bsz, s, d), F32),
                   jax.ShapeDtypeStruct((d, bsz * s), BF16)],
        compiler_params=_cparams("arbitrary", "arbitrary"),
        name="post",
    )(yl, ym, x, mod3, wol, wom, g2)


def _sort_pairs(n):
    pairs = []

    def merge(lo, hi, r):
        step = r * 2
        if step < hi - lo:
            merge(lo, hi, step)
            merge(lo + r, hi, step)
            pairs.extend((i, i + r) for i in range(lo + r, hi - r, step))
        else:
            pairs.append((lo, lo + r))

    def sort(lo, hi):
        if hi - lo >= 1:
            mid = lo + (hi - lo) // 2
            sort(lo, mid)
            sort(mid + 1, hi)
            merge(lo, hi, 1)

    sort(0, n - 1)
    return pairs


def _sort_desc(xs):
    xs = list(xs)
    for i, j in _sort_pairs(len(xs)):
        a, b = xs[i], xs[j]
        xs[i], xs[j] = jnp.maximum(a, b), jnp.minimum(a, b)
    return xs


def _merge_top(a, b):
    n = len(a)
    xs = [jnp.maximum(a[i], b[n - 1 - i]) for i in range(n)]
    dist = n // 2
    while dist >= 1:
        for i in range(n):
            if i & dist == 0:
                p, q = xs[i], xs[i + dist]
                xs[i], xs[i + dist] = jnp.maximum(p, q), jnp.minimum(p, q)
        dist //= 2
    return xs


def _kth_largest(vals, k):
    n = 1
    while n < len(vals):
        n *= 2
    present = [True] * len(vals) + [False] * (n - len(vals))
    ops = []
    for i, j in _sort_pairs(n):
        if present[i] and present[j]:
            ops.append(("cmp", i, j))
        elif present[j]:
            ops.append(("mov", i, j))
            present[i], present[j] = True, False
    need = {k}
    live = []
    for op in reversed(ops):
        kind, i, j = op
        if kind == "mov":
            if i in need:
                need.discard(i)
                need.add(j)
                live.append((kind, i, j, True, False))
        else:
            hi, lo = i in need, j in need
            if hi or lo:
                need.update((i, j))
                live.append((kind, i, j, hi, lo))
    xs = list(vals) + [None] * (n - len(vals))
    for kind, i, j, hi, lo in reversed(live):
        a, b = xs[i], xs[j]
        if kind == "mov":
            xs[i], xs[j] = b, None
        else:
            xs[i] = jnp.maximum(a, b) if hi else None
            xs[j] = jnp.minimum(a, b) if lo else None
    return xs[k]


def _top_values(slabs, n):
    groups = [_sort_desc(slabs[g:g + n]) for g in range(0, len(slabs), n)]
    while len(groups) > 1:
        groups = [_merge_top(groups[g], groups[g + 1]) for g in range(0, len(groups), 2)]
    return groups[0]


PACK_ROWS = 2 * SUBLANES


def _pair_bits(x):
    bits = pltpu.bitcast(x.astype(BF16).astype(F32), jnp.uint32)
    return (bits & jnp.uint32(0xFFFF0000)) | (bits >> 16)


def _route_kernel(h2t_ref, wqt_ref, kbig_ref, keys2_ref, c_ref, w1_ref, p_ref):
    tn = h2t_ref.shape[-1]
    nk = PEER_KEYS
    half_rows = PEER_HEADS * PEER_HALF
    qt = _dot(wqt_ref[...], h2t_ref[...])
    sub = lax.broadcasted_iota(jnp.int32, (SUBLANES, tn), 0)

    sc1 = _dot(kbig_ref[...], qt[:half_rows, :].astype(BF16))
    slabs = [sc1[PEER_HEADS * k:PEER_HEADS * (k + 1), :] for k in range(nk)]
    a = _top_values(slabs, PEER_TOPK)

    b = [None] * PEER_TOPK
    for h in range(PEER_HEADS):
        r0 = half_rows + h * PEER_HALF
        sc = _dot(keys2_ref[h], qt[r0:r0 + PEER_HALF, :].astype(BF16))
        tiles = [sc[SUBLANES * t:SUBLANES * (t + 1), :] for t in range(nk // SUBLANES)]
        xs = _sort_desc(tiles)
        shift = SUBLANES // 2
        while shift >= 1:
            xs = _merge_top(xs, [pltpu.roll(v, shift, 0) for v in xs])
            shift //= 2
        for ib in range(nk // PACK_ROWS):
            ranks, weights = [], []
            for tile in tiles[2 * ib:2 * ib + 2]:
                rk = jnp.full_like(tile, float(PEER_TOPK))
                for r in reversed(range(PEER_TOPK)):
                    rk = jnp.where(tile >= xs[r], float(r), rk)
                ranks.append(rk)
                weights.append(jnp.exp(tile - xs[0]))
            rk16 = jnp.concatenate(ranks, axis=0).astype(BF16)
            w16 = jnp.concatenate(weights, axis=0).astype(BF16)
            for lb in range(tn // LANES):
                r0 = 2 * h * PACK_ROWS
                p_ref[lb, ib, r0:r0 + PACK_ROWS, :] = rk16[:, lb * LANES:(lb + 1) * LANES]
                p_ref[lb, ib, r0 + PACK_ROWS:r0 + 2 * PACK_ROWS, :] = w16[:, lb * LANES:(lb + 1) * LANES]
        for r in range(PEER_TOPK):
            b[r] = xs[r] if h == 0 else jnp.where(sub == h, xs[r], b[r])

    pairs = [(i, j) for i in range(PEER_TOPK) for j in range(PEER_TOPK) if (i + 1) * (j + 1) <= PEER_TOPK]
    cands = {ij: a[ij[0]] + b[ij[1]] for ij in pairs}
    tau = _kth_largest(list(cands.values()), PEER_TOPK - 1)
    top = a[0] + b[0]
    z = jnp.zeros_like(tau)
    for c in cands.values():
        z = z + jnp.where(c >= tau, jnp.exp(c - top), 0.0)
    inv_z = 1.0 / z
    need = []
    for j in range(PEER_TOPK):
        t = None
        for i in range(PEER_TOPK):
            if (i, j) in cands:
                v = jnp.where(cands[(i, j)] >= tau, a[i], jnp.inf)
                t = v if t is None else jnp.minimum(t, v)
        need.append(t)
    for k in range(nk):
        cnt = jnp.zeros_like(tau)
        for j in range(PEER_TOPK):
            cnt = jnp.where(slabs[k] >= need[j], float(j + 1), cnt)
        c_ref[k] = _pair_bits(cnt)
        w1_ref[k] = _pair_bits(jnp.exp(slabs[k] - a[0]) * inv_z)


def _route(h2t, wqt, kbig, keys2):
    d, t = h2t.shape
    tn = TN_ROUTE
    by_key = pl.BlockSpec((PEER_KEYS, PEER_HEADS, tn), lambda i: (0, 0, i))
    packed = (PEER_KEYS // PACK_ROWS, 2 * PEER_HEADS * PACK_ROWS, LANES)
    return pl.pallas_call(
        _route_kernel,
        grid=(t // tn,),
        in_specs=[pl.BlockSpec((d, tn), lambda i: (0, i)),
                  pl.BlockSpec(wqt.shape, lambda i: (0, 0)),
                  pl.BlockSpec(kbig.shape, lambda i: (0, 0)),
                  pl.BlockSpec(keys2.shape, lambda i: (0, 0, 0))],
        out_specs=[by_key, by_key, pl.BlockSpec((tn // LANES,) + packed, lambda i: (i, 0, 0, 0))],
        out_shape=[jax.ShapeDtypeStruct((PEER_KEYS, PEER_HEADS, t), jnp.uint32),
                   jax.ShapeDtypeStruct((PEER_KEYS, PEER_HEADS, t), jnp.uint32),
                   jax.ShapeDtypeStruct((t // LANES,) + packed, BF16)],
        compiler_params=_cparams("arbitrary"),
        name="route",
    )(h2t, wqt, kbig, keys2)


def _dense_kernel(flag_ref, h2t_ref, u_ref, vt_ref, c_ref, w1_ref, p_ref, x1_ref, mod_ref,
                  fg_ref, o_ref, acc_ref, act_ref, gat_ref, *, nblk):
    s = pl.program_id(0)
    n_items = pl.num_programs(0) - 2
    tn = h2t_ref.shape[-1]
    eb = u_ref.shape[0]
    rows_per_step = eb // PEER_KEYS

    @pl.when(s == 0)
    def _():
        act_ref[1] = jnp.zeros(act_ref.shape[1:], F32)
        gat_ref[0] = jnp.zeros(gat_ref.shape[1:], BF16)

    jb = jnp.clip(s - 1, 0, n_items - 1) % nblk
    jc = jnp.clip(s - 2, 0, n_items - 1) % nblk

    @pl.when(jc == 0)
    def _():
        acc_ref[...] = jnp.zeros_like(acc_ref)

    n_lb = tn // LANES

    def row_tile(ref, i1, h, ls):
        return pltpu.bitcast(jnp.broadcast_to(ref[i1, h:h + 1, ls], (SUBLANES, LANES)), BF16)

    def gate_group(il, lb, part, oth):
        i1 = jb * rows_per_step + il
        base = il * PEER_KEYS
        ls = slice(lb * LANES, (lb + 1) * LANES)
        cntb = [row_tile(c_ref, i1, h, ls) for h in range(PEER_HEADS)]
        w1b = [row_tile(w1_ref, i1, h, ls) for h in range(PEER_HEADS)]
        tiles = PEER_KEYS // PACK_ROWS // GATE_PARTS
        for ib in range(part * tiles, (part + 1) * tiles):
            g = None
            for h in range(PEER_HEADS):
                r0 = 2 * h * PACK_ROWS
                rank2 = p_ref[lb, ib, r0:r0 + PACK_ROWS, :]
                w2 = p_ref[lb, ib, r0 + PACK_ROWS:r0 + 2 * PACK_ROWS, :]
                term = jnp.minimum(jnp.maximum(cntb[h] - rank2, 0.0), w2) * w1b[h]
                g = term if g is None else g + term
            r1 = base + ib * PACK_ROWS
            act = act_ref[oth, r1:r1 + PACK_ROWS, ls]
            gat_ref[oth, r1:r1 + PACK_ROWS, ls] = jax.nn.gelu(act.astype(BF16)) * g

    def matmul_chunks(cur):
        chunks = []
        for c0 in range(0, tn, MXU_COLS):
            cs = slice(c0, c0 + MXU_COLS)
            for r0 in range(0, acc_ref.shape[0], DOT_ROWS):
                rs = slice(r0, r0 + DOT_ROWS)

                def second(rs=rs, cs=cs):
                    acc_ref[rs, cs] += _dot(vt_ref[0, rs, :], gat_ref[cur, :, cs])
                chunks.append(second)
        for c0 in range(0, tn, MXU_COLS):
            cs = slice(c0, c0 + MXU_COLS)
            for r0 in range(0, eb, DOT_ROWS):
                rs = slice(r0, r0 + DOT_ROWS)

                def first(rs=rs, cs=cs):
                    act_ref[cur, rs, cs] = _dot(u_ref[rs, :], h2t_ref[:, cs])
                chunks.append(first)
        return chunks

    def sections(cur, oth):
        groups = [(il, lb, part) for il in range(rows_per_step) for lb in range(n_lb)
                  for part in range(GATE_PARTS)]
        chunks = matmul_chunks(cur)
        per_section = len(groups) // DENSE_SECTIONS
        for sec in range(DENSE_SECTIONS):
            @pl.when(flag_ref[0] == 0)
            def _(sec=sec):
                for gi in range(sec * per_section, (sec + 1) * per_section):
                    for ci in range(gi * len(chunks) // len(groups), (gi + 1) * len(chunks) // len(groups)):
                        chunks[ci]()
                    gate_group(*groups[gi], oth)

    for parity in range(2):
        @pl.when(s % 2 == parity)
        def _(parity=parity):
            sections(parity, 1 - parity)

    @pl.when(jnp.logical_and(jc == nblk - 1, s >= 2))
    def _():
        x2 = x1_ref[...] + mod_ref[0, 5:6, :] * acc_ref[...].T
        o_ref[...] = _rms(x2, fg_ref[...])


def _dense(h2t, u, vt3, cnt, w1, packed, x1, mod3, fg, seq):
    d, t = h2t.shape
    ne = vt3.shape[0]
    per_seq = seq // TN
    n_items = (t // TN) * ne
    item = lambda s, lag: jnp.clip(s - lag, 0, n_items - 1)
    tile = lambda s, lag: item(s, lag) // ne
    block = lambda s, lag: item(s, lag) % ne
    by_key = pl.BlockSpec((PEER_KEYS, PEER_HEADS, TN), lambda s: (0, 0, tile(s, 1)))
    return pl.pallas_call(
        functools.partial(_dense_kernel, nblk=ne),
        grid=(n_items + 2,),
        in_specs=[pl.BlockSpec(memory_space=pltpu.SMEM),
                  pl.BlockSpec((d, TN), lambda s: (0, tile(s, 0))),
                  pl.BlockSpec((EB, d), lambda s: (block(s, 0), 0)),
                  pl.BlockSpec((1, d, EB), lambda s: (block(s, 2), 0, 0)),
                  by_key, by_key,
                  pl.BlockSpec((TN // LANES,) + packed.shape[1:], lambda s: (tile(s, 1), 0, 0, 0)),
                  pl.BlockSpec((TN, d), lambda s: (tile(s, 2), 0)),
                  pl.BlockSpec((1, 6, d), lambda s: (tile(s, 2) // per_seq, 0, 0)),
                  pl.BlockSpec(fg.shape, lambda s: (0, 0))],
        out_specs=pl.BlockSpec((TN, d), lambda s: (tile(s, 2), 0)),
        out_shape=jax.ShapeDtypeStruct((t, d), F32),
        scratch_shapes=[pltpu.VMEM((d, TN), F32), pltpu.VMEM((2, EB, TN), F32),
                        pltpu.VMEM((2, EB, TN), BF16)],
        compiler_params=_cparams("arbitrary"),
        name="dense",
    )(jnp.zeros((1,), jnp.int32), h2t, u, vt3, cnt, w1, packed, x1, mod3, fg)


def _rot_half_cols(w):
    half = w.shape[-1] // 2
    return jnp.concatenate([-w[..., half:], w[..., :half]], axis=-1)


def _pad_cols(w, before, total):
    return jnp.pad(w, [(0, 0)] * (w.ndim - 1) + [(before, total - before - w.shape[-1])])


def kernel(x, c, positions, w_ada, b_ada, norm1_g, w_in, conv_w, conv_b, lru_wa, lru_ba, lru_wx, lru_bx,
           lru_lambda, q_norm_g, w_uq, kv_norm_g, w_ukv, lru_out_g, mla_out_g, w_out, norm2_g, peer_wq,
           peer_keys, peer_u, peer_v, final_g):
    bsz, seq, d = x.shape
    depth = w_ada.shape[0]
    lw = d // 2
    qr = w_uq.shape[1]
    kvr = w_ukv.shape[1]
    assert depth == 1, "the final norm is fused into the (single) layer's PEER kernel"
    assert seq % TM == 0 and seq % TN == 0 and seq % TQ == 0 and seq % TS == 0
    assert d == MLA_HEADS * LANES and lw == MLA_HEADS * V_DIM

    inv_freq = 1.0 / (ROPE_THETA ** (jnp.arange(0, QK_ROPE, 2, dtype=F32) / QK_ROPE))
    invf = _pad_cols(jnp.concatenate([inv_freq, inv_freq])[None, :], QK_NOPE, LANES)
    pos3 = positions.reshape(bsz, seq, 1)
    row = lambda v: v.reshape(1, -1)

    for l in range(depth):
        mod3 = _ada(c, w_ada[l], b_ada[l]).reshape(bsz, 6, d)

        wi = w_in[l]
        o0, o1, o2, o3 = 2 * lw, 2 * lw + qr, 2 * lw + qr + kvr, 2 * lw + qr + kvr + QK_ROPE
        wkr = wi[:, o2:o3]
        wkr2 = jnp.concatenate([_pad_cols(wkr, QK_NOPE, LANES),
                                _pad_cols(_rot_half_cols(wkr), QK_NOPE, LANES)], axis=1).astype(BF16)
        uq = w_uq[l].reshape(qr, MLA_HEADS, QK_NOPE + QK_ROPE)
        uq_main = _pad_cols(uq, 0, LANES).reshape(qr, d)
        uq_rot = _pad_cols(_rot_half_cols(uq[..., QK_NOPE:]), QK_NOPE, LANES).reshape(qr, d)
        wuq2 = jnp.concatenate([uq_main, uq_rot], axis=1).astype(BF16)
        ukv = w_ukv[l].reshape(kvr, MLA_HEADS, QK_NOPE + V_DIM)
        wuk = _pad_cols(ukv[..., :QK_NOPE], 0, LANES).reshape(kvr, d).astype(BF16)
        uv = ukv[..., QK_NOPE:]
        odd = (jnp.arange(MLA_HEADS) % 2 == 1)[None, :, None]
        wuv = jnp.where(odd, _pad_cols(uv, V_DIM, LANES), _pad_cols(uv, 0, LANES)).reshape(kvr, d).astype(BF16)
        half = jnp.arange(LANES)[None, :] >= V_DIM
        vones = (half != odd[0]).astype(F32).reshape(1, d)

        lru_in, q, k, v = _pre(x, mod3, row(norm1_g[l]), pos3, invf, wi[:, :o0].astype(BF16),
                               wi[:, o0:o1].astype(BF16), wi[:, o1:o2].astype(BF16), wkr2,
                               row(q_norm_g[l]), wuq2, row(kv_norm_g[l]), wuk, wuv, vones)

        eye = jnp.eye(LRU_BLOCKS, dtype=F32)
        blockdiag = lambda w: jnp.einsum("hij,hg->higj", w, eye).reshape(lw, lw)
        wg = jnp.concatenate([blockdiag(lru_wa[l]), blockdiag(lru_wx[l])], axis=1).astype(BF16)
        bg = jnp.concatenate([lru_ba[l], lru_bx[l]])[None, :]
        yl = _lru(lru_in, conv_w[l], row(conv_b[l]), wg, bg, row(lru_lambda[l]), row(lru_out_g[l]))

        ym = _attn(q, k, v, row(mla_out_g[l]))

        wo = w_out[l].astype(BF16)
        x1, h2t = _post(yl, ym, x, mod3, wo[:lw], wo[lw:], row(norm2_g[l]))

        wqt = (peer_wq[l].reshape(d, PEER_HEADS, 2, PEER_HALF).transpose(2, 1, 3, 0)
               .reshape(2 * PEER_HEADS * PEER_HALF, d).astype(BF16))
        keys = peer_keys[l].astype(BF16)
        kbig = jnp.einsum("hkd,hg->khgd", keys[:, 0], jnp.eye(PEER_HEADS, dtype=BF16)).reshape(
            PEER_KEYS * PEER_HEADS, PEER_HEADS * PEER_HALF)
        cnt, w1, packed = _route(h2t, wqt, kbig, keys[:, 1])

        ne = peer_u.shape[1] // EB
        vt3 = peer_v[l].astype(BF16).reshape(ne, EB, d).transpose(0, 2, 1)
        out = _dense(h2t, peer_u[l].astype(BF16), vt3, cnt, w1, packed,
                     x1.reshape(bsz * seq, d), mod3, row(final_g), seq)
        x = out.reshape(bsz, seq, d)
    return x
```

```python
import functools

import jax
import jax.numpy as jnp
from jax import lax
from jax.experimental import pallas as pl
from jax.experimental.pallas import tpu as pltpu

F32 = jnp.float32
BF16 = jnp.bfloat16

LRU_BLOCKS = 8
CONV_WIDTH = 4
LRU_C = 8.0
MLA_HEADS = 8
QK_NOPE = 64
QK_ROPE = 32
V_DIM = 64
ROPE_THETA = 10000.0
PEER_HEADS = 8
PEER_KEYS = 128
PEER_HALF = 128
PEER_TOPK = 16
EPS = 1e-6
LOG2_E = 1.4426950408889634

LANES = 128
SUBLANES = 8
VMEM_LIMIT = 56 * 1024 * 1024

TM = 512
TS = 256
TQ = 256
TN_ROUTE = 512
TN = 512
EB = 1024
MXU_COLS = 256
DOT_ROWS = 128
GATE_PARTS = 1
DENSE_SECTIONS = 1


def _cparams(*sem, flags=None):
    return pltpu.CompilerParams(dimension_semantics=sem, vmem_limit_bytes=VMEM_LIMIT, flags=flags)


def _rms(x, g):
    return x * lax.rsqrt(jnp.mean(x * x, axis=-1, keepdims=True) + EPS) * g


def _dot(a, b):
    return jnp.dot(a, b, preferred_element_type=F32)


def _ada_kernel(c_ref, w_ref, b_ref, o_ref):
    ca = jax.nn.silu(c_ref[...])
    o_ref[...] = jnp.dot(ca, w_ref[...], preferred_element_type=F32,
                         precision=lax.Precision.HIGHEST) + b_ref[...]


def _ada(c, w, b):
    bsz, d = c.shape
    n = w.shape[1]
    return pl.pallas_call(
        _ada_kernel,
        grid=(n // d,),
        in_specs=[pl.BlockSpec((bsz, d), lambda j: (0, 0)),
                  pl.BlockSpec((d, d), lambda j: (0, j)),
                  pl.BlockSpec((1, d), lambda j: (0, j))],
        out_specs=pl.BlockSpec((bsz, d), lambda j: (0, j)),
        out_shape=jax.ShapeDtypeStruct((bsz, n), F32),
        compiler_params=_cparams("arbitrary"),
        name="ada",
    )(c, w, b.reshape(1, n))


def _pre_kernel(x_ref, mod_ref, g1_ref, pos_ref, invf_ref, wlru_ref, wq_ref, wkv_ref, wkr_ref,
                qg_ref, wuq_ref, kvg_ref, wuk_ref, wuv_ref, vones_ref, lru_ref, q_ref, k_ref, v_ref):
    d = x_ref.shape[-1]
    x = x_ref[0]
    shift = mod_ref[0, 0:1, :]
    scale = mod_ref[0, 1:2, :]
    hb = (_rms(x, g1_ref[...]) * (1.0 + scale) + shift).astype(BF16)
    lru_ref[0] = _dot(hb, wlru_ref[...])

    ang = pos_ref[0].astype(F32) * invf_ref[...]
    cos = jnp.cos(ang)
    sin = jnp.sin(ang)

    qn = _rms(_dot(hb, wq_ref[...]), qg_ref[...]).astype(BF16)
    q2 = _dot(qn, wuq_ref[...])
    q_scale = (QK_NOPE + QK_ROPE) ** -0.5 * LOG2_E
    for h in range(MLA_HEADS):
        lo = h * LANES
        q_ref[0, :, lo:lo + LANES] = ((q2[:, lo:lo + LANES] * cos
                                       + q2[:, d + lo:d + lo + LANES] * sin) * q_scale).astype(BF16)

    kvn = _rms(_dot(hb, wkv_ref[...]), kvg_ref[...]).astype(BF16)
    kn = _dot(kvn, wuk_ref[...])
    v_ref[0] = (_dot(kvn, wuv_ref[...]) + vones_ref[...]).astype(BF16)
    kr2 = _dot(hb, wkr_ref[...])
    krot = kr2[:, :LANES] * cos + kr2[:, LANES:] * sin
    for h in range(MLA_HEADS):
        lo = h * LANES
        k_ref[0, :, lo:lo + LANES] = (kn[:, lo:lo + LANES] + krot).astype(BF16)


def _pre(x, mod3, g1, pos3, invf, wlru, wq, wkv, wkr2, qg, wuq2, kvg, wuk, wuv, vones):
    bsz, s, d = x.shape
    full = lambda a: pl.BlockSpec(a.shape, lambda b, i: (0,) * a.ndim)
    tok = lambda w: pl.BlockSpec((1, TM, w), lambda b, i: (b, i, 0))
    return pl.pallas_call(
        _pre_kernel,
        grid=(bsz, s // TM),
        in_specs=[tok(d), pl.BlockSpec((1, 6, d), lambda b, i: (b, 0, 0)), full(g1), tok(1), full(invf),
                  full(wlru), full(wq), full(wkv), full(wkr2), full(qg), full(wuq2), full(kvg),
                  full(wuk), full(wuv), full(vones)],
        out_specs=[tok(d), tok(d), tok(d), tok(d)],
        out_shape=[jax.ShapeDtypeStruct((bsz, s, d), F32),
                   jax.ShapeDtypeStruct((bsz, s, d), BF16),
                   jax.ShapeDtypeStruct((bsz, s, d), BF16),
                   jax.ShapeDtypeStruct((bsz, s, d), BF16)],
        compiler_params=_cparams("arbitrary", "arbitrary"),
        name="pre",
    )(x, mod3, g1, pos3, invf, wlru, wq, wkv, wkr2, qg, wuq2, kvg, wuk, wuv, vones)


def _lru_kernel(x_ref, cw_ref, cb_ref, wg_ref, bg_ref, lam_ref, og_ref, o_ref, prev_ref, h_ref):
    w = o_ref.shape[-1]

    @pl.when(pl.program_id(1) == 0)
    def _():
        prev_ref[...] = jnp.zeros_like(prev_ref)
        h_ref[...] = jnp.zeros_like(h_ref)

    x = x_ref[0, :, :w]
    gate = x_ref[0, :, w:]
    tail = prev_ref[...]
    row = lax.broadcasted_iota(jnp.int32, x.shape, 0)
    row8 = lax.broadcasted_iota(jnp.int32, tail.shape, 0)

    xc = cb_ref[...]
    for k in range(CONV_WIDTH - 1):
        back = CONV_WIDTH - 1 - k
        shifted = pltpu.roll(x, back, 0)
        first = jnp.where(row8 < back, pltpu.roll(tail, back, 0), shifted[:SUBLANES, :])
        shifted = jnp.concatenate([first, shifted[SUBLANES:, :]], axis=0)
        xc = xc + cw_ref[k:k + 1, :] * shifted
    xc = xc + cw_ref[CONV_WIDTH - 1:CONV_WIDTH, :] * x
    prev_ref[...] = x[TS - SUBLANES:, :]

    pre = _dot(xc.astype(BF16), wg_ref[...]) + bg_ref[...]
    r = jax.nn.sigmoid(pre[:, :w])
    i = jax.nn.sigmoid(pre[:, w:])
    log_a = -LRU_C * r * jax.nn.softplus(-lam_ref[...])
    a = jnp.exp(log_a)
    b = jnp.sqrt(1.0 - a * a) * (i * xc)

    sub = row % SUBLANES
    dist = 1
    while dist < SUBLANES:
        keep = sub >= dist
        b = jnp.where(keep, a * pltpu.roll(b, dist, 0) + b, b)
        a = jnp.where(keep, a * pltpu.roll(a, dist, 0), a)
        dist *= 2
    state = h_ref[0:1, :]
    groups = []
    for g0 in range(0, TS, SUBLANES):
        hg = b[g0:g0 + SUBLANES, :] + a[g0:g0 + SUBLANES, :] * state
        groups.append(hg)
        state = hg[SUBLANES - 1:SUBLANES, :]
    h = jnp.concatenate(groups, axis=0)
    h_ref[...] = jnp.broadcast_to(state, h_ref.shape)

    y = jax.nn.gelu(gate) * h
    o_ref[0] = _rms(y, og_ref[...]).astype(BF16)


def _lru(lru_in, cw, cb, wg, bg, lam, og):
    bsz, s, d = lru_in.shape
    w = d // 2
    full = lambda a: pl.BlockSpec(a.shape, lambda b, t: (0,) * a.ndim)
    return pl.pallas_call(
        _lru_kernel,
        grid=(bsz, s // TS),
        in_specs=[pl.BlockSpec((1, TS, d), lambda b, t: (b, t, 0)),
                  full(cw), full(cb), full(wg), full(bg), full(lam), full(og)],
        out_specs=pl.BlockSpec((1, TS, w), lambda b, t: (b, t, 0)),
        out_shape=jax.ShapeDtypeStruct((bsz, s, w), BF16),
        scratch_shapes=[pltpu.VMEM((SUBLANES, w), F32), pltpu.VMEM((SUBLANES, w), F32)],
        compiler_params=_cparams("arbitrary", "arbitrary"),
        name="lru",
    )(lru_in, cw, cb, wg, bg, lam, og)


def _attn_kernel(q_ref, k_ref, v_ref, g_ref, o_ref, m_ref, acc_ref, bias_ref):
    i = pl.program_id(1)
    nt = (((1,), (1,)), ((), ()))
    row = lax.broadcasted_iota(jnp.int32, (TQ, TQ), 0)
    col = lax.broadcasted_iota(jnp.int32, (TQ, TQ), 1)
    lane = lax.broadcasted_iota(jnp.int32, (TQ, LANES), 1)

    m_ref[...] = jnp.full(m_ref.shape, -jnp.inf, F32)
    acc_ref[...] = jnp.zeros(acc_ref.shape, F32)
    bias_ref[...] = jnp.where(col <= row, 0.0, -jnp.inf)

    def block(j, masked):
        start = pl.multiple_of(j * TQ, TQ)

        def scores(h):
            lo = h * LANES
            s = lax.dot_general(q_ref[0, :, lo:lo + LANES], k_ref[0, pl.ds(start, TQ), lo:lo + LANES], nt,
                                preferred_element_type=F32)
            return s + bias_ref[...] if masked else s

        def softmax_step(h, s):
            m_old = m_ref[h]
            m_new = jnp.maximum(m_old, jnp.max(s, axis=-1, keepdims=True))
            m_ref[h] = m_new
            alpha = jnp.exp2(m_old - m_new)
            p = [jnp.exp2(s[:, c:c + LANES] - m_new) for c in range(0, TQ, LANES)]
            return alpha, jnp.concatenate(p, axis=-1).astype(BF16)

        def weighted_values(h, alpha, p):
            lo = h * LANES
            acc_ref[h] = alpha * acc_ref[h] + _dot(p, v_ref[0, pl.ds(start, TQ), lo:lo + LANES])

        pending_s, pending_p = {}, {}
        for t in range(MLA_HEADS + 2):
            if t < MLA_HEADS:
                pending_s[t] = scores(t)
            if 1 <= t <= MLA_HEADS:
                pending_p[t - 1] = softmax_step(t - 1, pending_s.pop(t - 1))
            if t >= 2:
                weighted_values(t - 2, *pending_p.pop(t - 2))

    def unmasked(j, carry):
        block(j, False)
        return carry

    def diagonal(j, carry):
        block(j, True)
        return carry

    lax.fori_loop(0, i, unmasked, 0)
    lax.fori_loop(i, i + 1, diagonal, 0)

    pairs = []
    for hp in range(MLA_HEADS // 2):
        even, odd = acc_ref[2 * hp], acc_ref[2 * hp + 1]
        num = jnp.where(lane < V_DIM, even, odd)
        den = jnp.where(lane < V_DIM, pltpu.roll(even, V_DIM, 1), pltpu.roll(odd, V_DIM, 1))
        pairs.append(num / den)
    y = jnp.concatenate(pairs, axis=-1)
    o_ref[0] = _rms(y, g_ref[...]).astype(BF16)


def _attn(q, k, v, g):
    bsz, s, d = q.shape
    w = g.shape[-1]
    return pl.pallas_call(
        _attn_kernel,
        grid=(bsz, s // TQ),
        in_specs=[pl.BlockSpec((1, TQ, d), lambda b, i: (b, i, 0)),
                  pl.BlockSpec((1, s, d), lambda b, i: (b, 0, 0)),
                  pl.BlockSpec((1, s, d), lambda b, i: (b, 0, 0)),
                  pl.BlockSpec(g.shape, lambda b, i: (0, 0))],
        out_specs=pl.BlockSpec((1, TQ, w), lambda b, i: (b, i, 0)),
        out_shape=jax.ShapeDtypeStruct((bsz, s, w), BF16),
        scratch_shapes=[pltpu.VMEM((MLA_HEADS, TQ, LANES), F32), pltpu.VMEM((MLA_HEADS, TQ, LANES), F32),
                        pltpu.VMEM((TQ, TQ), F32)],
        compiler_params=_cparams("arbitrary", "arbitrary"),
        name="attn",
    )(q, k, v, g)


def _post_kernel(yl_ref, ym_ref, x_ref, mod_ref, wol_ref, wom_ref, g2_ref, x1_ref, h2t_ref):
    mix = _dot(yl_ref[0], wol_ref[...]) + _dot(ym_ref[0], wom_ref[...])
    x1 = x_ref[0] + mod_ref[0, 2:3, :] * mix
    x1_ref[0] = x1
    h2 = _rms(x1, g2_ref[...]) * (1.0 + mod_ref[0, 4:5, :]) + mod_ref[0, 3:4, :]
    h2t_ref[...] = h2.T.astype(BF16)


def _post(yl, ym, x, mod3, wol, wom, g2):
    bsz, s, d = x.shape
    w = yl.shape[-1]
    nt = s // TM
    full = lambda a: pl.BlockSpec(a.shape, lambda b, i: (0,) * a.ndim)
    tok = lambda c: pl.BlockSpec((1, TM, c), lambda b, i: (b, i, 0))
    return pl.pallas_call(
        _post_kernel,
        grid=(bsz, nt),
        in_specs=[tok(w), tok(w), tok(d), pl.BlockSpec((1, 6, d), lambda b, i: (b, 0, 0)),
                  full(wol), full(wom), full(g2)],
        out_specs=[tok(d), pl.BlockSpec((d, TM), lambda b, i: (0, b * nt + i))],
        out_shape=[jax.ShapeDtypeStruct((bsz, s, d), F32),
                   jax.ShapeDtypeStruct((d, bsz * s), BF16)],
        compiler_params=_cparams("arbitrary", "arbitrary"),
        name="post",
    )(yl, ym, x, mod3, wol, wom, g2)


def _sort_pairs(n):
    pairs = []

    def merge(lo, hi, r):
        step = r * 2
        if step < hi - lo:
            merge(lo, hi, step)
            merge(lo + r, hi, step)
            pairs.extend((i, i + r) for i in range(lo + r, hi - r, step))
        else:
            pairs.append((lo, lo + r))

    def sort(lo, hi):
        if hi - lo >= 1:
            mid = lo + (hi - lo) // 2
            sort(lo, mid)
            sort(mid + 1, hi)
            merge(lo, hi, 1)

    sort(0, n - 1)
    return pairs


def _sort_desc(xs):
    xs = list(xs)
    for i, j in _sort_pairs(len(xs)):
        a, b = xs[i], xs[j]
        xs[i], xs[j] = jnp.maximum(a, b), jnp.minimum(a, b)
    return xs


def _merge_top(a, b):
    n = len(a)
    xs = [jnp.maximum(a[i], b[n - 1 - i]) for i in range(n)]
    dist = n // 2
    while dist >= 1:
        for i in range(n):
            if i & dist == 0:
                p, q = xs[i], xs[i + dist]
                xs[i], xs[i + dist] = jnp.maximum(p, q), jnp.minimum(p, q)
        dist //= 2
    return xs


def _kth_largest(vals, k):
    n = 1
    while n < len(vals):
        n *= 2
    present = [True] * len(vals) + [False] * (n - len(vals))
    ops = []
    for i, j in _sort_pairs(n):
        if present[i] and present[j]:
            ops.append(("cmp", i, j))
        elif present[j]:
            ops.append(("mov", i, j))
            present[i], present[j] = True, False
    need = {k}
    live = []
    for op in reversed(ops):
        kind, i, j = op
        if kind == "mov":
            if i in need:
                need.discard(i)
                need.add(j)
                live.append((kind, i, j, True, False))
        else:
            hi, lo = i in need, j in need
            if hi or lo:
                need.update((i, j))
                live.append((kind, i, j, hi, lo))
    xs = list(vals) + [None] * (n - len(vals))
    for kind, i, j, hi, lo in reversed(live):
        a, b = xs[i], xs[j]
        if kind == "mov":
            xs[i], xs[j] = b, None
        else:
            xs[i] = jnp.maximum(a, b) if hi else None
            xs[j] = jnp.minimum(a, b) if lo else None
    return xs[k]


def _top_values(slabs, n):
    groups = [_sort_desc(slabs[g:g + n]) for g in range(0, len(slabs), n)]
    while len(groups) > 1:
        groups = [_merge_top(groups[g], groups[g + 1]) for g in range(0, len(groups), 2)]
    return groups[0]


PACK_ROWS = 2 * SUBLANES


def _pair_bits(x):
    bits = pltpu.bitcast(x.astype(BF16).astype(F32), jnp.uint32)
    return (bits & jnp.uint32(0xFFFF0000)) | (bits >> 16)


def _route_kernel(h2t_ref, wqt_ref, kbig_ref, keys2_ref, c_ref, w1_ref, p_ref):
    tn = h2t_ref.shape[-1]
    nk = PEER_KEYS
    half_rows = PEER_HEADS * PEER_HALF
    qt = _dot(wqt_ref[...], h2t_ref[...])
    sub = lax.broadcasted_iota(jnp.int32, (SUBLANES, tn), 0)

    sc1 = _dot(kbig_ref[...], qt[:half_rows, :].astype(BF16))
    slabs = [sc1[PEER_HEADS * k:PEER_HEADS * (k + 1), :] for k in range(nk)]
    a = _top_values(slabs, PEER_TOPK)

    b = [None] * PEER_TOPK
    for h in range(PEER_HEADS):
        r0 = half_rows + h * PEER_HALF
        sc = _dot(keys2_ref[h], qt[r0:r0 + PEER_HALF, :].astype(BF16))
        tiles = [sc[SUBLANES * t:SUBLANES * (t + 1), :] for t in range(nk // SUBLANES)]
        xs = _sort_desc(tiles)
        shift = SUBLANES // 2
        while shift >= 1:
            xs = _merge_top(xs, [pltpu.roll(v, shift, 0) for v in xs])
            shift //= 2
        for ib in range(nk // PACK_ROWS):
            ranks, weights = [], []
            for tile in tiles[2 * ib:2 * ib + 2]:
                rk = jnp.full_like(tile, float(PEER_TOPK))
                for r in reversed(range(PEER_TOPK)):
                    rk = jnp.where(tile >= xs[r], float(r), rk)
                ranks.append(rk)
                weights.append(jnp.exp(tile - xs[0]))
            rk16 = jnp.concatenate(ranks, axis=0).astype(BF16)
            w16 = jnp.concatenate(weights, axis=0).astype(BF16)
            for lb in range(tn // LANES):
                r0 = 2 * h * PACK_ROWS
                p_ref[lb, ib, r0:r0 + PACK_ROWS, :] = rk16[:, lb * LANES:(lb + 1) * LANES]
                p_ref[lb, ib, r0 + PACK_ROWS:r0 + 2 * PACK_ROWS, :] = w16[:, lb * LANES:(lb + 1) * LANES]
        for r in range(PEER_TOPK):
            b[r] = xs[r] if h == 0 else jnp.where(sub == h, xs[r], b[r])

    pairs = [(i, j) for i in range(PEER_TOPK) for j in range(PEER_TOPK) if (i + 1) * (j + 1) <= PEER_TOPK]
    cands = {ij: a[ij[0]] + b[ij[1]] for ij in pairs}
    tau = _kth_largest(list(cands.values()), PEER_TOPK - 1)
    top = a[0] + b[0]
    z = jnp.zeros_like(tau)
    for c in cands.values():
        z = z + jnp.where(c >= tau, jnp.exp(c - top), 0.0)
    inv_z = 1.0 / z
    need = []
    for j in range(PEER_TOPK):
        t = None
        for i in range(PEER_TOPK):
            if (i, j) in cands:
                v = jnp.where(cands[(i, j)] >= tau, a[i], jnp.inf)
                t = v if t is None else jnp.minimum(t, v)
        need.append(t)
    for k in range(nk):
        cnt = jnp.zeros_like(tau)
        for j in range(PEER_TOPK):
            cnt = jnp.where(slabs[k] >= need[j], float(j + 1), cnt)
        c_ref[k] = _pair_bits(cnt)
        w1_ref[k] = _pair_bits(jnp.exp(slabs[k] - a[0]) * inv_z)


def _route(h2t, wqt, kbig, keys2):
    d, t = h2t.shape
    tn = TN_ROUTE
    by_key = pl.BlockSpec((PEER_KEYS, PEER_HEADS, tn), lambda i: (0, 0, i))
    packed = (PEER_KEYS // PACK_ROWS, 2 * PEER_HEADS * PACK_ROWS, LANES)
    return pl.pallas_call(
        _route_kernel,
        grid=(t // tn,),
        in_specs=[pl.BlockSpec((d, tn), lambda i: (0, i)),
                  pl.BlockSpec(wqt.shape, lambda i: (0, 0)),
                  pl.BlockSpec(kbig.shape, lambda i: (0, 0)),
                  pl.BlockSpec(keys2.shape, lambda i: (0, 0, 0))],
        out_specs=[by_key, by_key, pl.BlockSpec((tn // LANES,) + packed, lambda i: (i, 0, 0, 0))],
        out_shape=[jax.ShapeDtypeStruct((PEER_KEYS, PEER_HEADS, t), jnp.uint32),
                   jax.ShapeDtypeStruct((PEER_KEYS, PEER_HEADS, t), jnp.uint32),
                   jax.ShapeDtypeStruct((t // LANES,) + packed, BF16)],
        compiler_params=_cparams("arbitrary"),
        name="route",
    )(h2t, wqt, kbig, keys2)


def _dense_kernel(flag_ref, h2t_ref, u_ref, vt_ref, c_ref, w1_ref, p_ref, x1_ref, mod_ref,
                  fg_ref, o_ref, acc_ref, act_ref, gat_ref, *, nblk):
    s = pl.program_id(0)
    n_items = pl.num_programs(0) - 2
    tn = h2t_ref.shape[-1]
    eb = u_ref.shape[0]
    rows_per_step = eb // PEER_KEYS

    @pl.when(s == 0)
    def _():
        act_ref[1] = jnp.zeros(act_ref.shape[1:], F32)
        gat_ref[0] = jnp.zeros(gat_ref.shape[1:], BF16)

    jb = jnp.clip(s - 1, 0, n_items - 1) % nblk
    jc = jnp.clip(s - 2, 0, n_items - 1) % nblk

    @pl.when(jc == 0)
    def _():
        acc_ref[...] = jnp.zeros_like(acc_ref)

    n_lb = tn // LANES

    def row_tile(ref, i1, h, ls):
        return pltpu.bitcast(jnp.broadcast_to(ref[i1, h:h + 1, ls], (SUBLANES, LANES)), BF16)

    def gate_group(il, lb, part, oth):
        i1 = jb * rows_per_step + il
        base = il * PEER_KEYS
        ls = slice(lb * LANES, (lb + 1) * LANES)
        cntb = [row_tile(c_ref, i1, h, ls) for h in range(PEER_HEADS)]
        w1b = [row_tile(w1_ref, i1, h, ls) for h in range(PEER_HEADS)]
        tiles = PEER_KEYS // PACK_ROWS // GATE_PARTS
        for ib in range(part * tiles, (part + 1) * tiles):
            g = None
            for h in range(PEER_HEADS):
                r0 = 2 * h * PACK_ROWS
                rank2 = p_ref[lb, ib, r0:r0 + PACK_ROWS, :]
                w2 = p_ref[lb, ib, r0 + PACK_ROWS:r0 + 2 * PACK_ROWS, :]
                term = jnp.minimum(jnp.maximum(cntb[h] - rank2, 0.0), w2) * w1b[h]
                g = term if g is None else g + term
            r1 = base + ib * PACK_ROWS
            act = act_ref[oth, r1:r1 + PACK_ROWS, ls]
            gat_ref[oth, r1:r1 + PACK_ROWS, ls] = jax.nn.gelu(act.astype(BF16)) * g

    def matmul_chunks(cur):
        chunks = []
        for c0 in range(0, tn, MXU_COLS):
            cs = slice(c0, c0 + MXU_COLS)
            for r0 in range(0, acc_ref.shape[0], DOT_ROWS):
                rs = slice(r0, r0 + DOT_ROWS)

                def second(rs=rs, cs=cs):
                    acc_ref[rs, cs] += _dot(vt_ref[0, rs, :], gat_ref[cur, :, cs])
                chunks.append(second)
        for c0 in range(0, tn, MXU_COLS):
            cs = slice(c0, c0 + MXU_COLS)
            for r0 in range(0, eb, DOT_ROWS):
                rs = slice(r0, r0 + DOT_ROWS)

                def first(rs=rs, cs=cs):
                    act_ref[cur, rs, cs] = _dot(u_ref[rs, :], h2t_ref[:, cs])
                chunks.append(first)
        return chunks

    def sections(cur, oth):
        groups = [(il, lb, part) for il in range(rows_per_step) for lb in range(n_lb)
                  for part in range(GATE_PARTS)]
        chunks = matmul_chunks(cur)
        per_section = len(groups) // DENSE_SECTIONS
        for sec in range(DENSE_SECTIONS):
            @pl.when(flag_ref[0] == 0)
            def _(sec=sec):
                for gi in range(sec * per_section, (sec + 1) * per_section):
                    for ci in range(gi * len(chunks) // len(groups), (gi + 1) * len(chunks) // len(groups)):
                        chunks[ci]()
                    gate_group(*groups[gi], oth)

    for parity in range(2):
        @pl.when(s % 2 == parity)
        def _(parity=parity):
            sections(parity, 1 - parity)

    @pl.when(jnp.logical_and(jc == nblk - 1, s >= 2))
    def _():
        x2 = x1_ref[...] + mod_ref[0, 5:6, :] * acc_ref[...].T
        o_ref[...] = _rms(x2, fg_ref[...])


def _dense(h2t, u, vt3, cnt, w1, packed, x1, mod3, fg, seq):
    d, t = h2t.shape
    ne = vt3.shape[0]
    per_seq = seq // TN
    n_items = (t // TN) * ne
    item = lambda s, lag: jnp.clip(s - lag, 0, n_items - 1)
    tile = lambda s, lag: item(s, lag) // ne
    block = lambda s, lag: item(s, lag) % ne
    by_key = pl.BlockSpec((PEER_KEYS, PEER_HEADS, TN), lambda s: (0, 0, tile(s, 1)))
    return pl.pallas_call(
        functools.partial(_dense_kernel, nblk=ne),
        grid=(n_items + 2,),
        in_specs=[pl.BlockSpec(memory_space=pltpu.SMEM),
                  pl.BlockSpec((d, TN), lambda s: (0, tile(s, 0))),
                  pl.BlockSpec((EB, d), lambda s: (block(s, 0), 0)),
                  pl.BlockSpec((1, d, EB), lambda s: (block(s, 2), 0, 0)),
                  by_key, by_key,
                  pl.BlockSpec((TN // LANES,) + packed.shape[1:], lambda s: (tile(s, 1), 0, 0, 0)),
                  pl.BlockSpec((TN, d), lambda s: (tile(s, 2), 0)),
                  pl.BlockSpec((1, 6, d), lambda s: (tile(s, 2) // per_seq, 0, 0)),
                  pl.BlockSpec(fg.shape, lambda s: (0, 0))],
        out_specs=pl.BlockSpec((TN, d), lambda s: (tile(s, 2), 0)),
        out_shape=jax.ShapeDtypeStruct((t, d), F32),
        scratch_shapes=[pltpu.VMEM((d, TN), F32), pltpu.VMEM((2, EB, TN), F32),
                        pltpu.VMEM((2, EB, TN), BF16)],
        compiler_params=_cparams("arbitrary"),
        name="dense",
    )(jnp.zeros((1,), jnp.int32), h2t, u, vt3, cnt, w1, packed, x1, mod3, fg)


def _rot_half_cols(w):
    half = w.shape[-1] // 2
    return jnp.concatenate([-w[..., half:], w[..., :half]], axis=-1)


def _pad_cols(w, before, total):
    return jnp.pad(w, [(0, 0)] * (w.ndim - 1) + [(before, total - before - w.shape[-1])])


def kernel(x, c, positions, w_ada, b_ada, norm1_g, w_in, conv_w, conv_b, lru_wa, lru_ba, lru_wx, lru_bx,
           lru_lambda, q_norm_g, w_uq, kv_norm_g, w_ukv, lru_out_g, mla_out_g, w_out, norm2_g, peer_wq,
           peer_keys, peer_u, peer_v, final_g):
    bsz, seq, d = x.shape
    depth = w_ada.shape[0]
    lw = d // 2
    qr = w_uq.shape[1]
    kvr = w_ukv.shape[1]
    assert depth == 1, "the final norm is fused into the (single) layer's PEER kernel"
    assert seq % TM == 0 and seq % TN == 0 and seq % TQ == 0 and seq % TS == 0
    assert d == MLA_HEADS * LANES and lw == MLA_HEADS * V_DIM

    inv_freq = 1.0 / (ROPE_THETA ** (jnp.arange(0, QK_ROPE, 2, dtype=F32) / QK_ROPE))
    invf = _pad_cols(jnp.concatenate([inv_freq, inv_freq])[None, :], QK_NOPE, LANES)
    pos3 = positions.reshape(bsz, seq, 1)
    row = lambda v: v.reshape(1, -1)

    for l in range(depth):
        mod3 = _ada(c, w_ada[l], b_ada[l]).reshape(bsz, 6, d)

        wi = w_in[l]
        o0, o1, o2, o3 = 2 * lw, 2 * lw + qr, 2 * lw + qr + kvr, 2 * lw + qr + kvr + QK_ROPE
        wkr = wi[:, o2:o3]
        wkr2 = jnp.concatenate([_pad_cols(wkr, QK_NOPE, LANES),
                                _pad_cols(_rot_half_cols(wkr), QK_NOPE, LANES)], axis=1).astype(BF16)
        uq = w_uq[l].reshape(qr, MLA_HEADS, QK_NOPE + QK_ROPE)
        uq_main = _pad_cols(uq, 0, LANES).reshape(qr, d)
        uq_rot = _pad_cols(_rot_half_cols(uq[..., QK_NOPE:]), QK_NOPE, LANES).reshape(qr, d)
        wuq2 = jnp.concatenate([uq_main, uq_rot], axis=1).astype(BF16)
        ukv = w_ukv[l].reshape(kvr, MLA_HEADS, QK_NOPE + V_DIM)
        wuk = _pad_cols(ukv[..., :QK_NOPE], 0, LANES).reshape(kvr, d).astype(BF16)
        uv = ukv[..., QK_NOPE:]
        odd = (jnp.arange(MLA_HEADS) % 2 == 1)[None, :, None]
        wuv = jnp.where(odd, _pad_cols(uv, V_DIM, LANES), _pad_cols(uv, 0, LANES)).reshape(kvr, d).astype(BF16)
        half = jnp.arange(LANES)[None, :] >= V_DIM
        vones = (half != odd[0]).astype(F32).reshape(1, d)

        lru_in, q, k, v = _pre(x, mod3, row(norm1_g[l]), pos3, invf, wi[:, :o0].astype(BF16),
                               wi[:, o0:o1].astype(BF16), wi[:, o1:o2].astype(BF16), wkr2,
                               row(q_norm_g[l]), wuq2, row(kv_norm_g[l]), wuk, wuv, vones)

        eye = jnp.eye(LRU_BLOCKS, dtype=F32)
        blockdiag = lambda w: jnp.einsum("hij,hg->higj", w, eye).reshape(lw, lw)
        wg = jnp.concatenate([blockdiag(lru_wa[l]), blockdiag(lru_wx[l])], axis=1).astype(BF16)
        bg = jnp.concatenate([lru_ba[l], lru_bx[l]])[None, :]
        yl = _lru(lru_in, conv_w[l], row(conv_b[l]), wg, bg, row(lru_lambda[l]), row(lru_out_g[l]))

        ym = _attn(q, k, v, row(mla_out_g[l]))

        wo = w_out[l].astype(BF16)
        x1, h2t = _post(yl, ym, x, mod3, wo[:lw], wo[lw:], row(norm2_g[l]))

        wqt = (peer_wq[l].reshape(d, PEER_HEADS, 2, PEER_HALF).transpose(2, 1, 3, 0)
               .reshape(2 * PEER_HEADS * PEER_HALF, d).astype(BF16))
        keys = peer_keys[l].astype(BF16)
        kbig = jnp.einsum("hkd,hg->khgd", keys[:, 0], jnp.eye(PEER_HEADS, dtype=BF16)).reshape(
            PEER_KEYS * PEER_HEADS, PEER_HEADS * PEER_HALF)
        cnt, w1, packed = _route(h2t, wqt, kbig, keys[:, 1])

        ne = peer_u.shape[1] // EB
        vt3 = peer_v[l].astype(BF16).reshape(ne, EB, d).transpose(0, 2, 1)
        out = _dense(h2t, peer_u[l].astype(BF16), vt3, cnt, w1, packed,
                     x1.reshape(bsz * seq, d), mod3, row(final_g), seq)
        x = out.reshape(bsz, seq, d)
    return x
```

```python
import functools

import jax
import jax.numpy as jnp
from jax import lax
from jax.experimental import pallas as pl
from jax.experimental.pallas import tpu as pltpu

F32 = jnp.float32
BF16 = jnp.bfloat16

LRU_BLOCKS = 8
CONV_WIDTH = 4
LRU_C = 8.0
MLA_HEADS = 8
QK_NOPE = 64
QK_ROPE = 32
V_DIM = 64
ROPE_THETA = 10000.0
PEER_HEADS = 8
PEER_KEYS = 128
PEER_HALF = 128
PEER_TOPK = 16
EPS = 1e-6
LOG2_E = 1.4426950408889634

LANES = 128
SUBLANES = 8
VMEM_LIMIT = 56 * 1024 * 1024

TM = 512
TS = 256
TQ = 512
TN_ROUTE = 512
TN = 512
EB = 1024
MXU_COLS = 256
DOT_ROWS = 128
GATE_PARTS = 1
DENSE_SECTIONS = 1


def _cparams(*sem, flags=None):
    return pltpu.CompilerParams(dimension_semantics=sem, vmem_limit_bytes=VMEM_LIMIT, flags=flags)


def _rms(x, g):
    return x * lax.rsqrt(jnp.mean(x * x, axis=-1, keepdims=True) + EPS) * g


def _dot(a, b):
    return jnp.dot(a, b, preferred_element_type=F32)


def _ada_kernel(c_ref, w_ref, b_ref, o_ref):
    ca = jax.nn.silu(c_ref[...])
    o_ref[...] = jnp.dot(ca, w_ref[...], preferred_element_type=F32,
                         precision=lax.Precision.HIGHEST) + b_ref[...]


def _ada(c, w, b):
    bsz, d = c.shape
    n = w.shape[1]
    return pl.pallas_call(
        _ada_kernel,
        grid=(n // d,),
        in_specs=[pl.BlockSpec((bsz, d), lambda j: (0, 0)),
                  pl.BlockSpec((d, d), lambda j: (0, j)),
                  pl.BlockSpec((1, d), lambda j: (0, j))],
        out_specs=pl.BlockSpec((bsz, d), lambda j: (0, j)),
        out_shape=jax.ShapeDtypeStruct((bsz, n), F32),
        compiler_params=_cparams("arbitrary"),
        name="ada",
    )(c, w, b.reshape(1, n))


def _pre_kernel(x_ref, mod_ref, g1_ref, pos_ref, invf_ref, wlru_ref, wq_ref, wkv_ref, wkr_ref,
                qg_ref, wuq_ref, kvg_ref, wuk_ref, wuv_ref, vones_ref, lru_ref, q_ref, k_ref, v_ref):
    d = x_ref.shape[-1]
    x = x_ref[0]
    shift = mod_ref[0, 0:1, :]
    scale = mod_ref[0, 1:2, :]
    hb = (_rms(x, g1_ref[...]) * (1.0 + scale) + shift).astype(BF16)
    lru_ref[0] = _dot(hb, wlru_ref[...])

    ang = pos_ref[0].astype(F32) * invf_ref[...]
    cos = jnp.cos(ang)
    sin = jnp.sin(ang)

    qn = _rms(_dot(hb, wq_ref[...]), qg_ref[...]).astype(BF16)
    q2 = _dot(qn, wuq_ref[...])
    q_scale = (QK_NOPE + QK_ROPE) ** -0.5 * LOG2_E
    for h in range(MLA_HEADS):
        lo = h * LANES
        q_ref[0, :, lo:lo + LANES] = ((q2[:, lo:lo + LANES] * cos
                                       + q2[:, d + lo:d + lo + LANES] * sin) * q_scale).astype(BF16)

    kvn = _rms(_dot(hb, wkv_ref[...]), kvg_ref[...]).astype(BF16)
    kn = _dot(kvn, wuk_ref[...])
    v_ref[0] = (_dot(kvn, wuv_ref[...]) + vones_ref[...]).astype(BF16)
    kr2 = _dot(hb, wkr_ref[...])
    krot = kr2[:, :LANES] * cos + kr2[:, LANES:] * sin
    for h in range(MLA_HEADS):
        lo = h * LANES
        k_ref[0, :, lo:lo + LANES] = (kn[:, lo:lo + LANES] + krot).astype(BF16)


def _pre(x, mod3, g1, pos3, invf, wlru, wq, wkv, wkr2, qg, wuq2, kvg, wuk, wuv, vones):
    bsz, s, d = x.shape
    full = lambda a: pl.BlockSpec(a.shape, lambda b, i: (0,) * a.ndim)
    tok = lambda w: pl.BlockSpec((1, TM, w), lambda b, i: (b, i, 0))
    return pl.pallas_call(
        _pre_kernel,
        grid=(bsz, s // TM),
        in_specs=[tok(d), pl.BlockSpec((1, 6, d), lambda b, i: (b, 0, 0)), full(g1), tok(1), full(invf),
                  full(wlru), full(wq), full(wkv), full(wkr2), full(qg), full(wuq2), full(kvg),
                  full(wuk), full(wuv), full(vones)],
        out_specs=[tok(d), tok(d), tok(d), tok(d)],
        out_shape=[jax.ShapeDtypeStruct((bsz, s, d), F32),
                   jax.ShapeDtypeStruct((bsz, s, d), BF16),
                   jax.ShapeDtypeStruct((bsz, s, d), BF16),
                   jax.ShapeDtypeStruct((bsz, s, d), BF16)],
        compiler_params=_cparams("arbitrary", "arbitrary"),
        name="pre",
    )(x, mod3, g1, pos3, invf, wlru, wq, wkv, wkr2, qg, wuq2, kvg, wuk, wuv, vones)


def _lru_kernel(x_ref, cw_ref, cb_ref, wg_ref, bg_ref, lam_ref, og_ref, o_ref, prev_ref, h_ref):
    w = o_ref.shape[-1]

    @pl.when(pl.program_id(1) == 0)
    def _():
        prev_ref[...] = jnp.zeros_like(prev_ref)
        h_ref[...] = jnp.zeros_like(h_ref)

    x = x_ref[0, :, :w]
    gate = x_ref[0, :, w:]
    tail = prev_ref[...]
    row = lax.broadcasted_iota(jnp.int32, x.shape, 0)
    row8 = lax.broadcasted_iota(jnp.int32, tail.shape, 0)

    xc = cb_ref[...]
    for k in range(CONV_WIDTH - 1):
        back = CONV_WIDTH - 1 - k
        shifted = pltpu.roll(x, back, 0)
        first = jnp.where(row8 < back, pltpu.roll(tail, back, 0), shifted[:SUBLANES, :])
        shifted = jnp.concatenate([first, shifted[SUBLANES:, :]], axis=0)
        xc = xc + cw_ref[k:k + 1, :] * shifted
    xc = xc + cw_ref[CONV_WIDTH - 1:CONV_WIDTH, :] * x
    prev_ref[...] = x[TS - SUBLANES:, :]

    pre = _dot(xc.astype(BF16), wg_ref[...]) + bg_ref[...]
    r = jax.nn.sigmoid(pre[:, :w])
    i = jax.nn.sigmoid(pre[:, w:])
    log_a = -LRU_C * r * jax.nn.softplus(-lam_ref[...])
    a = jnp.exp(log_a)
    b = jnp.sqrt(1.0 - a * a) * (i * xc)

    sub = row % SUBLANES
    dist = 1
    while dist < SUBLANES:
        keep = sub >= dist
        b = jnp.where(keep, a * pltpu.roll(b, dist, 0) + b, b)
        a = jnp.where(keep, a * pltpu.roll(a, dist, 0), a)
        dist *= 2
    state = h_ref[0:1, :]
    groups = []
    for g0 in range(0, TS, SUBLANES):
        hg = b[g0:g0 + SUBLANES, :] + a[g0:g0 + SUBLANES, :] * state
        groups.append(hg)
        state = hg[SUBLANES - 1:SUBLANES, :]
    h = jnp.concatenate(groups, axis=0)
    h_ref[...] = jnp.broadcast_to(state, h_ref.shape)

    y = jax.nn.gelu(gate) * h
    o_ref[0] = _rms(y, og_ref[...]).astype(BF16)


def _lru(lru_in, cw, cb, wg, bg, lam, og):
    bsz, s, d = lru_in.shape
    w = d // 2
    full = lambda a: pl.BlockSpec(a.shape, lambda b, t: (0,) * a.ndim)
    return pl.pallas_call(
        _lru_kernel,
        grid=(bsz, s // TS),
        in_specs=[pl.BlockSpec((1, TS, d), lambda b, t: (b, t, 0)),
                  full(cw), full(cb), full(wg), full(bg), full(lam), full(og)],
        out_specs=pl.BlockSpec((1, TS, w), lambda b, t: (b, t, 0)),
        out_shape=jax.ShapeDtypeStruct((bsz, s, w), BF16),
        scratch_shapes=[pltpu.VMEM((SUBLANES, w), F32), pltpu.VMEM((SUBLANES, w), F32)],
        compiler_params=_cparams("arbitrary", "arbitrary"),
        name="lru",
    )(lru_in, cw, cb, wg, bg, lam, og)


def _attn_kernel(q_ref, k_ref, v_ref, g_ref, o_ref, m_ref, acc_ref, bias_ref):
    i = pl.program_id(1)
    nt = (((1,), (1,)), ((), ()))
    row = lax.broadcasted_iota(jnp.int32, (TQ, TQ), 0)
    col = lax.broadcasted_iota(jnp.int32, (TQ, TQ), 1)
    lane = lax.broadcasted_iota(jnp.int32, (TQ, LANES), 1)

    m_ref[...] = jnp.full(m_ref.shape, -jnp.inf, F32)
    acc_ref[...] = jnp.zeros(acc_ref.shape, F32)
    bias_ref[...] = jnp.where(col <= row, 0.0, -jnp.inf)

    def block(j, masked):
        start = pl.multiple_of(j * TQ, TQ)

        def scores(h):
            lo = h * LANES
            s = lax.dot_general(q_ref[0, :, lo:lo + LANES], k_ref[0, pl.ds(start, TQ), lo:lo + LANES], nt,
                                preferred_element_type=F32)
            return s + bias_ref[...] if masked else s

        def softmax_step(h, s):
            m_old = m_ref[h]
            m_new = jnp.maximum(m_old, jnp.max(s, axis=-1, keepdims=True))
            m_ref[h] = m_new
            alpha = jnp.exp2(m_old - m_new)
            p = [jnp.exp2(s[:, c:c + LANES] - m_new) for c in range(0, TQ, LANES)]
            return alpha, jnp.concatenate(p, axis=-1).astype(BF16)

        def weighted_values(h, alpha, p):
            lo = h * LANES
            acc_ref[h] = alpha * acc_ref[h] + _dot(p, v_ref[0, pl.ds(start, TQ), lo:lo + LANES])

        pending_s, pending_p = {}, {}
        for t in range(MLA_HEADS + 2):
            if t < MLA_HEADS:
                pending_s[t] = scores(t)
            if 1 <= t <= MLA_HEADS:
                pending_p[t - 1] = softmax_step(t - 1, pending_s.pop(t - 1))
            if t >= 2:
                weighted_values(t - 2, *pending_p.pop(t - 2))

    def unmasked(j, carry):
        block(j, False)
        return carry

    def diagonal(j, carry):
        block(j, True)
        return carry

    lax.fori_loop(0, i, unmasked, 0)
    lax.fori_loop(i, i + 1, diagonal, 0)

    pairs = []
    for hp in range(MLA_HEADS // 2):
        even, odd = acc_ref[2 * hp], acc_ref[2 * hp + 1]
        num = jnp.where(lane < V_DIM, even, odd)
        den = jnp.where(lane < V_DIM, pltpu.roll(even, V_DIM, 1), pltpu.roll(odd, V_DIM, 1))
        pairs.append(num / den)
    y = jnp.concatenate(pairs, axis=-1)
    o_ref[0] = _rms(y, g_ref[...]).astype(BF16)


def _attn(q, k, v, g):
    bsz, s, d = q.shape
    w = g.shape[-1]
    return pl.pallas_call(
        _attn_kernel,
        grid=(bsz, s // TQ),
        in_specs=[pl.BlockSpec((1, TQ, d), lambda b, i: (b, i, 0)),
                  pl.BlockSpec((1, s, d), lambda b, i: (b, 0, 0)),
                  pl.BlockSpec((1, s, d), lambda b, i: (b, 0, 0)),
                  pl.BlockSpec(g.shape, lambda b, i: (0, 0))],
        out_specs=pl.BlockSpec((1, TQ, w), lambda b, i: (b, i, 0)),
        out_shape=jax.ShapeDtypeStruct((bsz, s, w), BF16),
        scratch_shapes=[pltpu.VMEM((MLA_HEADS, TQ, LANES), F32), pltpu.VMEM((MLA_HEADS, TQ, LANES), F32),
                        pltpu.VMEM((TQ, TQ), F32)],
        compiler_params=_cparams("arbitrary", "arbitrary"),
        name="attn",
    )(q, k, v, g)


def _post_kernel(yl_ref, ym_ref, x_ref, mod_ref, wol_ref, wom_ref, g2_ref, x1_ref, h2t_ref):
    mix = _dot(yl_ref[0], wol_ref[...]) + _dot(ym_ref[0], wom_ref[...])
    x1 = x_ref[0] + mod_ref[0, 2:3, :] * mix
    x1_ref[0] = x1
    h2 = _rms(x1, g2_ref[...]) * (1.0 + mod_ref[0, 4:5, :]) + mod_ref[0, 3:4, :]
    h2t_ref[...] = h2.T.astype(BF16)


def _post(yl, ym, x, mod3, wol, wom, g2):
    bsz, s, d = x.shape
    w = yl.shape[-1]
    nt = s // TM
    full = lambda a: pl.BlockSpec(a.shape, lambda b, i: (0,) * a.ndim)
    tok = lambda c: pl.BlockSpec((1, TM, c), lambda b, i: (b, i, 0))
    return pl.pallas_call(
        _post_kernel,
        grid=(bsz, nt),
        in_specs=[tok(w), tok(w), tok(d), pl.BlockSpec((1, 6, d), lambda b, i: (b, 0, 0)),
                  full(wol), full(wom), full(g2)],
        out_specs=[tok(d), pl.BlockSpec((d, TM), lambda b, i: (0, b * nt + i))],
        out_shape=[jax.ShapeDtypeStruct((bsz, s, d), F32),
                   jax.ShapeDtypeStruct((d, bsz * s), BF16)],
        compiler_params=_cparams("arbitrary", "arbitrary"),
        name="post",
    )(yl, ym, x, mod3, wol, wom, g2)


def _sort_pairs(n):
    pairs = []

    def merge(lo, hi, r):
        step = r * 2
        if step < hi - lo:
            merge(lo, hi, step)
            merge(lo + r, hi, step)
            pairs.extend((i, i + r) for i in range(lo + r, hi - r, step))
        else:
            pairs.append((lo, lo + r))

    def sort(lo, hi):
        if hi - lo >= 1:
            mid = lo + (hi - lo) // 2
            sort(lo, mid)
            sort(mid + 1, hi)
            merge(lo, hi, 1)

    sort(0, n - 1)
    return pairs


def _sort_desc(xs):
    xs = list(xs)
    for i, j in _sort_pairs(len(xs)):
        a, b = xs[i], xs[j]
        xs[i], xs[j] = jnp.maximum(a, b), jnp.minimum(a, b)
    return xs


def _merge_top(a, b):
    n = len(a)
    xs = [jnp.maximum(a[i], b[n - 1 - i]) for i in range(n)]
    dist = n // 2
    while dist >= 1:
        for i in range(n):
            if i & dist == 0:
                p, q = xs[i], xs[i + dist]
                xs[i], xs[i + dist] = jnp.maximum(p, q), jnp.minimum(p, q)
        dist //= 2
    return xs


def _kth_largest(vals, k):
    n = 1
    while n < len(vals):
        n *= 2
    present = [True] * len(vals) + [False] * (n - len(vals))
    ops = []
    for i, j in _sort_pairs(n):
        if present[i] and present[j]:
            ops.append(("cmp", i, j))
        elif present[j]:
            ops.append(("mov", i, j))
            present[i], present[j] = True, False
    need = {k}
    live = []
    for op in reversed(ops):
        kind, i, j = op
        if kind == "mov":
            if i in need:
                need.discard(i)
                need.add(j)
                live.append((kind, i, j, True, False))
        else:
            hi, lo = i in need, j in need
            if hi or lo:
                need.update((i, j))
                live.append((kind, i, j, hi, lo))
    xs = list(vals) + [None] * (n - len(vals))
    for kind, i, j, hi, lo in reversed(live):
        a, b = xs[i], xs[j]
        if kind == "mov":
            xs[i], xs[j] = b, None
        else:
            xs[i] = jnp.maximum(a, b) if hi else None
            xs[j] = jnp.minimum(a, b) if lo else None
    return xs[k]


def _top_values(slabs, n):
    groups = [_sort_desc(slabs[g:g + n]) for g in range(0, len(slabs), n)]
    while len(groups) > 1:
        groups = [_merge_top(groups[g], groups[g + 1]) for g in range(0, len(groups), 2)]
    return groups[0]


PACK_ROWS = 2 * SUBLANES


def _pair_bits(x):
    bits = pltpu.bitcast(x.astype(BF16).astype(F32), jnp.uint32)
    return (bits & jnp.uint32(0xFFFF0000)) | (bits >> 16)


def _route_kernel(h2t_ref, wqt_ref, kbig_ref, keys2_ref, c_ref, w1_ref, p_ref):
    tn = h2t_ref.shape[-1]
    nk = PEER_KEYS
    half_rows = PEER_HEADS * PEER_HALF
    qt = _dot(wqt_ref[...], h2t_ref[...])
    sub = lax.broadcasted_iota(jnp.int32, (SUBLANES, tn), 0)

    sc1 = _dot(kbig_ref[...], qt[:half_rows, :].astype(BF16))
    slabs = [sc1[PEER_HEADS * k:PEER_HEADS * (k + 1), :] for k in range(nk)]
    a = _top_values(slabs, PEER_TOPK)

    b = [None] * PEER_TOPK
    for h in range(PEER_HEADS):
        r0 = half_rows + h * PEER_HALF
        sc = _dot(keys2_ref[h], qt[r0:r0 + PEER_HALF, :].astype(BF16))
        tiles = [sc[SUBLANES * t:SUBLANES * (t + 1), :] for t in range(nk // SUBLANES)]
        xs = _sort_desc(tiles)
        shift = SUBLANES // 2
        while shift >= 1:
            xs = _merge_top(xs, [pltpu.roll(v, shift, 0) for v in xs])
            shift //= 2
        for ib in range(nk // PACK_ROWS):
            ranks, weights = [], []
            for tile in tiles[2 * ib:2 * ib + 2]:
                rk = jnp.full_like(tile, float(PEER_TOPK))
                for r in reversed(range(PEER_TOPK)):
                    rk = jnp.where(tile >= xs[r], float(r), rk)
                ranks.append(rk)
                weights.append(jnp.exp(tile - xs[0]))
            rk16 = jnp.concatenate(ranks, axis=0).astype(BF16)
            w16 = jnp.concatenate(weights, axis=0).astype(BF16)
            for lb in range(tn // LANES):
                r0 = 2 * h * PACK_ROWS
                p_ref[lb, ib, r0:r0 + PACK_ROWS, :] = rk16[:, lb * LANES:(lb + 1) * LANES]
                p_ref[lb, ib, r0 + PACK_ROWS:r0 + 2 * PACK_ROWS, :] = w16[:, lb * LANES:(lb + 1) * LANES]
        for r in range(PEER_TOPK):
            b[r] = xs[r] if h == 0 else jnp.where(sub == h, xs[r], b[r])

    pairs = [(i, j) for i in range(PEER_TOPK) for j in range(PEER_TOPK) if (i + 1) * (j + 1) <= PEER_TOPK]
    cands = {ij: a[ij[0]] + b[ij[1]] for ij in pairs}
    tau = _kth_largest(list(cands.values()), PEER_TOPK - 1)
    top = a[0] + b[0]
    z = jnp.zeros_like(tau)
    for c in cands.values():
        z = z + jnp.where(c >= tau, jnp.exp(c - top), 0.0)
    inv_z = 1.0 / z
    need = []
    for j in range(PEER_TOPK):
        t = None
        for i in range(PEER_TOPK):
            if (i, j) in cands:
                v = jnp.where(cands[(i, j)] >= tau, a[i], jnp.inf)
                t = v if t is None else jnp.minimum(t, v)
        need.append(t)
    for k in range(nk):
        cnt = jnp.zeros_like(tau)
        for j in range(PEER_TOPK):
            cnt = jnp.where(slabs[k] >= need[j], float(j + 1), cnt)
        c_ref[k] = _pair_bits(cnt)
        w1_ref[k] = _pair_bits(jnp.exp(slabs[k] - a[0]) * inv_z)


def _route(h2t, wqt, kbig, keys2):
    d, t = h2t.shape
    tn = TN_ROUTE
    by_key = pl.BlockSpec((PEER_KEYS, PEER_HEADS, tn), lambda i: (0, 0, i))
    packed = (PEER_KEYS // PACK_ROWS, 2 * PEER_HEADS * PACK_ROWS, LANES)
    return pl.pallas_call(
        _route_kernel,
        grid=(t // tn,),
        in_specs=[pl.BlockSpec((d, tn), lambda i: (0, i)),
                  pl.BlockSpec(wqt.shape, lambda i: (0, 0)),
                  pl.BlockSpec(kbig.shape, lambda i: (0, 0)),
                  pl.BlockSpec(keys2.shape, lambda i: (0, 0, 0))],
        out_specs=[by_key, by_key, pl.BlockSpec((tn // LANES,) + packed, lambda i: (i, 0, 0, 0))],
        out_shape=[jax.ShapeDtypeStruct((PEER_KEYS, PEER_HEADS, t), jnp.uint32),
                   jax.ShapeDtypeStruct((PEER_KEYS, PEER_HEADS, t), jnp.uint32),
                   jax.ShapeDtypeStruct((t // LANES,) + packed, BF16)],
        compiler_params=_cparams("arbitrary"),
        name="route",
    )(h2t, wqt, kbig, keys2)


def _dense_kernel(flag_ref, h2t_ref, u_ref, vt_ref, c_ref, w1_ref, p_ref, x1_ref, mod_ref,
                  fg_ref, o_ref, acc_ref, act_ref, gat_ref, *, nblk):
    s = pl.program_id(0)
    n_items = pl.num_programs(0) - 2
    tn = h2t_ref.shape[-1]
    eb = u_ref.shape[0]
    rows_per_step = eb // PEER_KEYS

    @pl.when(s == 0)
    def _():
        act_ref[1] = jnp.zeros(act_ref.shape[1:], F32)
        gat_ref[0] = jnp.zeros(gat_ref.shape[1:], BF16)

    jb = jnp.clip(s - 1, 0, n_items - 1) % nblk
    jc = jnp.clip(s - 2, 0, n_items - 1) % nblk

    @pl.when(jc == 0)
    def _():
        acc_ref[...] = jnp.zeros_like(acc_ref)

    n_lb = tn // LANES

    def row_tile(ref, i1, h, ls):
        return pltpu.bitcast(jnp.broadcast_to(ref[i1, h:h + 1, ls], (SUBLANES, LANES)), BF16)

    def gate_group(il, lb, part, oth):
        i1 = jb * rows_per_step + il
        base = il * PEER_KEYS
        ls = slice(lb * LANES, (lb + 1) * LANES)
        cntb = [row_tile(c_ref, i1, h, ls) for h in range(PEER_HEADS)]
        w1b = [row_tile(w1_ref, i1, h, ls) for h in range(PEER_HEADS)]
        tiles = PEER_KEYS // PACK_ROWS // GATE_PARTS
        for ib in range(part * tiles, (part + 1) * tiles):
            g = None
            for h in range(PEER_HEADS):
                r0 = 2 * h * PACK_ROWS
                rank2 = p_ref[lb, ib, r0:r0 + PACK_ROWS, :]
                w2 = p_ref[lb, ib, r0 + PACK_ROWS:r0 + 2 * PACK_ROWS, :]
                term = jnp.minimum(jnp.maximum(cntb[h] - rank2, 0.0), w2) * w1b[h]
                g = term if g is None else g + term
            r1 = base + ib * PACK_ROWS
            act = act_ref[oth, r1:r1 + PACK_ROWS, ls]
            gat_ref[oth, r1:r1 + PACK_ROWS, ls] = jax.nn.gelu(act.astype(BF16)) * g

    def matmul_chunks(cur):
        chunks = []
        for c0 in range(0, tn, MXU_COLS):
            cs = slice(c0, c0 + MXU_COLS)
            for r0 in range(0, acc_ref.shape[0], DOT_ROWS):
                rs = slice(r0, r0 + DOT_ROWS)

                def second(rs=rs, cs=cs):
                    acc_ref[rs, cs] += _dot(vt_ref[0, rs, :], gat_ref[cur, :, cs])
                chunks.append(second)
        for c0 in range(0, tn, MXU_COLS):
            cs = slice(c0, c0 + MXU_COLS)
            for r0 in range(0, eb, DOT_ROWS):
                rs = slice(r0, r0 + DOT_ROWS)

                def first(rs=rs, cs=cs):
                    act_ref[cur, rs, cs] = _dot(u_ref[rs, :], h2t_ref[:, cs])
                chunks.append(first)
        return chunks

    def sections(cur, oth):
        groups = [(il, lb, part) for il in range(rows_per_step) for lb in range(n_lb)
                  for part in range(GATE_PARTS)]
        chunks = matmul_chunks(cur)
        per_section = len(groups) // DENSE_SECTIONS
        for sec in range(DENSE_SECTIONS):
            @pl.when(flag_ref[0] == 0)
            def _(sec=sec):
                for gi in range(sec * per_section, (sec + 1) * per_section):
                    for ci in range(gi * len(chunks) // len(groups), (gi + 1) * len(chunks) // len(groups)):
                        chunks[ci]()
                    gate_group(*groups[gi], oth)

    for parity in range(2):
        @pl.when(s % 2 == parity)
        def _(parity=parity):
            sections(parity, 1 - parity)

    @pl.when(jnp.logical_and(jc == nblk - 1, s >= 2))
    def _():
        x2 = x1_ref[...] + mod_ref[0, 5:6, :] * acc_ref[...].T
        o_ref[...] = _rms(x2, fg_ref[...])


def _dense(h2t, u, vt3, cnt, w1, packed, x1, mod3, fg, seq):
    d, t = h2t.shape
    ne = vt3.shape[0]
    per_seq = seq // TN
    n_items = (t // TN) * ne
    item = lambda s, lag: jnp.clip(s - lag, 0, n_items - 1)
    tile = lambda s, lag: item(s, lag) // ne
    block = lambda s, lag: item(s, lag) % ne
    by_key = pl.BlockSpec((PEER_KEYS, PEER_HEADS, TN), lambda s: (0, 0, tile(s, 1)))
    return pl.pallas_call(
        functools.partial(_dense_kernel, nblk=ne),
        grid=(n_items + 2,),
        in_specs=[pl.BlockSpec(memory_space=pltpu.SMEM),
                  pl.BlockSpec((d, TN), lambda s: (0, tile(s, 0))),
                  pl.BlockSpec((EB, d), lambda s: (block(s, 0), 0)),
                  pl.BlockSpec((1, d, EB), lambda s: (block(s, 2), 0, 0)),
                  by_key, by_key,
                  pl.BlockSpec((TN // LANES,) + packed.shape[1:], lambda s: (tile(s, 1), 0, 0, 0)),
                  pl.BlockSpec((TN, d), lambda s: (tile(s, 2), 0)),
                  pl.BlockSpec((1, 6, d), lambda s: (tile(s, 2) // per_seq, 0, 0)),
                  pl.BlockSpec(fg.shape, lambda s: (0, 0))],
        out_specs=pl.BlockSpec((TN, d), lambda s: (tile(s, 2), 0)),
        out_shape=jax.ShapeDtypeStruct((t, d), F32),
        scratch_shapes=[pltpu.VMEM((d, TN), F32), pltpu.VMEM((2, EB, TN), F32),
                        pltpu.VMEM((2, EB, TN), BF16)],
        compiler_params=_cparams("arbitrary"),
        name="dense",
    )(jnp.zeros((1,), jnp.int32), h2t, u, vt3, cnt, w1, packed, x1, mod3, fg)


def _rot_half_cols(w):
    half = w.shape[-1] // 2
    return jnp.concatenate([-w[..., half:], w[..., :half]], axis=-1)


def _pad_cols(w, before, total):
    return jnp.pad(w, [(0, 0)] * (w.ndim - 1) + [(before, total - before - w.shape[-1])])


def kernel(x, c, positions, w_ada, b_ada, norm1_g, w_in, conv_w, conv_b, lru_wa, lru_ba, lru_wx, lru_bx,
           lru_lambda, q_norm_g, w_uq, kv_norm_g, w_ukv, lru_out_g, mla_out_g, w_out, norm2_g, peer_wq,
           peer_keys, peer_u, peer_v, final_g):
    bsz, seq, d = x.shape
    depth = w_ada.shape[0]
    lw = d // 2
    qr = w_uq.shape[1]
    kvr = w_ukv.shape[1]
    assert depth == 1, "the final norm is fused into the (single) layer's PEER kernel"
    assert seq % TM == 0 and seq % TN == 0 and seq % TQ == 0 and seq % TS == 0
    assert d == MLA_HEADS * LANES and lw == MLA_HEADS * V_DIM

    inv_freq = 1.0 / (ROPE_THETA ** (jnp.arange(0, QK_ROPE, 2, dtype=F32) / QK_ROPE))
    invf = _pad_cols(jnp.concatenate([inv_freq, inv_freq])[None, :], QK_NOPE, LANES)
    pos3 = positions.reshape(bsz, seq, 1)
    row = lambda v: v.reshape(1, -1)

    for l in range(depth):
        mod3 = _ada(c, w_ada[l], b_ada[l]).reshape(bsz, 6, d)

        wi = w_in[l]
        o0, o1, o2, o3 = 2 * lw, 2 * lw + qr, 2 * lw + qr + kvr, 2 * lw + qr + kvr + QK_ROPE
        wkr = wi[:, o2:o3]
        wkr2 = jnp.concatenate([_pad_cols(wkr, QK_NOPE, LANES),
                                _pad_cols(_rot_half_cols(wkr), QK_NOPE, LANES)], axis=1).astype(BF16)
        uq = w_uq[l].reshape(qr, MLA_HEADS, QK_NOPE + QK_ROPE)
        uq_main = _pad_cols(uq, 0, LANES).reshape(qr, d)
        uq_rot = _pad_cols(_rot_half_cols(uq[..., QK_NOPE:]), QK_NOPE, LANES).reshape(qr, d)
        wuq2 = jnp.concatenate([uq_main, uq_rot], axis=1).astype(BF16)
        ukv = w_ukv[l].reshape(kvr, MLA_HEADS, QK_NOPE + V_DIM)
        wuk = _pad_cols(ukv[..., :QK_NOPE], 0, LANES).reshape(kvr, d).astype(BF16)
        uv = ukv[..., QK_NOPE:]
        odd = (jnp.arange(MLA_HEADS) % 2 == 1)[None, :, None]
        wuv = jnp.where(odd, _pad_cols(uv, V_DIM, LANES), _pad_cols(uv, 0, LANES)).reshape(kvr, d).astype(BF16)
        half = jnp.arange(LANES)[None, :] >= V_DIM
        vones = (half != odd[0]).astype(F32).reshape(1, d)

        lru_in, q, k, v = _pre(x, mod3, row(norm1_g[l]), pos3, invf, wi[:, :o0].astype(BF16),
                               wi[:, o0:o1].astype(BF16), wi[:, o1:o2].astype(BF16), wkr2,
                               row(q_norm_g[l]), wuq2, row(kv_norm_g[l]), wuk, wuv, vones)

        eye = jnp.eye(LRU_BLOCKS, dtype=F32)
        blockdiag = lambda w: jnp.einsum("hij,hg->higj", w, eye).reshape(lw, lw)
        wg = jnp.concatenate([blockdiag(lru_wa[l]), blockdiag(lru_wx[l])], axis=1).astype(BF16)
        bg = jnp.concatenate([lru_ba[l], lru_bx[l]])[None, :]
        yl = _lru(lru_in, conv_w[l], row(conv_b[l]), wg, bg, row(lru_lambda[l]), row(lru_out_g[l]))

        ym = _attn(q, k, v, row(mla_out_g[l]))

        wo = w_out[l].astype(BF16)
        x1, h2t = _post(yl, ym, x, mod3, wo[:lw], wo[lw:], row(norm2_g[l]))

        wqt = (peer_wq[l].reshape(d, PEER_HEADS, 2, PEER_HALF).transpose(2, 1, 3, 0)
               .reshape(2 * PEER_HEADS * PEER_HALF, d).astype(BF16))
        keys = peer_keys[l].astype(BF16)
        kbig = jnp.einsum("hkd,hg->khgd", keys[:, 0], jnp.eye(PEER_HEADS, dtype=BF16)).reshape(
            PEER_KEYS * PEER_HEADS, PEER_HEADS * PEER_HALF)
        cnt, w1, packed = _route(h2t, wqt, kbig, keys[:, 1])

        ne = peer_u.shape[1] // EB
        vt3 = peer_v[l].astype(BF16).reshape(ne, EB, d).transpose(0, 2, 1)
        out = _dense(h2t, peer_u[l].astype(BF16), vt3, cnt, w1, packed,
                     x1.reshape(bsz * seq, d), mod3, row(final_g), seq)
        x = out.reshape(bsz, seq, d)
    return x
```

```python
import functools

import jax
import jax.numpy as jnp
from jax import lax
from jax.experimental import pallas as pl
from jax.experimental.pallas import tpu as pltpu

F32 = jnp.float32
BF16 = jnp.bfloat16

LRU_BLOCKS = 8
CONV_WIDTH = 4
LRU_C = 8.0
MLA_HEADS = 8
QK_NOPE = 64
QK_ROPE = 32
V_DIM = 64
ROPE_THETA = 10000.0
PEER_HEADS = 8
PEER_KEYS = 128
PEER_HALF = 128
PEER_TOPK = 16
EPS = 1e-6
LOG2_E = 1.4426950408889634

LANES = 128
SUBLANES = 8
VMEM_LIMIT = 56 * 1024 * 1024

TM = 512
TS = 512
TQ = 512
TN_ROUTE = 512
TN = 512
EB = 1024
MXU_COLS = 256
DOT_ROWS = 128
GATE_PARTS = 1
DENSE_SECTIONS = 1


def _cparams(*sem, flags=None):
    return pltpu.CompilerParams(dimension_semantics=sem, vmem_limit_bytes=VMEM_LIMIT, flags=flags)


def _rms(x, g):
    return x * lax.rsqrt(jnp.mean(x * x, axis=-1, keepdims=True) + EPS) * g


def _dot(a, b):
    return jnp.dot(a, b, preferred_element_type=F32)


def _ada_kernel(c_ref, w_ref, b_ref, o_ref):
    ca = jax.nn.silu(c_ref[...])
    o_ref[...] = jnp.dot(ca, w_ref[...], preferred_element_type=F32,
                         precision=lax.Precision.HIGHEST) + b_ref[...]


def _ada(c, w, b):
    bsz, d = c.shape
    n = w.shape[1]
    return pl.pallas_call(
        _ada_kernel,
        grid=(n // d,),
        in_specs=[pl.BlockSpec((bsz, d), lambda j: (0, 0)),
                  pl.BlockSpec((d, d), lambda j: (0, j)),
                  pl.BlockSpec((1, d), lambda j: (0, j))],
        out_specs=pl.BlockSpec((bsz, d), lambda j: (0, j)),
        out_shape=jax.ShapeDtypeStruct((bsz, n), F32),
        compiler_params=_cparams("arbitrary"),
        name="ada",
    )(c, w, b.reshape(1, n))


def _pre_kernel(x_ref, mod_ref, g1_ref, pos_ref, invf_ref, wlru_ref, wq_ref, wkv_ref, wkr_ref,
                qg_ref, wuq_ref, kvg_ref, wuk_ref, wuv_ref, vones_ref, lru_ref, q_ref, k_ref, v_ref):
    d = x_ref.shape[-1]
    x = x_ref[0]
    shift = mod_ref[0, 0:1, :]
    scale = mod_ref[0, 1:2, :]
    hb = (_rms(x, g1_ref[...]) * (1.0 + scale) + shift).astype(BF16)
    lru_ref[0] = _dot(hb, wlru_ref[...])

    ang = pos_ref[0].astype(F32) * invf_ref[...]
    cos = jnp.cos(ang)
    sin = jnp.sin(ang)

    qn = _rms(_dot(hb, wq_ref[...]), qg_ref[...]).astype(BF16)
    q2 = _dot(qn, wuq_ref[...])
    q_scale = (QK_NOPE + QK_ROPE) ** -0.5 * LOG2_E
    for h in range(MLA_HEADS):
        lo = h * LANES
        q_ref[0, :, lo:lo + LANES] = ((q2[:, lo:lo + LANES] * cos
                                       + q2[:, d + lo:d + lo + LANES] * sin) * q_scale).astype(BF16)

    kvn = _rms(_dot(hb, wkv_ref[...]), kvg_ref[...]).astype(BF16)
    kn = _dot(kvn, wuk_ref[...])
    v_ref[0] = (_dot(kvn, wuv_ref[...]) + vones_ref[...]).astype(BF16)
    kr2 = _dot(hb, wkr_ref[...])
    krot = kr2[:, :LANES] * cos + kr2[:, LANES:] * sin
    for h in range(MLA_HEADS):
        lo = h * LANES
        k_ref[0, :, lo:lo + LANES] = (kn[:, lo:lo + LANES] + krot).astype(BF16)


def _pre(x, mod3, g1, pos3, invf, wlru, wq, wkv, wkr2, qg, wuq2, kvg, wuk, wuv, vones):
    bsz, s, d = x.shape
    full = lambda a: pl.BlockSpec(a.shape, lambda b, i: (0,) * a.ndim)
    tok = lambda w: pl.BlockSpec((1, TM, w), lambda b, i: (b, i, 0))
    return pl.pallas_call(
        _pre_kernel,
        grid=(bsz, s // TM),
        in_specs=[tok(d), pl.BlockSpec((1, 6, d), lambda b, i: (b, 0, 0)), full(g1), tok(1), full(invf),
                  full(wlru), full(wq), full(wkv), full(wkr2), full(qg), full(wuq2), full(kvg),
                  full(wuk), full(wuv), full(vones)],
        out_specs=[tok(d), tok(d), tok(d), tok(d)],
        out_shape=[jax.ShapeDtypeStruct((bsz, s, d), F32),
                   jax.ShapeDtypeStruct((bsz, s, d), BF16),
                   jax.ShapeDtypeStruct((bsz, s, d), BF16),
                   jax.ShapeDtypeStruct((bsz, s, d), BF16)],
        compiler_params=_cparams("arbitrary", "arbitrary"),
        name="pre",
    )(x, mod3, g1, pos3, invf, wlru, wq, wkv, wkr2, qg, wuq2, kvg, wuk, wuv, vones)


def _lru_kernel(x_ref, cw_ref, cb_ref, wg_ref, bg_ref, lam_ref, og_ref, o_ref, prev_ref, h_ref):
    w = o_ref.shape[-1]

    @pl.when(pl.program_id(1) == 0)
    def _():
        prev_ref[...] = jnp.zeros_like(prev_ref)
        h_ref[...] = jnp.zeros_like(h_ref)

    x = x_ref[0, :, :w]
    gate = x_ref[0, :, w:]
    tail = prev_ref[...]
    row = lax.broadcasted_iota(jnp.int32, x.shape, 0)
    row8 = lax.broadcasted_iota(jnp.int32, tail.shape, 0)

    xc = cb_ref[...]
    for k in range(CONV_WIDTH - 1):
        back = CONV_WIDTH - 1 - k
        shifted = pltpu.roll(x, back, 0)
        first = jnp.where(row8 < back, pltpu.roll(tail, back, 0), shifted[:SUBLANES, :])
        shifted = jnp.concatenate([first, shifted[SUBLANES:, :]], axis=0)
        xc = xc + cw_ref[k:k + 1, :] * shifted
    xc = xc + cw_ref[CONV_WIDTH - 1:CONV_WIDTH, :] * x
    prev_ref[...] = x[TS - SUBLANES:, :]

    pre = _dot(xc.astype(BF16), wg_ref[...]) + bg_ref[...]
    r = jax.nn.sigmoid(pre[:, :w])
    i = jax.nn.sigmoid(pre[:, w:])
    log_a = -LRU_C * r * jax.nn.softplus(-lam_ref[...])
    a = jnp.exp(log_a)
    one_minus = 1.0 - a * a
    root = jnp.where(one_minus > 0.0, one_minus * lax.rsqrt(one_minus), 0.0)
    b = root * (i * xc)

    sub = row % SUBLANES
    dist = 1
    while dist < SUBLANES:
        keep = sub >= dist
        b = jnp.where(keep, a * pltpu.roll(b, dist, 0) + b, b)
        a = jnp.where(keep, a * pltpu.roll(a, dist, 0), a)
        dist *= 2
    state = h_ref[0:1, :]
    groups = []
    for g0 in range(0, TS, SUBLANES):
        hg = b[g0:g0 + SUBLANES, :] + a[g0:g0 + SUBLANES, :] * state
        groups.append(hg)
        state = hg[SUBLANES - 1:SUBLANES, :]
    h = jnp.concatenate(groups, axis=0)
    h_ref[...] = jnp.broadcast_to(state, h_ref.shape)

    y = jax.nn.gelu(gate) * h
    o_ref[0] = _rms(y, og_ref[...]).astype(BF16)


def _lru(lru_in, cw, cb, wg, bg, lam, og):
    bsz, s, d = lru_in.shape
    w = d // 2
    full = lambda a: pl.BlockSpec(a.shape, lambda b, t: (0,) * a.ndim)
    return pl.pallas_call(
        _lru_kernel,
        grid=(bsz, s // TS),
        in_specs=[pl.BlockSpec((1, TS, d), lambda b, t: (b, t, 0)),
                  full(cw), full(cb), full(wg), full(bg), full(lam), full(og)],
        out_specs=pl.BlockSpec((1, TS, w), lambda b, t: (b, t, 0)),
        out_shape=jax.ShapeDtypeStruct((bsz, s, w), BF16),
        scratch_shapes=[pltpu.VMEM((SUBLANES, w), F32), pltpu.VMEM((SUBLANES, w), F32)],
        compiler_params=_cparams("arbitrary", "arbitrary"),
        name="lru",
    )(lru_in, cw, cb, wg, bg, lam, og)


def _attn_kernel(q_ref, k_ref, v_ref, g_ref, o_ref, m_ref, acc_ref, bias_ref):
    i = pl.program_id(1)
    nt = (((1,), (1,)), ((), ()))
    row = lax.broadcasted_iota(jnp.int32, (TQ, TQ), 0)
    col = lax.broadcasted_iota(jnp.int32, (TQ, TQ), 1)
    lane = lax.broadcasted_iota(jnp.int32, (TQ, LANES), 1)

    m_ref[...] = jnp.full(m_ref.shape, -jnp.inf, F32)
    acc_ref[...] = jnp.zeros(acc_ref.shape, F32)
    bias_ref[...] = jnp.where(col <= row, 0.0, -jnp.inf)

    def block(j, masked):
        start = pl.multiple_of(j * TQ, TQ)

        def scores(h):
            lo = h * LANES
            s = lax.dot_general(q_ref[0, :, lo:lo + LANES], k_ref[0, pl.ds(start, TQ), lo:lo + LANES], nt,
                                preferred_element_type=F32)
            return s + bias_ref[...] if masked else s

        def softmax_step(h, s):
            m_old = m_ref[h]
            m_new = jnp.maximum(m_old, jnp.max(s, axis=-1, keepdims=True))
            m_ref[h] = m_new
            alpha = jnp.exp2(m_old - m_new)
            p = [jnp.exp2(s[:, c:c + LANES] - m_new) for c in range(0, TQ, LANES)]
            return alpha, jnp.concatenate(p, axis=-1).astype(BF16)

        def weighted_values(h, alpha, p):
            lo = h * LANES
            acc_ref[h] = alpha * acc_ref[h] + _dot(p, v_ref[0, pl.ds(start, TQ), lo:lo + LANES])

        pending_s, pending_p = {}, {}
        for t in range(MLA_HEADS + 2):
            if t < MLA_HEADS:
                pending_s[t] = scores(t)
            if 1 <= t <= MLA_HEADS:
                pending_p[t - 1] = softmax_step(t - 1, pending_s.pop(t - 1))
            if t >= 2:
                weighted_values(t - 2, *pending_p.pop(t - 2))

    def unmasked(j, carry):
        block(j, False)
        return carry

    def diagonal(j, carry):
        block(j, True)
        return carry

    lax.fori_loop(0, i, unmasked, 0)
    lax.fori_loop(i, i + 1, diagonal, 0)

    pairs = []
    for hp in range(MLA_HEADS // 2):
        even, odd = acc_ref[2 * hp], acc_ref[2 * hp + 1]
        num = jnp.where(lane < V_DIM, even, odd)
        den = jnp.where(lane < V_DIM, pltpu.roll(even, V_DIM, 1), pltpu.roll(odd, V_DIM, 1))
        pairs.append(num / den)
    y = jnp.concatenate(pairs, axis=-1)
    o_ref[0] = _rms(y, g_ref[...]).astype(BF16)


def _attn(q, k, v, g):
    bsz, s, d = q.shape
    w = g.shape[-1]
    return pl.pallas_call(
        _attn_kernel,
        grid=(bsz, s // TQ),
        in_specs=[pl.BlockSpec((1, TQ, d), lambda b, i: (b, i, 0)),
                  pl.BlockSpec((1, s, d), lambda b, i: (b, 0, 0)),
                  pl.BlockSpec((1, s, d), lambda b, i: (b, 0, 0)),
                  pl.BlockSpec(g.shape, lambda b, i: (0, 0))],
        out_specs=pl.BlockSpec((1, TQ, w), lambda b, i: (b, i, 0)),
        out_shape=jax.ShapeDtypeStruct((bsz, s, w), BF16),
        scratch_shapes=[pltpu.VMEM((MLA_HEADS, TQ, LANES), F32), pltpu.VMEM((MLA_HEADS, TQ, LANES), F32),
                        pltpu.VMEM((TQ, TQ), F32)],
        compiler_params=_cparams("arbitrary", "arbitrary"),
        name="attn",
    )(q, k, v, g)


def _post_kernel(yl_ref, ym_ref, x_ref, mod_ref, wol_ref, wom_ref, g2_ref, x1_ref, h2t_ref):
    mix = _dot(yl_ref[0], wol_ref[...]) + _dot(ym_ref[0], wom_ref[...])
    x1 = x_ref[0] + mod_ref[0, 2:3, :] * mix
    x1_ref[0] = x1
    h2 = _rms(x1, g2_ref[...]) * (1.0 + mod_ref[0, 4:5, :]) + mod_ref[0, 3:4, :]
    h2t_ref[...] = h2.T.astype(BF16)


def _post(yl, ym, x, mod3, wol, wom, g2):
    bsz, s, d = x.shape
    w = yl.shape[-1]
    nt = s // TM
    full = lambda a: pl.BlockSpec(a.shape, lambda b, i: (0,) * a.ndim)
    tok = lambda c: pl.BlockSpec((1, TM, c), lambda b, i: (b, i, 0))
    return pl.pallas_call(
        _post_kernel,
        grid=(bsz, nt),
        in_specs=[tok(w), tok(w), tok(d), pl.BlockSpec((1, 6, d), lambda b, i: (b, 0, 0)),
                  full(wol), full(wom), full(g2)],
        out_specs=[tok(d), pl.BlockSpec((d, TM), lambda b, i: (0, b * nt + i))],
        out_shape=[jax.ShapeDtypeStruct((bsz, s, d), F32),
                   jax.ShapeDtypeStruct((d, bsz * s), BF16)],
        compiler_params=_cparams("arbitrary", "arbitrary"),
        name="post",
    )(yl, ym, x, mod3, wol, wom, g2)


def _sort_pairs(n):
    pairs = []

    def merge(lo, hi, r):
        step = r * 2
        if step < hi - lo:
            merge(lo, hi, step)
            merge(lo + r, hi, step)
            pairs.extend((i, i + r) for i in range(lo + r, hi - r, step))
        else:
            pairs.append((lo, lo + r))

    def sort(lo, hi):
        if hi - lo >= 1:
            mid = lo + (hi - lo) // 2
            sort(lo, mid)
            sort(mid + 1, hi)
            merge(lo, hi, 1)

    sort(0, n - 1)
    return pairs


def _sort_desc(xs):
    xs = list(xs)
    for i, j in _sort_pairs(len(xs)):
        a, b = xs[i], xs[j]
        xs[i], xs[j] = jnp.maximum(a, b), jnp.minimum(a, b)
    return xs


def _merge_top(a, b):
    n = len(a)
    xs = [jnp.maximum(a[i], b[n - 1 - i]) for i in range(n)]
    dist = n // 2
    while dist >= 1:
        for i in range(n):
            if i & dist == 0:
                p, q = xs[i], xs[i + dist]
                xs[i], xs[i + dist] = jnp.maximum(p, q), jnp.minimum(p, q)
        dist //= 2
    return xs


def _kth_largest(vals, k):
    n = 1
    while n < len(vals):
        n *= 2
    present = [True] * len(vals) + [False] * (n - len(vals))
    ops = []
    for i, j in _sort_pairs(n):
        if present[i] and present[j]:
            ops.append(("cmp", i, j))
        elif present[j]:
            ops.append(("mov", i, j))
            present[i], present[j] = True, False
    need = {k}
    live = []
    for op in reversed(ops):
        kind, i, j = op
        if kind == "mov":
            if i in need:
                need.discard(i)
                need.add(j)
                live.append((kind, i, j, True, False))
        else:
            hi, lo = i in need, j in need
            if hi or lo:
                need.update((i, j))
                live.append((kind, i, j, hi, lo))
    xs = list(vals) + [None] * (n - len(vals))
    for kind, i, j, hi, lo in reversed(live):
        a, b = xs[i], xs[j]
        if kind == "mov":
            xs[i], xs[j] = b, None
        else:
            xs[i] = jnp.maximum(a, b) if hi else None
            xs[j] = jnp.minimum(a, b) if lo else None
    return xs[k]


def _top_values(slabs, n):
    groups = [_sort_desc(slabs[g:g + n]) for g in range(0, len(slabs), n)]
    while len(groups) > 1:
        groups = [_merge_top(groups[g], groups[g + 1]) for g in range(0, len(groups), 2)]
    return groups[0]


PACK_ROWS = 2 * SUBLANES


def _pair_bits(x):
    bits = pltpu.bitcast(x.astype(BF16).astype(F32), jnp.uint32)
    return (bits & jnp.uint32(0xFFFF0000)) | (bits >> 16)


def _route_kernel(h2t_ref, wqt_ref, kbig_ref, keys2_ref, c_ref, w1_ref, p_ref):
    tn = h2t_ref.shape[-1]
    nk = PEER_KEYS
    half_rows = PEER_HEADS * PEER_HALF
    qt = _dot(wqt_ref[...], h2t_ref[...])
    sub = lax.broadcasted_iota(jnp.int32, (SUBLANES, tn), 0)

    sc1 = _dot(kbig_ref[...], qt[:half_rows, :].astype(BF16))
    slabs = [sc1[PEER_HEADS * k:PEER_HEADS * (k + 1), :] for k in range(nk)]
    a = _top_values(slabs, PEER_TOPK)

    b = [None] * PEER_TOPK
    for h in range(PEER_HEADS):
        r0 = half_rows + h * PEER_HALF
        sc = _dot(keys2_ref[h], qt[r0:r0 + PEER_HALF, :].astype(BF16))
        tiles = [sc[SUBLANES * t:SUBLANES * (t + 1), :] for t in range(nk // SUBLANES)]
        xs = _sort_desc(tiles)
        shift = SUBLANES // 2
        while shift >= 1:
            xs = _merge_top(xs, [pltpu.roll(v, shift, 0) for v in xs])
            shift //= 2
        for ib in range(nk // PACK_ROWS):
            ranks, weights = [], []
            for tile in tiles[2 * ib:2 * ib + 2]:
                rk = jnp.full_like(tile, float(PEER_TOPK))
                for r in reversed(range(PEER_TOPK)):
                    rk = jnp.where(tile >= xs[r], float(r), rk)
                ranks.append(rk)
                weights.append(jnp.exp(tile - xs[0]))
            rk16 = jnp.concatenate(ranks, axis=0).astype(BF16)
            w16 = jnp.concatenate(weights, axis=0).astype(BF16)
            for lb in range(tn // LANES):
                r0 = 2 * h * PACK_ROWS
                p_ref[lb, ib, r0:r0 + PACK_ROWS, :] = rk16[:, lb * LANES:(lb + 1) * LANES]
                p_ref[lb, ib, r0 + PACK_ROWS:r0 + 2 * PACK_ROWS, :] = w16[:, lb * LANES:(lb + 1) * LANES]
        for r in range(PEER_TOPK):
            b[r] = xs[r] if h == 0 else jnp.where(sub == h, xs[r], b[r])

    pairs = [(i, j) for i in range(PEER_TOPK) for j in range(PEER_TOPK) if (i + 1) * (j + 1) <= PEER_TOPK]
    cands = {ij: a[ij[0]] + b[ij[1]] for ij in pairs}
    tau = _kth_largest(list(cands.values()), PEER_TOPK - 1)
    top = a[0] + b[0]
    z = jnp.zeros_like(tau)
    for c in cands.values():
        z = z + jnp.where(c >= tau, jnp.exp(c - top), 0.0)
    inv_z = 1.0 / z
    need = []
    for j in range(PEER_TOPK):
        t = None
        for i in range(PEER_TOPK):
            if (i, j) in cands:
                v = jnp.where(cands[(i, j)] >= tau, a[i], jnp.inf)
                t = v if t is None else jnp.minimum(t, v)
        need.append(t)
    for k in range(nk):
        cnt = jnp.zeros_like(tau)
        for j in range(PEER_TOPK):
            cnt = jnp.where(slabs[k] >= need[j], float(j + 1), cnt)
        c_ref[k] = _pair_bits(cnt)
        w1_ref[k] = _pair_bits(jnp.exp(slabs[k] - a[0]) * inv_z)


def _route(h2t, wqt, kbig, keys2):
    d, t = h2t.shape
    tn = TN_ROUTE
    by_key = pl.BlockSpec((PEER_KEYS, PEER_HEADS, tn), lambda i: (0, 0, i))
    packed = (PEER_KEYS // PACK_ROWS, 2 * PEER_HEADS * PACK_ROWS, LANES)
    return pl.pallas_call(
        _route_kernel,
        grid=(t // tn,),
        in_specs=[pl.BlockSpec((d, tn), lambda i: (0, i)),
                  pl.BlockSpec(wqt.shape, lambda i: (0, 0)),
                  pl.BlockSpec(kbig.shape, lambda i: (0, 0)),
                  pl.BlockSpec(keys2.shape, lambda i: (0, 0, 0))],
        out_specs=[by_key, by_key, pl.BlockSpec((tn // LANES,) + packed, lambda i: (i, 0, 0, 0))],
        out_shape=[jax.ShapeDtypeStruct((PEER_KEYS, PEER_HEADS, t), jnp.uint32),
                   jax.ShapeDtypeStruct((PEER_KEYS, PEER_HEADS, t), jnp.uint32),
                   jax.ShapeDtypeStruct((t // LANES,) + packed, BF16)],
        compiler_params=_cparams("arbitrary"),
        name="route",
    )(h2t, wqt, kbig, keys2)


def _dense_kernel(flag_ref, h2t_ref, u_ref, vt_ref, c_ref, w1_ref, p_ref, x1_ref, mod_ref,
                  fg_ref, o_ref, acc_ref, act_ref, gat_ref, *, nblk):
    s = pl.program_id(0)
    n_items = pl.num_programs(0) - 2
    tn = h2t_ref.shape[-1]
    eb = u_ref.shape[0]
    rows_per_step = eb // PEER_KEYS

    @pl.when(s == 0)
    def _():
        act_ref[1] = jnp.zeros(act_ref.shape[1:], F32)
        gat_ref[0] = jnp.zeros(gat_ref.shape[1:], BF16)

    jb = jnp.clip(s - 1, 0, n_items - 1) % nblk
    jc = jnp.clip(s - 2, 0, n_items - 1) % nblk

    @pl.when(jc == 0)
    def _():
        acc_ref[...] = jnp.zeros_like(acc_ref)

    n_lb = tn // LANES

    def row_tile(ref, i1, h, ls):
        return pltpu.bitcast(jnp.broadcast_to(ref[i1, h:h + 1, ls], (SUBLANES, LANES)), BF16)

    def gate_group(il, lb, part, oth):
        i1 = jb * rows_per_step + il
        base = il * PEER_KEYS
        ls = slice(lb * LANES, (lb + 1) * LANES)
        cntb = [row_tile(c_ref, i1, h, ls) for h in range(PEER_HEADS)]
        w1b = [row_tile(w1_ref, i1, h, ls) for h in range(PEER_HEADS)]
        tiles = PEER_KEYS // PACK_ROWS // GATE_PARTS
        for ib in range(part * tiles, (part + 1) * tiles):
            g = None
            for h in range(PEER_HEADS):
                r0 = 2 * h * PACK_ROWS
                rank2 = p_ref[lb, ib, r0:r0 + PACK_ROWS, :]
                w2 = p_ref[lb, ib, r0 + PACK_ROWS:r0 + 2 * PACK_ROWS, :]
                term = jnp.minimum(jnp.maximum(cntb[h] - rank2, 0.0), w2) * w1b[h]
                g = term if g is None else g + term
            r1 = base + ib * PACK_ROWS
            act = act_ref[oth, r1:r1 + PACK_ROWS, ls]
            gat_ref[oth, r1:r1 + PACK_ROWS, ls] = jax.nn.gelu(act.astype(BF16)) * g

    def matmul_chunks(cur):
        chunks = []
        for c0 in range(0, tn, MXU_COLS):
            cs = slice(c0, c0 + MXU_COLS)
            for r0 in range(0, acc_ref.shape[0], DOT_ROWS):
                rs = slice(r0, r0 + DOT_ROWS)

                def second(rs=rs, cs=cs):
                    acc_ref[rs, cs] += _dot(vt_ref[0, rs, :], gat_ref[cur, :, cs])
                chunks.append(second)
        for c0 in range(0, tn, MXU_COLS):
            cs = slice(c0, c0 + MXU_COLS)
            for r0 in range(0, eb, DOT_ROWS):
                rs = slice(r0, r0 + DOT_ROWS)

                def first(rs=rs, cs=cs):
                    act_ref[cur, rs, cs] = _dot(u_ref[rs, :], h2t_ref[:, cs])
                chunks.append(first)
        return chunks

    def sections(cur, oth):
        groups = [(il, lb, part) for il in range(rows_per_step) for lb in range(n_lb)
                  for part in range(GATE_PARTS)]
        chunks = matmul_chunks(cur)
        per_section = len(groups) // DENSE_SECTIONS
        for sec in range(DENSE_SECTIONS):
            @pl.when(flag_ref[0] == 0)
            def _(sec=sec):
                for gi in range(sec * per_section, (sec + 1) * per_section):
                    for ci in range(gi * len(chunks) // len(groups), (gi + 1) * len(chunks) // len(groups)):
                        chunks[ci]()
                    gate_group(*groups[gi], oth)

    for parity in range(2):
        @pl.when(s % 2 == parity)
        def _(parity=parity):
            sections(parity, 1 - parity)

    @pl.when(jnp.logical_and(jc == nblk - 1, s >= 2))
    def _():
        x2 = x1_ref[...] + mod_ref[0, 5:6, :] * acc_ref[...].T
        o_ref[...] = _rms(x2, fg_ref[...])


def _dense(h2t, u, vt3, cnt, w1, packed, x1, mod3, fg, seq):
    d, t = h2t.shape
    ne = vt3.shape[0]
    per_seq = seq // TN
    n_items = (t // TN) * ne
    item = lambda s, lag: jnp.clip(s - lag, 0, n_items - 1)
    tile = lambda s, lag: item(s, lag) // ne
    block = lambda s, lag: item(s, lag) % ne
    by_key = pl.BlockSpec((PEER_KEYS, PEER_HEADS, TN), lambda s: (0, 0, tile(s, 1)))
    return pl.pallas_call(
        functools.partial(_dense_kernel, nblk=ne),
        grid=(n_items + 2,),
        in_specs=[pl.BlockSpec(memory_space=pltpu.SMEM),
                  pl.BlockSpec((d, TN), lambda s: (0, tile(s, 0))),
                  pl.BlockSpec((EB, d), lambda s: (block(s, 0), 0)),
                  pl.BlockSpec((1, d, EB), lambda s: (block(s, 2), 0, 0)),
                  by_key, by_key,
                  pl.BlockSpec((TN // LANES,) + packed.shape[1:], lambda s: (tile(s, 1), 0, 0, 0)),
                  pl.BlockSpec((TN, d), lambda s: (tile(s, 2), 0)),
                  pl.BlockSpec((1, 6, d), lambda s: (tile(s, 2) // per_seq, 0, 0)),
                  pl.BlockSpec(fg.shape, lambda s: (0, 0))],
        out_specs=pl.BlockSpec((TN, d), lambda s: (tile(s, 2), 0)),
        out_shape=jax.ShapeDtypeStruct((t, d), F32),
        scratch_shapes=[pltpu.VMEM((d, TN), F32), pltpu.VMEM((2, EB, TN), F32),
                        pltpu.VMEM((2, EB, TN), BF16)],
        compiler_params=_cparams("arbitrary"),
        name="dense",
    )(jnp.zeros((1,), jnp.int32), h2t, u, vt3, cnt, w1, packed, x1, mod3, fg)


def _rot_half_cols(w):
    half = w.shape[-1] // 2
    return jnp.concatenate([-w[..., half:], w[..., :half]], axis=-1)


def _pad_cols(w, before, total):
    return jnp.pad(w, [(0, 0)] * (w.ndim - 1) + [(before, total - before - w.shape[-1])])


def kernel(x, c, positions, w_ada, b_ada, norm1_g, w_in, conv_w, conv_b, lru_wa, lru_ba, lru_wx, lru_bx,
           lru_lambda, q_norm_g, w_uq, kv_norm_g, w_ukv, lru_out_g, mla_out_g, w_out, norm2_g, peer_wq,
           peer_keys, peer_u, peer_v, final_g):
    bsz, seq, d = x.shape
    depth = w_ada.shape[0]
    lw = d // 2
    qr = w_uq.shape[1]
    kvr = w_ukv.shape[1]
    assert depth == 1, "the final norm is fused into the (single) layer's PEER kernel"
    assert seq % TM == 0 and seq % TN == 0 and seq % TQ == 0 and seq % TS == 0
    assert d == MLA_HEADS * LANES and lw == MLA_HEADS * V_DIM

    inv_freq = 1.0 / (ROPE_THETA ** (jnp.arange(0, QK_ROPE, 2, dtype=F32) / QK_ROPE))
    invf = _pad_cols(jnp.concatenate([inv_freq, inv_freq])[None, :], QK_NOPE, LANES)
    pos3 = positions.reshape(bsz, seq, 1)
    row = lambda v: v.reshape(1, -1)

    for l in range(depth):
        mod3 = _ada(c, w_ada[l], b_ada[l]).reshape(bsz, 6, d)

        wi = w_in[l]
        o0, o1, o2, o3 = 2 * lw, 2 * lw + qr, 2 * lw + qr + kvr, 2 * lw + qr + kvr + QK_ROPE
        wkr = wi[:, o2:o3]
        wkr2 = jnp.concatenate([_pad_cols(wkr, QK_NOPE, LANES),
                                _pad_cols(_rot_half_cols(wkr), QK_NOPE, LANES)], axis=1).astype(BF16)
        uq = w_uq[l].reshape(qr, MLA_HEADS, QK_NOPE + QK_ROPE)
        uq_main = _pad_cols(uq, 0, LANES).reshape(qr, d)
        uq_rot = _pad_cols(_rot_half_cols(uq[..., QK_NOPE:]), QK_NOPE, LANES).reshape(qr, d)
        wuq2 = jnp.concatenate([uq_main, uq_rot], axis=1).astype(BF16)
        ukv = w_ukv[l].reshape(kvr, MLA_HEADS, QK_NOPE + V_DIM)
        wuk = _pad_cols(ukv[..., :QK_NOPE], 0, LANES).reshape(kvr, d).astype(BF16)
        uv = ukv[..., QK_NOPE:]
        odd = (jnp.arange(MLA_HEADS) % 2 == 1)[None, :, None]
        wuv = jnp.where(odd, _pad_cols(uv, V_DIM, LANES), _pad_cols(uv, 0, LANES)).reshape(kvr, d).astype(BF16)
        half = jnp.arange(LANES)[None, :] >= V_DIM
        vones = (half != odd[0]).astype(F32).reshape(1, d)

        lru_in, q, k, v = _pre(x, mod3, row(norm1_g[l]), pos3, invf, wi[:, :o0].astype(BF16),
                               wi[:, o0:o1].astype(BF16), wi[:, o1:o2].astype(BF16), wkr2,
                               row(q_norm_g[l]), wuq2, row(kv_norm_g[l]), wuk, wuv, vones)

        eye = jnp.eye(LRU_BLOCKS, dtype=F32)
        blockdiag = lambda w: jnp.einsum("hij,hg->higj", w, eye).reshape(lw, lw)
        wg = jnp.concatenate([blockdiag(lru_wa[l]), blockdiag(lru_wx[l])], axis=1).astype(BF16)
        bg = jnp.concatenate([lru_ba[l], lru_bx[l]])[None, :]
        yl = _lru(lru_in, conv_w[l], row(conv_b[l]), wg, bg, row(lru_lambda[l]), row(lru_out_g[l]))

        ym = _attn(q, k, v, row(mla_out_g[l]))

        wo = w_out[l].astype(BF16)
        x1, h2t = _post(yl, ym, x, mod3, wo[:lw], wo[lw:], row(norm2_g[l]))

        wqt = (peer_wq[l].reshape(d, PEER_HEADS, 2, PEER_HALF).transpose(2, 1, 3, 0)
               .reshape(2 * PEER_HEADS * PEER_HALF, d).astype(BF16))
        keys = peer_keys[l].astype(BF16)
        kbig = jnp.einsum("hkd,hg->khgd", keys[:, 0], jnp.eye(PEER_HEADS, dtype=BF16)).reshape(
            PEER_KEYS * PEER_HEADS, PEER_HEADS * PEER_HALF)
        cnt, w1, packed = _route(h2t, wqt, kbig, keys[:, 1])

        ne = peer_u.shape[1] // EB
        vt3 = peer_v[l].astype(BF16).reshape(ne, EB, d).transpose(0, 2, 1)
        out = _dense(h2t, peer_u[l].astype(BF16), vt3, cnt, w1, packed,
                     x1.reshape(bsz * seq, d), mod3, row(final_g), seq)
        x = out.reshape(bsz, seq, d)
    return x
```

```python
import functools

import jax
import jax.numpy as jnp
from jax import lax
from jax.experimental import pallas as pl
from jax.experimental.pallas import tpu as pltpu

F32 = jnp.float32
BF16 = jnp.bfloat16

LRU_BLOCKS = 8
CONV_WIDTH = 4
LRU_C = 8.0
MLA_HEADS = 8
QK_NOPE = 64
QK_ROPE = 32
V_DIM = 64
ROPE_THETA = 10000.0
PEER_HEADS = 8
PEER_KEYS = 128
PEER_HALF = 128
PEER_TOPK = 16
EPS = 1e-6
LOG2_E = 1.4426950408889634

LANES = 128
SUBLANES = 8
VMEM_LIMIT = 56 * 1024 * 1024

TM = 512
TS = 512
TQ = 512
TN_ROUTE = 512
TN = 512
EB = 1024
MXU_COLS = 256
DOT_ROWS = 128
GATE_ROWS = 2
DENSE_SECTIONS = 1


def _cparams(*sem, flags=None):
    return pltpu.CompilerParams(dimension_semantics=sem, vmem_limit_bytes=VMEM_LIMIT, flags=flags)


def _rms(x, g):
    return x * lax.rsqrt(jnp.mean(x * x, axis=-1, keepdims=True) + EPS) * g


def _dot(a, b):
    return jnp.dot(a, b, preferred_element_type=F32)


def _ada_kernel(c_ref, w_ref, b_ref, o_ref):
    ca = jax.nn.silu(c_ref[...])
    o_ref[...] = jnp.dot(ca, w_ref[...], preferred_element_type=F32,
                         precision=lax.Precision.HIGHEST) + b_ref[...]


def _ada(c, w, b):
    bsz, d = c.shape
    n = w.shape[1]
    return pl.pallas_call(
        _ada_kernel,
        grid=(n // d,),
        in_specs=[pl.BlockSpec((bsz, d), lambda j: (0, 0)),
                  pl.BlockSpec((d, d), lambda j: (0, j)),
                  pl.BlockSpec((1, d), lambda j: (0, j))],
        out_specs=pl.BlockSpec((bsz, d), lambda j: (0, j)),
        out_shape=jax.ShapeDtypeStruct((bsz, n), F32),
        compiler_params=_cparams("arbitrary"),
        name="ada",
    )(c, w, b.reshape(1, n))


def _pre_kernel(x_ref, mod_ref, g1_ref, pos_ref, invf_ref, wlru_ref, wq_ref, wkv_ref, wkr_ref,
                qg_ref, wuq_ref, kvg_ref, wuk_ref, wuv_ref, vones_ref, lru_ref, q_ref, k_ref, v_ref):
    d = x_ref.shape[-1]
    x = x_ref[0]
    shift = mod_ref[0, 0:1, :]
    scale = mod_ref[0, 1:2, :]
    hb = (_rms(x, g1_ref[...]) * (1.0 + scale) + shift).astype(BF16)
    lru_ref[0] = _dot(hb, wlru_ref[...])

    ang = pos_ref[0].astype(F32) * invf_ref[...]
    cos = jnp.cos(ang)
    sin = jnp.sin(ang)

    qn = _rms(_dot(hb, wq_ref[...]), qg_ref[...]).astype(BF16)
    q2 = _dot(qn, wuq_ref[...])
    q_scale = (QK_NOPE + QK_ROPE) ** -0.5 * LOG2_E
    for h in range(MLA_HEADS):
        lo = h * LANES
        q_ref[0, :, lo:lo + LANES] = ((q2[:, lo:lo + LANES] * cos
                                       + q2[:, d + lo:d + lo + LANES] * sin) * q_scale).astype(BF16)

    kvn = _rms(_dot(hb, wkv_ref[...]), kvg_ref[...]).astype(BF16)
    kn = _dot(kvn, wuk_ref[...])
    v_ref[0] = (_dot(kvn, wuv_ref[...]) + vones_ref[...]).astype(BF16)
    kr2 = _dot(hb, wkr_ref[...])
    krot = kr2[:, :LANES] * cos + kr2[:, LANES:] * sin
    for h in range(MLA_HEADS):
        lo = h * LANES
        k_ref[0, :, lo:lo + LANES] = (kn[:, lo:lo + LANES] + krot).astype(BF16)


def _pre(x, mod3, g1, pos3, invf, wlru, wq, wkv, wkr2, qg, wuq2, kvg, wuk, wuv, vones):
    bsz, s, d = x.shape
    full = lambda a: pl.BlockSpec(a.shape, lambda b, i: (0,) * a.ndim)
    tok = lambda w: pl.BlockSpec((1, TM, w), lambda b, i: (b, i, 0))
    return pl.pallas_call(
        _pre_kernel,
        grid=(bsz, s // TM),
        in_specs=[tok(d), pl.BlockSpec((1, 6, d), lambda b, i: (b, 0, 0)), full(g1), tok(1), full(invf),
                  full(wlru), full(wq), full(wkv), full(wkr2), full(qg), full(wuq2), full(kvg),
                  full(wuk), full(wuv), full(vones)],
        out_specs=[tok(d), tok(d), tok(d), tok(d)],
        out_shape=[jax.ShapeDtypeStruct((bsz, s, d), F32),
                   jax.ShapeDtypeStruct((bsz, s, d), BF16),
                   jax.ShapeDtypeStruct((bsz, s, d), BF16),
                   jax.ShapeDtypeStruct((bsz, s, d), BF16)],
        compiler_params=_cparams("arbitrary", "arbitrary"),
        name="pre",
    )(x, mod3, g1, pos3, invf, wlru, wq, wkv, wkr2, qg, wuq2, kvg, wuk, wuv, vones)


def _lru_kernel(x_ref, cw_ref, cb_ref, wg_ref, bg_ref, lam_ref, og_ref, o_ref, prev_ref, h_ref):
    w = o_ref.shape[-1]

    @pl.when(pl.program_id(1) == 0)
    def _():
        prev_ref[...] = jnp.zeros_like(prev_ref)
        h_ref[...] = jnp.zeros_like(h_ref)

    x = x_ref[0, :, :w]
    gate = x_ref[0, :, w:]
    tail = prev_ref[...]
    row = lax.broadcasted_iota(jnp.int32, x.shape, 0)
    row8 = lax.broadcasted_iota(jnp.int32, tail.shape, 0)

    xc = cb_ref[...]
    for k in range(CONV_WIDTH - 1):
        back = CONV_WIDTH - 1 - k
        shifted = pltpu.roll(x, back, 0)
        first = jnp.where(row8 < back, pltpu.roll(tail, back, 0), shifted[:SUBLANES, :])
        shifted = jnp.concatenate([first, shifted[SUBLANES:, :]], axis=0)
        xc = xc + cw_ref[k:k + 1, :] * shifted
    xc = xc + cw_ref[CONV_WIDTH - 1:CONV_WIDTH, :] * x
    prev_ref[...] = x[TS - SUBLANES:, :]

    pre = _dot(xc.astype(BF16), wg_ref[...]) + bg_ref[...]
    r = jax.nn.sigmoid(pre[:, :w])
    i = jax.nn.sigmoid(pre[:, w:])
    log_a = -LRU_C * r * jax.nn.softplus(-lam_ref[...])
    a = jnp.exp(log_a)
    one_minus = 1.0 - a * a
    root = jnp.where(one_minus > 0.0, one_minus * lax.rsqrt(one_minus), 0.0)
    b = root * (i * xc)

    sub = row % SUBLANES
    dist = 1
    while dist < SUBLANES:
        keep = sub >= dist
        b = jnp.where(keep, a * pltpu.roll(b, dist, 0) + b, b)
        a = jnp.where(keep, a * pltpu.roll(a, dist, 0), a)
        dist *= 2
    state = h_ref[0:1, :]
    groups = []
    for g0 in range(0, TS, SUBLANES):
        hg = b[g0:g0 + SUBLANES, :] + a[g0:g0 + SUBLANES, :] * state
        groups.append(hg)
        state = hg[SUBLANES - 1:SUBLANES, :]
    h = jnp.concatenate(groups, axis=0)
    h_ref[...] = jnp.broadcast_to(state, h_ref.shape)

    y = jax.nn.gelu(gate) * h
    o_ref[0] = _rms(y, og_ref[...]).astype(BF16)


def _lru(lru_in, cw, cb, wg, bg, lam, og):
    bsz, s, d = lru_in.shape
    w = d // 2
    full = lambda a: pl.BlockSpec(a.shape, lambda b, t: (0,) * a.ndim)
    return pl.pallas_call(
        _lru_kernel,
        grid=(bsz, s // TS),
        in_specs=[pl.BlockSpec((1, TS, d), lambda b, t: (b, t, 0)),
                  full(cw), full(cb), full(wg), full(bg), full(lam), full(og)],
        out_specs=pl.BlockSpec((1, TS, w), lambda b, t: (b, t, 0)),
        out_shape=jax.ShapeDtypeStruct((bsz, s, w), BF16),
        scratch_shapes=[pltpu.VMEM((SUBLANES, w), F32), pltpu.VMEM((SUBLANES, w), F32)],
        compiler_params=_cparams("arbitrary", "arbitrary"),
        name="lru",
    )(lru_in, cw, cb, wg, bg, lam, og)


def _attn_kernel(q_ref, k_ref, v_ref, g_ref, o_ref, m_ref, acc_ref, bias_ref):
    i = pl.program_id(1)
    nt = (((1,), (1,)), ((), ()))
    row = lax.broadcasted_iota(jnp.int32, (TQ, TQ), 0)
    col = lax.broadcasted_iota(jnp.int32, (TQ, TQ), 1)
    lane = lax.broadcasted_iota(jnp.int32, (TQ, LANES), 1)

    m_ref[...] = jnp.full(m_ref.shape, -jnp.inf, F32)
    acc_ref[...] = jnp.zeros(acc_ref.shape, F32)
    bias_ref[...] = jnp.where(col <= row, 0.0, -jnp.inf)

    def block(j, masked):
        start = pl.multiple_of(j * TQ, TQ)

        def scores(h):
            lo = h * LANES
            s = lax.dot_general(q_ref[0, :, lo:lo + LANES], k_ref[0, pl.ds(start, TQ), lo:lo + LANES], nt,
                                preferred_element_type=F32)
            return s + bias_ref[...] if masked else s

        def softmax_step(h, s):
            m_old = m_ref[h]
            m_new = jnp.maximum(m_old, jnp.max(s, axis=-1, keepdims=True))
            m_ref[h] = m_new
            alpha = jnp.exp2(m_old - m_new)
            p = [jnp.exp2(s[:, c:c + LANES] - m_new) for c in range(0, TQ, LANES)]
            return alpha, jnp.concatenate(p, axis=-1).astype(BF16)

        def weighted_values(h, alpha, p):
            lo = h * LANES
            acc_ref[h] = alpha * acc_ref[h] + _dot(p, v_ref[0, pl.ds(start, TQ), lo:lo + LANES])

        pending_s, pending_p = {}, {}
        for t in range(MLA_HEADS + 2):
            if t < MLA_HEADS:
                pending_s[t] = scores(t)
            if 1 <= t <= MLA_HEADS:
                pending_p[t - 1] = softmax_step(t - 1, pending_s.pop(t - 1))
            if t >= 2:
                weighted_values(t - 2, *pending_p.pop(t - 2))

    def unmasked(j, carry):
        block(j, False)
        return carry

    def diagonal(j, carry):
        block(j, True)
        return carry

    lax.fori_loop(0, i, unmasked, 0)
    lax.fori_loop(i, i + 1, diagonal, 0)

    pairs = []
    for hp in range(MLA_HEADS // 2):
        even, odd = acc_ref[2 * hp], acc_ref[2 * hp + 1]
        num = jnp.where(lane < V_DIM, even, odd)
        den = jnp.where(lane < V_DIM, pltpu.roll(even, V_DIM, 1), pltpu.roll(odd, V_DIM, 1))
        pairs.append(num / den)
    y = jnp.concatenate(pairs, axis=-1)
    o_ref[0] = _rms(y, g_ref[...]).astype(BF16)


def _attn(q, k, v, g):
    bsz, s, d = q.shape
    w = g.shape[-1]
    return pl.pallas_call(
        _attn_kernel,
        grid=(bsz, s // TQ),
        in_specs=[pl.BlockSpec((1, TQ, d), lambda b, i: (b, i, 0)),
                  pl.BlockSpec((1, s, d), lambda b, i: (b, 0, 0)),
                  pl.BlockSpec((1, s, d), lambda b, i: (b, 0, 0)),
                  pl.BlockSpec(g.shape, lambda b, i: (0, 0))],
        out_specs=pl.BlockSpec((1, TQ, w), lambda b, i: (b, i, 0)),
        out_shape=jax.ShapeDtypeStruct((bsz, s, w), BF16),
        scratch_shapes=[pltpu.VMEM((MLA_HEADS, TQ, LANES), F32), pltpu.VMEM((MLA_HEADS, TQ, LANES), F32),
                        pltpu.VMEM((TQ, TQ), F32)],
        compiler_params=_cparams("arbitrary", "arbitrary"),
        name="attn",
    )(q, k, v, g)


def _post_kernel(yl_ref, ym_ref, x_ref, mod_ref, wol_ref, wom_ref, g2_ref, x1_ref, h2t_ref):
    mix = _dot(yl_ref[0], wol_ref[...]) + _dot(ym_ref[0], wom_ref[...])
    x1 = x_ref[0] + mod_ref[0, 2:3, :] * mix
    x1_ref[0] = x1
    h2 = _rms(x1, g2_ref[...]) * (1.0 + mod_ref[0, 4:5, :]) + mod_ref[0, 3:4, :]
    h2t_ref[...] = h2.T.astype(BF16)


def _post(yl, ym, x, mod3, wol, wom, g2):
    bsz, s, d = x.shape
    w = yl.shape[-1]
    nt = s // TM
    full = lambda a: pl.BlockSpec(a.shape, lambda b, i: (0,) * a.ndim)
    tok = lambda c: pl.BlockSpec((1, TM, c), lambda b, i: (b, i, 0))
    return pl.pallas_call(
        _post_kernel,
        grid=(bsz, nt),
        in_specs=[tok(w), tok(w), tok(d), pl.BlockSpec((1, 6, d), lambda b, i: (b, 0, 0)),
                  full(wol), full(wom), full(g2)],
        out_specs=[tok(d), pl.BlockSpec((d, TM), lambda b, i: (0, b * nt + i))],
        out_shape=[jax.ShapeDtypeStruct((bsz, s, d), F32),
                   jax.ShapeDtypeStruct((d, bsz * s), BF16)],
        compiler_params=_cparams("arbitrary", "arbitrary"),
        name="post",
    )(yl, ym, x, mod3, wol, wom, g2)


def _sort_pairs(n):
    pairs = []

    def merge(lo, hi, r):
        step = r * 2
        if step < hi - lo:
            merge(lo, hi, step)
            merge(lo + r, hi, step)
            pairs.extend((i, i + r) for i in range(lo + r, hi - r, step))
        else:
            pairs.append((lo, lo + r))

    def sort(lo, hi):
        if hi - lo >= 1:
            mid = lo + (hi - lo) // 2
            sort(lo, mid)
            sort(mid + 1, hi)
            merge(lo, hi, 1)

    sort(0, n - 1)
    return pairs


def _sort_desc(xs):
    xs = list(xs)
    for i, j in _sort_pairs(len(xs)):
        a, b = xs[i], xs[j]
        xs[i], xs[j] = jnp.maximum(a, b), jnp.minimum(a, b)
    return xs


def _merge_top(a, b):
    n = len(a)
    xs = [jnp.maximum(a[i], b[n - 1 - i]) for i in range(n)]
    dist = n // 2
    while dist >= 1:
        for i in range(n):
            if i & dist == 0:
                p, q = xs[i], xs[i + dist]
                xs[i], xs[i + dist] = jnp.maximum(p, q), jnp.minimum(p, q)
        dist //= 2
    return xs


def _kth_largest(vals, k):
    n = 1
    while n < len(vals):
        n *= 2
    present = [True] * len(vals) + [False] * (n - len(vals))
    ops = []
    for i, j in _sort_pairs(n):
        if present[i] and present[j]:
            ops.append(("cmp", i, j))
        elif present[j]:
            ops.append(("mov", i, j))
            present[i], present[j] = True, False
    need = {k}
    live = []
    for op in reversed(ops):
        kind, i, j = op
        if kind == "mov":
            if i in need:
                need.discard(i)
                need.add(j)
                live.append((kind, i, j, True, False))
        else:
            hi, lo = i in need, j in need
            if hi or lo:
                need.update((i, j))
                live.append((kind, i, j, hi, lo))
    xs = list(vals) + [None] * (n - len(vals))
    for kind, i, j, hi, lo in reversed(live):
        a, b = xs[i], xs[j]
        if kind == "mov":
            xs[i], xs[j] = b, None
        else:
            xs[i] = jnp.maximum(a, b) if hi else None
            xs[j] = jnp.minimum(a, b) if lo else None
    return xs[k]


def _top_values(slabs, n):
    groups = [_sort_desc(slabs[g:g + n]) for g in range(0, len(slabs), n)]
    while len(groups) > 1:
        groups = [_merge_top(groups[g], groups[g + 1]) for g in range(0, len(groups), 2)]
    return groups[0]


PACK_ROWS = 2 * SUBLANES


def _pair_bits(x):
    bits = pltpu.bitcast(x.astype(BF16).astype(F32), jnp.uint32)
    return (bits & jnp.uint32(0xFFFF0000)) | (bits >> 16)


def _route_kernel(h2t_ref, wqt_ref, kbig_ref, keys2_ref, c_ref, w1_ref, p_ref):
    tn = h2t_ref.shape[-1]
    nk = PEER_KEYS
    half_rows = PEER_HEADS * PEER_HALF
    qt = _dot(wqt_ref[...], h2t_ref[...])
    sub = lax.broadcasted_iota(jnp.int32, (SUBLANES, tn), 0)

    sc1 = _dot(kbig_ref[...], qt[:half_rows, :].astype(BF16))
    slabs = [sc1[PEER_HEADS * k:PEER_HEADS * (k + 1), :] for k in range(nk)]
    a = _top_values(slabs, PEER_TOPK)

    b = [None] * PEER_TOPK
    for h in range(PEER_HEADS):
        r0 = half_rows + h * PEER_HALF
        sc = _dot(keys2_ref[h], qt[r0:r0 + PEER_HALF, :].astype(BF16))
        tiles = [sc[SUBLANES * t:SUBLANES * (t + 1), :] for t in range(nk // SUBLANES)]
        xs = _sort_desc(tiles)
        shift = SUBLANES // 2
        while shift >= 1:
            xs = _merge_top(xs, [pltpu.roll(v, shift, 0) for v in xs])
            shift //= 2
        for ib in range(nk // PACK_ROWS):
            ranks, weights = [], []
            for tile in tiles[2 * ib:2 * ib + 2]:
                rk = jnp.full_like(tile, float(PEER_TOPK))
                for r in reversed(range(PEER_TOPK)):
                    rk = jnp.where(tile >= xs[r], float(r), rk)
                ranks.append(rk)
                weights.append(jnp.exp(tile - xs[0]))
            rk16 = jnp.concatenate(ranks, axis=0).astype(BF16)
            w16 = jnp.concatenate(weights, axis=0).astype(BF16)
            for lb in range(tn // LANES):
                r0 = 2 * h * PACK_ROWS
                p_ref[lb, ib, r0:r0 + PACK_ROWS, :] = rk16[:, lb * LANES:(lb + 1) * LANES]
                p_ref[lb, ib, r0 + PACK_ROWS:r0 + 2 * PACK_ROWS, :] = w16[:, lb * LANES:(lb + 1) * LANES]
        for r in range(PEER_TOPK):
            b[r] = xs[r] if h == 0 else jnp.where(sub == h, xs[r], b[r])

    pairs = [(i, j) for i in range(PEER_TOPK) for j in range(PEER_TOPK) if (i + 1) * (j + 1) <= PEER_TOPK]
    cands = {ij: a[ij[0]] + b[ij[1]] for ij in pairs}
    tau = _kth_largest(list(cands.values()), PEER_TOPK - 1)
    top = a[0] + b[0]
    z = jnp.zeros_like(tau)
    for c in cands.values():
        z = z + jnp.where(c >= tau, jnp.exp(c - top), 0.0)
    inv_z = 1.0 / z
    need = []
    for j in range(PEER_TOPK):
        t = None
        for i in range(PEER_TOPK):
            if (i, j) in cands:
                v = jnp.where(cands[(i, j)] >= tau, a[i], jnp.inf)
                t = v if t is None else jnp.minimum(t, v)
        need.append(t)
    for k in range(nk):
        cnt = jnp.zeros_like(tau)
        for j in range(PEER_TOPK):
            cnt = jnp.where(slabs[k] >= need[j], float(j + 1), cnt)
        c_ref[k] = _pair_bits(cnt)
        w1_ref[k] = _pair_bits(jnp.exp(slabs[k] - a[0]) * inv_z)


def _route(h2t, wqt, kbig, keys2):
    d, t = h2t.shape
    tn = TN_ROUTE
    by_key = pl.BlockSpec((PEER_KEYS, PEER_HEADS, tn), lambda i: (0, 0, i))
    packed = (PEER_KEYS // PACK_ROWS, 2 * PEER_HEADS * PACK_ROWS, LANES)
    return pl.pallas_call(
        _route_kernel,
        grid=(t // tn,),
        in_specs=[pl.BlockSpec((d, tn), lambda i: (0, i)),
                  pl.BlockSpec(wqt.shape, lambda i: (0, 0)),
                  pl.BlockSpec(kbig.shape, lambda i: (0, 0)),
                  pl.BlockSpec(keys2.shape, lambda i: (0, 0, 0))],
        out_specs=[by_key, by_key, pl.BlockSpec((tn // LANES,) + packed, lambda i: (i, 0, 0, 0))],
        out_shape=[jax.ShapeDtypeStruct((PEER_KEYS, PEER_HEADS, t), jnp.uint32),
                   jax.ShapeDtypeStruct((PEER_KEYS, PEER_HEADS, t), jnp.uint32),
                   jax.ShapeDtypeStruct((t // LANES,) + packed, BF16)],
        compiler_params=_cparams("arbitrary"),
        name="route",
    )(h2t, wqt, kbig, keys2)


def _dense_kernel(flag_ref, h2t_ref, u_ref, vt_ref, c_ref, w1_ref, p_ref, x1_ref, mod_ref,
                  fg_ref, o_ref, acc_ref, act_ref, gat_ref, *, nblk):
    s = pl.program_id(0)
    n_items = pl.num_programs(0) - 2
    tn = h2t_ref.shape[-1]
    eb = u_ref.shape[0]
    rows_per_step = eb // PEER_KEYS

    @pl.when(s == 0)
    def _():
        act_ref[1] = jnp.zeros(act_ref.shape[1:], F32)
        gat_ref[0] = jnp.zeros(gat_ref.shape[1:], BF16)

    jb = jnp.clip(s - 1, 0, n_items - 1) % nblk
    jc = jnp.clip(s - 2, 0, n_items - 1) % nblk

    @pl.when(jc == 0)
    def _():
        acc_ref[...] = jnp.zeros_like(acc_ref)

    n_lb = tn // LANES

    def row_tile(ref, i1, h, ls):
        return pltpu.bitcast(jnp.broadcast_to(ref[i1, h:h + 1, ls], (SUBLANES, LANES)), BF16)

    def gate_group(ig, lb, oth):
        rows = range(ig * GATE_ROWS, (ig + 1) * GATE_ROWS)
        ls = slice(lb * LANES, (lb + 1) * LANES)
        cntb = [[row_tile(c_ref, jb * rows_per_step + il, h, ls) for h in range(PEER_HEADS)] for il in rows]
        w1b = [[row_tile(w1_ref, jb * rows_per_step + il, h, ls) for h in range(PEER_HEADS)] for il in rows]
        for ib in range(PEER_KEYS // PACK_ROWS):
            g = [None] * GATE_ROWS
            for h in range(PEER_HEADS):
                r0 = 2 * h * PACK_ROWS
                rank2 = p_ref[lb, ib, r0:r0 + PACK_ROWS, :]
                w2 = p_ref[lb, ib, r0 + PACK_ROWS:r0 + 2 * PACK_ROWS, :]
                for k in range(GATE_ROWS):
                    term = jnp.minimum(jnp.maximum(cntb[k][h] - rank2, 0.0), w2) * w1b[k][h]
                    g[k] = term if g[k] is None else g[k] + term
            for k, il in enumerate(rows):
                r1 = il * PEER_KEYS + ib * PACK_ROWS
                act = act_ref[oth, r1:r1 + PACK_ROWS, ls]
                gat_ref[oth, r1:r1 + PACK_ROWS, ls] = jax.nn.gelu(act.astype(BF16)) * g[k]

    def matmul_chunks(cur):
        chunks = []
        for c0 in range(0, tn, MXU_COLS):
            cs = slice(c0, c0 + MXU_COLS)
            for r0 in range(0, acc_ref.shape[0], DOT_ROWS):
                rs = slice(r0, r0 + DOT_ROWS)

                def second(rs=rs, cs=cs):
                    acc_ref[rs, cs] += _dot(vt_ref[0, rs, :], gat_ref[cur, :, cs])
                chunks.append(second)
        for c0 in range(0, tn, MXU_COLS):
            cs = slice(c0, c0 + MXU_COLS)
            for r0 in range(0, eb, DOT_ROWS):
                rs = slice(r0, r0 + DOT_ROWS)

                def first(rs=rs, cs=cs):
                    act_ref[cur, rs, cs] = _dot(u_ref[rs, :], h2t_ref[:, cs])
                chunks.append(first)
        return chunks

    def sections(cur, oth):
        groups = [(ig, lb) for ig in range(rows_per_step // GATE_ROWS) for lb in range(n_lb)]
        chunks = matmul_chunks(cur)
        per_section = len(groups) // DENSE_SECTIONS
        for sec in range(DENSE_SECTIONS):
            @pl.when(flag_ref[0] == 0)
            def _(sec=sec):
                for gi in range(sec * per_section, (sec + 1) * per_section):
                    for ci in range(gi * len(chunks) // len(groups), (gi + 1) * len(chunks) // len(groups)):
                        chunks[ci]()
                    gate_group(*groups[gi], oth)

    for parity in range(2):
        @pl.when(s % 2 == parity)
        def _(parity=parity):
            sections(parity, 1 - parity)

    @pl.when(jnp.logical_and(jc == nblk - 1, s >= 2))
    def _():
        x2 = x1_ref[...] + mod_ref[0, 5:6, :] * acc_ref[...].T
        o_ref[...] = _rms(x2, fg_ref[...])


def _dense(h2t, u, vt3, cnt, w1, packed, x1, mod3, fg, seq):
    d, t = h2t.shape
    ne = vt3.shape[0]
    per_seq = seq // TN
    n_items = (t // TN) * ne
    item = lambda s, lag: jnp.clip(s - lag, 0, n_items - 1)
    tile = lambda s, lag: item(s, lag) // ne
    block = lambda s, lag: item(s, lag) % ne
    by_key = pl.BlockSpec((PEER_KEYS, PEER_HEADS, TN), lambda s: (0, 0, tile(s, 1)))
    return pl.pallas_call(
        functools.partial(_dense_kernel, nblk=ne),
        grid=(n_items + 2,),
        in_specs=[pl.BlockSpec(memory_space=pltpu.SMEM),
                  pl.BlockSpec((d, TN), lambda s: (0, tile(s, 0))),
                  pl.BlockSpec((EB, d), lambda s: (block(s, 0), 0)),
                  pl.BlockSpec((1, d, EB), lambda s: (block(s, 2), 0, 0)),
                  by_key, by_key,
                  pl.BlockSpec((TN // LANES,) + packed.shape[1:], lambda s: (tile(s, 1), 0, 0, 0)),
                  pl.BlockSpec((TN, d), lambda s: (tile(s, 2), 0)),
                  pl.BlockSpec((1, 6, d), lambda s: (tile(s, 2) // per_seq, 0, 0)),
                  pl.BlockSpec(fg.shape, lambda s: (0, 0))],
        out_specs=pl.BlockSpec((TN, d), lambda s: (tile(s, 2), 0)),
        out_shape=jax.ShapeDtypeStruct((t, d), F32),
        scratch_shapes=[pltpu.VMEM((d, TN), F32), pltpu.VMEM((2, EB, TN), F32),
                        pltpu.VMEM((2, EB, TN), BF16)],
        compiler_params=_cparams("arbitrary"),
        name="dense",
    )(jnp.zeros((1,), jnp.int32), h2t, u, vt3, cnt, w1, packed, x1, mod3, fg)


def _rot_half_cols(w):
    half = w.shape[-1] // 2
    return jnp.concatenate([-w[..., half:], w[..., :half]], axis=-1)


def _pad_cols(w, before, total):
    return jnp.pad(w, [(0, 0)] * (w.ndim - 1) + [(before, total - before - w.shape[-1])])


def kernel(x, c, positions, w_ada, b_ada, norm1_g, w_in, conv_w, conv_b, lru_wa, lru_ba, lru_wx, lru_bx,
           lru_lambda, q_norm_g, w_uq, kv_norm_g, w_ukv, lru_out_g, mla_out_g, w_out, norm2_g, peer_wq,
           peer_keys, peer_u, peer_v, final_g):
    bsz, seq, d = x.shape
    depth = w_ada.shape[0]
    lw = d // 2
    qr = w_uq.shape[1]
    kvr = w_ukv.shape[1]
    assert depth == 1, "the final norm is fused into the (single) layer's PEER kernel"
    assert seq % TM == 0 and seq % TN == 0 and seq % TQ == 0 and seq % TS == 0
    assert d == MLA_HEADS * LANES and lw == MLA_HEADS * V_DIM

    inv_freq = 1.0 / (ROPE_THETA ** (jnp.arange(0, QK_ROPE, 2, dtype=F32) / QK_ROPE))
    invf = _pad_cols(jnp.concatenate([inv_freq, inv_freq])[None, :], QK_NOPE, LANES)
    pos3 = positions.reshape(bsz, seq, 1)
    row = lambda v: v.reshape(1, -1)

    for l in range(depth):
        mod3 = _ada(c, w_ada[l], b_ada[l]).reshape(bsz, 6, d)

        wi = w_in[l]
        o0, o1, o2, o3 = 2 * lw, 2 * lw + qr, 2 * lw + qr + kvr, 2 * lw + qr + kvr + QK_ROPE
        wkr = wi[:, o2:o3]
        wkr2 = jnp.concatenate([_pad_cols(wkr, QK_NOPE, LANES),
                                _pad_cols(_rot_half_cols(wkr), QK_NOPE, LANES)], axis=1).astype(BF16)
        uq = w_uq[l].reshape(qr, MLA_HEADS, QK_NOPE + QK_ROPE)
        uq_main = _pad_cols(uq, 0, LANES).reshape(qr, d)
        uq_rot = _pad_cols(_rot_half_cols(uq[..., QK_NOPE:]), QK_NOPE, LANES).reshape(qr, d)
        wuq2 = jnp.concatenate([uq_main, uq_rot], axis=1).astype(BF16)
        ukv = w_ukv[l].reshape(kvr, MLA_HEADS, QK_NOPE + V_DIM)
        wuk = _pad_cols(ukv[..., :QK_NOPE], 0, LANES).reshape(kvr, d).astype(BF16)
        uv = ukv[..., QK_NOPE:]
        odd = (jnp.arange(MLA_HEADS) % 2 == 1)[None, :, None]
        wuv = jnp.where(odd, _pad_cols(uv, V_DIM, LANES), _pad_cols(uv, 0, LANES)).reshape(kvr, d).astype(BF16)
        half = jnp.arange(LANES)[None, :] >= V_DIM
        vones = (half != odd[0]).astype(F32).reshape(1, d)

        lru_in, q, k, v = _pre(x, mod3, row(norm1_g[l]), pos3, invf, wi[:, :o0].astype(BF16),
                               wi[:, o0:o1].astype(BF16), wi[:, o1:o2].astype(BF16), wkr2,
                               row(q_norm_g[l]), wuq2, row(kv_norm_g[l]), wuk, wuv, vones)

        eye = jnp.eye(LRU_BLOCKS, dtype=F32)
        blockdiag = lambda w: jnp.einsum("hij,hg->higj", w, eye).reshape(lw, lw)
        wg = jnp.concatenate([blockdiag(lru_wa[l]), blockdiag(lru_wx[l])], axis=1).astype(BF16)
        bg = jnp.concatenate([lru_ba[l], lru_bx[l]])[None, :]
        yl = _lru(lru_in, conv_w[l], row(conv_b[l]), wg, bg, row(lru_lambda[l]), row(lru_out_g[l]))

        ym = _attn(q, k, v, row(mla_out_g[l]))

        wo = w_out[l].astype(BF16)
        x1, h2t = _post(yl, ym, x, mod3, wo[:lw], wo[lw:], row(norm2_g[l]))

        wqt = (peer_wq[l].reshape(d, PEER_HEADS, 2, PEER_HALF).transpose(2, 1, 3, 0)
               .reshape(2 * PEER_HEADS * PEER_HALF, d).astype(BF16))
        keys = peer_keys[l].astype(BF16)
        kbig = jnp.einsum("hkd,hg->khgd", keys[:, 0], jnp.eye(PEER_HEADS, dtype=BF16)).reshape(
            PEER_KEYS * PEER_HEADS, PEER_HEADS * PEER_HALF)
        cnt, w1, packed = _route(h2t, wqt, kbig, keys[:, 1])

        ne = peer_u.shape[1] // EB
        vt3 = peer_v[l].astype(BF16).reshape(ne, EB, d).transpose(0, 2, 1)
        out = _dense(h2t, peer_u[l].astype(BF16), vt3, cnt, w1, packed,
                     x1.reshape(bsz * seq, d), mod3, row(final_g), seq)
        x = out.reshape(bsz, seq, d)
    return x
```

```python
import functools

import jax
import jax.numpy as jnp
from jax import lax
from jax.experimental import pallas as pl
from jax.experimental.pallas import tpu as pltpu

F32 = jnp.float32
BF16 = jnp.bfloat16

LRU_BLOCKS = 8
CONV_WIDTH = 4
LRU_C = 8.0
MLA_HEADS = 8
QK_NOPE = 64
QK_ROPE = 32
V_DIM = 64
ROPE_THETA = 10000.0
PEER_HEADS = 8
PEER_KEYS = 128
PEER_HALF = 128
PEER_TOPK = 16
EPS = 1e-6
LOG2_E = 1.4426950408889634

LANES = 128
SUBLANES = 8
VMEM_LIMIT = 56 * 1024 * 1024

TM = 512
TS = 512
TQ = 512
TN_ROUTE = 512
TN = 512
EB = 1024
MXU_COLS = 512
DOT_ROWS = 1024
GATE_ROWS = 2
DENSE_SECTIONS = 1


def _cparams(*sem, flags=None):
    return pltpu.CompilerParams(dimension_semantics=sem, vmem_limit_bytes=VMEM_LIMIT, flags=flags)


def _rms(x, g):
    return x * lax.rsqrt(jnp.mean(x * x, axis=-1, keepdims=True) + EPS) * g


def _dot(a, b):
    return jnp.dot(a, b, preferred_element_type=F32)


def _ada_kernel(c_ref, w_ref, b_ref, o_ref):
    ca = jax.nn.silu(c_ref[...])
    o_ref[...] = jnp.dot(ca, w_ref[...], preferred_element_type=F32,
                         precision=lax.Precision.HIGHEST) + b_ref[...]


def _ada(c, w, b):
    bsz, d = c.shape
    n = w.shape[1]
    return pl.pallas_call(
        _ada_kernel,
        grid=(n // d,),
        in_specs=[pl.BlockSpec((bsz, d), lambda j: (0, 0)),
                  pl.BlockSpec((d, d), lambda j: (0, j)),
                  pl.BlockSpec((1, d), lambda j: (0, j))],
        out_specs=pl.BlockSpec((bsz, d), lambda j: (0, j)),
        out_shape=jax.ShapeDtypeStruct((bsz, n), F32),
        compiler_params=_cparams("arbitrary"),
        name="ada",
    )(c, w, b.reshape(1, n))


def _pre_kernel(x_ref, mod_ref, g1_ref, pos_ref, invf_ref, wlru_ref, wq_ref, wkv_ref, wkr_ref,
                qg_ref, wuq_ref, kvg_ref, wuk_ref, wuv_ref, vones_ref, lru_ref, q_ref, k_ref, v_ref):
    d = x_ref.shape[-1]
    x = x_ref[0]
    shift = mod_ref[0, 0:1, :]
    scale = mod_ref[0, 1:2, :]
    hb = (_rms(x, g1_ref[...]) * (1.0 + scale) + shift).astype(BF16)
    lru_ref[0] = _dot(hb, wlru_ref[...])

    ang = pos_ref[0].astype(F32) * invf_ref[...]
    cos = jnp.cos(ang)
    sin = jnp.sin(ang)

    qn = _rms(_dot(hb, wq_ref[...]), qg_ref[...]).astype(BF16)
    q2 = _dot(qn, wuq_ref[...])
    q_scale = (QK_NOPE + QK_ROPE) ** -0.5 * LOG2_E
    for h in range(MLA_HEADS):
        lo = h * LANES
        q_ref[0, :, lo:lo + LANES] = ((q2[:, lo:lo + LANES] * cos
                                       + q2[:, d + lo:d + lo + LANES] * sin) * q_scale).astype(BF16)

    kvn = _rms(_dot(hb, wkv_ref[...]), kvg_ref[...]).astype(BF16)
    kn = _dot(kvn, wuk_ref[...])
    v_ref[0] = (_dot(kvn, wuv_ref[...]) + vones_ref[...]).astype(BF16)
    kr2 = _dot(hb, wkr_ref[...])
    krot = kr2[:, :LANES] * cos + kr2[:, LANES:] * sin
    for h in range(MLA_HEADS):
        lo = h * LANES
        k_ref[0, :, lo:lo + LANES] = (kn[:, lo:lo + LANES] + krot).astype(BF16)


def _pre(x, mod3, g1, pos3, invf, wlru, wq, wkv, wkr2, qg, wuq2, kvg, wuk, wuv, vones):
    bsz, s, d = x.shape
    full = lambda a: pl.BlockSpec(a.shape, lambda b, i: (0,) * a.ndim)
    tok = lambda w: pl.BlockSpec((1, TM, w), lambda b, i: (b, i, 0))
    return pl.pallas_call(
        _pre_kernel,
        grid=(bsz, s // TM),
        in_specs=[tok(d), pl.BlockSpec((1, 6, d), lambda b, i: (b, 0, 0)), full(g1), tok(1), full(invf),
                  full(wlru), full(wq), full(wkv), full(wkr2), full(qg), full(wuq2), full(kvg),
                  full(wuk), full(wuv), full(vones)],
        out_specs=[tok(d), tok(d), tok(d), tok(d)],
        out_shape=[jax.ShapeDtypeStruct((bsz, s, d), F32),
                   jax.ShapeDtypeStruct((bsz, s, d), BF16),
                   jax.ShapeDtypeStruct((bsz, s, d), BF16),
                   jax.ShapeDtypeStruct((bsz, s, d), BF16)],
        compiler_params=_cparams("arbitrary", "arbitrary"),
        name="pre",
    )(x, mod3, g1, pos3, invf, wlru, wq, wkv, wkr2, qg, wuq2, kvg, wuk, wuv, vones)


def _lru_kernel(x_ref, cw_ref, cb_ref, wg_ref, bg_ref, lam_ref, og_ref, o_ref, prev_ref, h_ref):
    w = o_ref.shape[-1]

    @pl.when(pl.program_id(1) == 0)
    def _():
        prev_ref[...] = jnp.zeros_like(prev_ref)
        h_ref[...] = jnp.zeros_like(h_ref)

    x = x_ref[0, :, :w]
    gate = x_ref[0, :, w:]
    tail = prev_ref[...]
    row = lax.broadcasted_iota(jnp.int32, x.shape, 0)
    row8 = lax.broadcasted_iota(jnp.int32, tail.shape, 0)

    xc = cb_ref[...]
    for k in range(CONV_WIDTH - 1):
        back = CONV_WIDTH - 1 - k
        shifted = pltpu.roll(x, back, 0)
        first = jnp.where(row8 < back, pltpu.roll(tail, back, 0), shifted[:SUBLANES, :])
        shifted = jnp.concatenate([first, shifted[SUBLANES:, :]], axis=0)
        xc = xc + cw_ref[k:k + 1, :] * shifted
    xc = xc + cw_ref[CONV_WIDTH - 1:CONV_WIDTH, :] * x
    prev_ref[...] = x[TS - SUBLANES:, :]

    pre = _dot(xc.astype(BF16), wg_ref[...]) + bg_ref[...]
    r = jax.nn.sigmoid(pre[:, :w])
    i = jax.nn.sigmoid(pre[:, w:])
    log_a = -LRU_C * r * jax.nn.softplus(-lam_ref[...])
    a = jnp.exp(log_a)
    one_minus = 1.0 - a * a
    root = jnp.where(one_minus > 0.0, one_minus * lax.rsqrt(one_minus), 0.0)
    b = root * (i * xc)

    sub = row % SUBLANES
    dist = 1
    while dist < SUBLANES:
        keep = sub >= dist
        b = jnp.where(keep, a * pltpu.roll(b, dist, 0) + b, b)
        a = jnp.where(keep, a * pltpu.roll(a, dist, 0), a)
        dist *= 2
    state = h_ref[0:1, :]
    groups = []
    for g0 in range(0, TS, SUBLANES):
        hg = b[g0:g0 + SUBLANES, :] + a[g0:g0 + SUBLANES, :] * state
        groups.append(hg)
        state = hg[SUBLANES - 1:SUBLANES, :]
    h = jnp.concatenate(groups, axis=0)
    h_ref[...] = jnp.broadcast_to(state, h_ref.shape)

    y = jax.nn.gelu(gate) * h
    o_ref[0] = _rms(y, og_ref[...]).astype(BF16)


def _lru(lru_in, cw, cb, wg, bg, lam, og):
    bsz, s, d = lru_in.shape
    w = d // 2
    full = lambda a: pl.BlockSpec(a.shape, lambda b, t: (0,) * a.ndim)
    return pl.pallas_call(
        _lru_kernel,
        grid=(bsz, s // TS),
        in_specs=[pl.BlockSpec((1, TS, d), lambda b, t: (b, t, 0)),
                  full(cw), full(cb), full(wg), full(bg), full(lam), full(og)],
        out_specs=pl.BlockSpec((1, TS, w), lambda b, t: (b, t, 0)),
        out_shape=jax.ShapeDtypeStruct((bsz, s, w), BF16),
        scratch_shapes=[pltpu.VMEM((SUBLANES, w), F32), pltpu.VMEM((SUBLANES, w), F32)],
        compiler_params=_cparams("arbitrary", "arbitrary"),
        name="lru",
    )(lru_in, cw, cb, wg, bg, lam, og)


def _attn_kernel(q_ref, k_ref, v_ref, g_ref, o_ref, m_ref, acc_ref, bias_ref):
    i = pl.program_id(1)
    nt = (((1,), (1,)), ((), ()))
    row = lax.broadcasted_iota(jnp.int32, (TQ, TQ), 0)
    col = lax.broadcasted_iota(jnp.int32, (TQ, TQ), 1)
    lane = lax.broadcasted_iota(jnp.int32, (TQ, LANES), 1)

    m_ref[...] = jnp.full(m_ref.shape, -jnp.inf, F32)
    acc_ref[...] = jnp.zeros(acc_ref.shape, F32)
    bias_ref[...] = jnp.where(col <= row, 0.0, -jnp.inf)

    def block(j, masked):
        start = pl.multiple_of(j * TQ, TQ)

        def scores(h):
            lo = h * LANES
            s = lax.dot_general(q_ref[0, :, lo:lo + LANES], k_ref[0, pl.ds(start, TQ), lo:lo + LANES], nt,
                                preferred_element_type=F32)
            return s + bias_ref[...] if masked else s

        def softmax_step(h, s):
            m_old = m_ref[h]
            m_new = jnp.maximum(m_old, jnp.max(s, axis=-1, keepdims=True))
            m_ref[h] = m_new
            alpha = jnp.exp2(m_old - m_new)
            p = [jnp.exp2(s[:, c:c + LANES] - m_new) for c in range(0, TQ, LANES)]
            return alpha, jnp.concatenate(p, axis=-1).astype(BF16)

        def weighted_values(h, alpha, p):
            lo = h * LANES
            acc_ref[h] = alpha * acc_ref[h] + _dot(p, v_ref[0, pl.ds(start, TQ), lo:lo + LANES])

        pending_s, pending_p = {}, {}
        for t in range(MLA_HEADS + 2):
            if t < MLA_HEADS:
                pending_s[t] = scores(t)
            if 1 <= t <= MLA_HEADS:
                pending_p[t - 1] = softmax_step(t - 1, pending_s.pop(t - 1))
            if t >= 2:
                weighted_values(t - 2, *pending_p.pop(t - 2))

    def unmasked(j, carry):
        block(j, False)
        return carry

    def diagonal(j, carry):
        block(j, True)
        return carry

    lax.fori_loop(0, i, unmasked, 0)
    lax.fori_loop(i, i + 1, diagonal, 0)

    pairs = []
    for hp in range(MLA_HEADS // 2):
        even, odd = acc_ref[2 * hp], acc_ref[2 * hp + 1]
        num = jnp.where(lane < V_DIM, even, odd)
        den = jnp.where(lane < V_DIM, pltpu.roll(even, V_DIM, 1), pltpu.roll(odd, V_DIM, 1))
        pairs.append(num / den)
    y = jnp.concatenate(pairs, axis=-1)
    o_ref[0] = _rms(y, g_ref[...]).astype(BF16)


def _attn(q, k, v, g):
    bsz, s, d = q.shape
    w = g.shape[-1]
    return pl.pallas_call(
        _attn_kernel,
        grid=(bsz, s // TQ),
        in_specs=[pl.BlockSpec((1, TQ, d), lambda b, i: (b, i, 0)),
                  pl.BlockSpec((1, s, d), lambda b, i: (b, 0, 0)),
                  pl.BlockSpec((1, s, d), lambda b, i: (b, 0, 0)),
                  pl.BlockSpec(g.shape, lambda b, i: (0, 0))],
        out_specs=pl.BlockSpec((1, TQ, w), lambda b, i: (b, i, 0)),
        out_shape=jax.ShapeDtypeStruct((bsz, s, w), BF16),
        scratch_shapes=[pltpu.VMEM((MLA_HEADS, TQ, LANES), F32), pltpu.VMEM((MLA_HEADS, TQ, LANES), F32),
                        pltpu.VMEM((TQ, TQ), F32)],
        compiler_params=_cparams("arbitrary", "arbitrary"),
        name="attn",
    )(q, k, v, g)


def _post_kernel(yl_ref, ym_ref, x_ref, mod_ref, wol_ref, wom_ref, g2_ref, x1_ref, h2t_ref):
    mix = _dot(yl_ref[0], wol_ref[...]) + _dot(ym_ref[0], wom_ref[...])
    x1 = x_ref[0] + mod_ref[0, 2:3, :] * mix
    x1_ref[0] = x1
    h2 = _rms(x1, g2_ref[...]) * (1.0 + mod_ref[0, 4:5, :]) + mod_ref[0, 3:4, :]
    h2t_ref[...] = h2.T.astype(BF16)


def _post(yl, ym, x, mod3, wol, wom, g2):
    bsz, s, d = x.shape
    w = yl.shape[-1]
    nt = s // TM
    full = lambda a: pl.BlockSpec(a.shape, lambda b, i: (0,) * a.ndim)
    tok = lambda c: pl.BlockSpec((1, TM, c), lambda b, i: (b, i, 0))
    return pl.pallas_call(
        _post_kernel,
        grid=(bsz, nt),
        in_specs=[tok(w), tok(w), tok(d), pl.BlockSpec((1, 6, d), lambda b, i: (b, 0, 0)),
                  full(wol), full(wom), full(g2)],
        out_specs=[tok(d), pl.BlockSpec((d, TM), lambda b, i: (0, b * nt + i))],
        out_shape=[jax.ShapeDtypeStruct((bsz, s, d), F32),
                   jax.ShapeDtypeStruct((d, bsz * s), BF16)],
        compiler_params=_cparams("arbitrary", "arbitrary"),
        name="post",
    )(yl, ym, x, mod3, wol, wom, g2)


def _sort_pairs(n):
    pairs = []

    def merge(lo, hi, r):
        step = r * 2
        if step < hi - lo:
            merge(lo, hi, step)
            merge(lo + r, hi, step)
            pairs.extend((i, i + r) for i in range(lo + r, hi - r, step))
        else:
            pairs.append((lo, lo + r))

    def sort(lo, hi):
        if hi - lo >= 1:
            mid = lo + (hi - lo) // 2
            sort(lo, mid)
            sort(mid + 1, hi)
            merge(lo, hi, 1)

    sort(0, n - 1)
    return pairs


def _sort_desc(xs):
    xs = list(xs)
    for i, j in _sort_pairs(len(xs)):
        a, b = xs[i], xs[j]
        xs[i], xs[j] = jnp.maximum(a, b), jnp.minimum(a, b)
    return xs


def _merge_top(a, b):
    n = len(a)
    xs = [jnp.maximum(a[i], b[n - 1 - i]) for i in range(n)]
    dist = n // 2
    while dist >= 1:
        for i in range(n):
            if i & dist == 0:
                p, q = xs[i], xs[i + dist]
                xs[i], xs[i + dist] = jnp.maximum(p, q), jnp.minimum(p, q)
        dist //= 2
    return xs


def _kth_largest(vals, k):
    n = 1
    while n < len(vals):
        n *= 2
    present = [True] * len(vals) + [False] * (n - len(vals))
    ops = []
    for i, j in _sort_pairs(n):
        if present[i] and present[j]:
            ops.append(("cmp", i, j))
        elif present[j]:
            ops.append(("mov", i, j))
            present[i], present[j] = True, False
    need = {k}
    live = []
    for op in reversed(ops):
        kind, i, j = op
        if kind == "mov":
            if i in need:
                need.discard(i)
                need.add(j)
                live.append((kind, i, j, True, False))
        else:
            hi, lo = i in need, j in need
            if hi or lo:
                need.update((i, j))
                live.append((kind, i, j, hi, lo))
    xs = list(vals) + [None] * (n - len(vals))
    for kind, i, j, hi, lo in reversed(live):
        a, b = xs[i], xs[j]
        if kind == "mov":
            xs[i], xs[j] = b, None
        else:
            xs[i] = jnp.maximum(a, b) if hi else None
            xs[j] = jnp.minimum(a, b) if lo else None
    return xs[k]


def _top_values(slabs, n):
    groups = [_sort_desc(slabs[g:g + n]) for g in range(0, len(slabs), n)]
    while len(groups) > 1:
        groups = [_merge_top(groups[g], groups[g + 1]) for g in range(0, len(groups), 2)]
    return groups[0]


PACK_ROWS = 2 * SUBLANES


def _pair_bits(x):
    bits = pltpu.bitcast(x.astype(BF16).astype(F32), jnp.uint32)
    return (bits & jnp.uint32(0xFFFF0000)) | (bits >> 16)


def _route_kernel(h2t_ref, wqt_ref, kbig_ref, keys2_ref, c_ref, w1_ref, p_ref):
    tn = h2t_ref.shape[-1]
    nk = PEER_KEYS
    half_rows = PEER_HEADS * PEER_HALF
    qt = _dot(wqt_ref[...], h2t_ref[...])
    sub = lax.broadcasted_iota(jnp.int32, (SUBLANES, tn), 0)

    sc1 = _dot(kbig_ref[...], qt[:half_rows, :].astype(BF16))
    slabs = [sc1[PEER_HEADS * k:PEER_HEADS * (k + 1), :] for k in range(nk)]
    a = _top_values(slabs, PEER_TOPK)

    b = [None] * PEER_TOPK
    for h in range(PEER_HEADS):
        r0 = half_rows + h * PEER_HALF
        sc = _dot(keys2_ref[h], qt[r0:r0 + PEER_HALF, :].astype(BF16))
        tiles = [sc[SUBLANES * t:SUBLANES * (t + 1), :] for t in range(nk // SUBLANES)]
        xs = _sort_desc(tiles)
        shift = SUBLANES // 2
        while shift >= 1:
            xs = _merge_top(xs, [pltpu.roll(v, shift, 0) for v in xs])
            shift //= 2
        for ib in range(nk // PACK_ROWS):
            ranks, weights = [], []
            for tile in tiles[2 * ib:2 * ib + 2]:
                rk = jnp.full_like(tile, float(PEER_TOPK))
                for r in reversed(range(PEER_TOPK)):
                    rk = jnp.where(tile >= xs[r], float(r), rk)
                ranks.append(rk)
                weights.append(jnp.exp(tile - xs[0]))
            rk16 = jnp.concatenate(ranks, axis=0).astype(BF16)
            w16 = jnp.concatenate(weights, axis=0).astype(BF16)
            for lb in range(tn // LANES):
                r0 = 2 * h * PACK_ROWS
                p_ref[lb, ib, r0:r0 + PACK_ROWS, :] = rk16[:, lb * LANES:(lb + 1) * LANES]
                p_ref[lb, ib, r0 + PACK_ROWS:r0 + 2 * PACK_ROWS, :] = w16[:, lb * LANES:(lb + 1) * LANES]
        for r in range(PEER_TOPK):
            b[r] = xs[r] if h == 0 else jnp.where(sub == h, xs[r], b[r])

    pairs = [(i, j) for i in range(PEER_TOPK) for j in range(PEER_TOPK) if (i + 1) * (j + 1) <= PEER_TOPK]
    cands = {ij: a[ij[0]] + b[ij[1]] for ij in pairs}
    tau = _kth_largest(list(cands.values()), PEER_TOPK - 1)
    top = a[0] + b[0]
    z = jnp.zeros_like(tau)
    for c in cands.values():
        z = z + jnp.where(c >= tau, jnp.exp(c - top), 0.0)
    inv_z = 1.0 / z
    need = []
    for j in range(PEER_TOPK):
        t = None
        for i in range(PEER_TOPK):
            if (i, j) in cands:
                v = jnp.where(cands[(i, j)] >= tau, a[i], jnp.inf)
                t = v if t is None else jnp.minimum(t, v)
        need.append(t)
    for k in range(nk):
        cnt = jnp.zeros_like(tau)
        for j in range(PEER_TOPK):
            cnt = jnp.where(slabs[k] >= need[j], float(j + 1), cnt)
        c_ref[k] = _pair_bits(cnt)
        w1_ref[k] = _pair_bits(jnp.exp(slabs[k] - a[0]) * inv_z)


def _route(h2t, wqt, kbig, keys2):
    d, t = h2t.shape
    tn = TN_ROUTE
    by_key = pl.BlockSpec((PEER_KEYS, PEER_HEADS, tn), lambda i: (0, 0, i))
    packed = (PEER_KEYS // PACK_ROWS, 2 * PEER_HEADS * PACK_ROWS, LANES)
    return pl.pallas_call(
        _route_kernel,
        grid=(t // tn,),
        in_specs=[pl.BlockSpec((d, tn), lambda i: (0, i)),
                  pl.BlockSpec(wqt.shape, lambda i: (0, 0)),
                  pl.BlockSpec(kbig.shape, lambda i: (0, 0)),
                  pl.BlockSpec(keys2.shape, lambda i: (0, 0, 0))],
        out_specs=[by_key, by_key, pl.BlockSpec((tn // LANES,) + packed, lambda i: (i, 0, 0, 0))],
        out_shape=[jax.ShapeDtypeStruct((PEER_KEYS, PEER_HEADS, t), jnp.uint32),
                   jax.ShapeDtypeStruct((PEER_KEYS, PEER_HEADS, t), jnp.uint32),
                   jax.ShapeDtypeStruct((t // LANES,) + packed, BF16)],
        compiler_params=_cparams("arbitrary"),
        name="route",
    )(h2t, wqt, kbig, keys2)


def _dense_kernel(flag_ref, h2t_ref, u_ref, vt_ref, c_ref, w1_ref, p_ref, x1_ref, mod_ref,
                  fg_ref, o_ref, acc_ref, act_ref, gat_ref, *, nblk):
    s = pl.program_id(0)
    n_items = pl.num_programs(0) - 2
    tn = h2t_ref.shape[-1]
    eb = u_ref.shape[0]
    rows_per_step = eb // PEER_KEYS

    @pl.when(s == 0)
    def _():
        act_ref[1] = jnp.zeros(act_ref.shape[1:], F32)
        gat_ref[0] = jnp.zeros(gat_ref.shape[1:], BF16)

    jb = jnp.clip(s - 1, 0, n_items - 1) % nblk
    jc = jnp.clip(s - 2, 0, n_items - 1) % nblk

    @pl.when(jc == 0)
    def _():
        acc_ref[...] = jnp.zeros_like(acc_ref)

    n_lb = tn // LANES

    def row_tile(ref, i1, h, ls):
        return pltpu.bitcast(jnp.broadcast_to(ref[i1, h:h + 1, ls], (SUBLANES, LANES)), BF16)

    def gate_group(ig, lb, oth):
        rows = range(ig * GATE_ROWS, (ig + 1) * GATE_ROWS)
        ls = slice(lb * LANES, (lb + 1) * LANES)
        cntb = [[row_tile(c_ref, jb * rows_per_step + il, h, ls) for h in range(PEER_HEADS)] for il in rows]
        w1b = [[row_tile(w1_ref, jb * rows_per_step + il, h, ls) for h in range(PEER_HEADS)] for il in rows]
        for ib in range(PEER_KEYS // PACK_ROWS):
            g = [None] * GATE_ROWS
            for h in range(PEER_HEADS):
                r0 = 2 * h * PACK_ROWS
                rank2 = p_ref[lb, ib, r0:r0 + PACK_ROWS, :]
                w2 = p_ref[lb, ib, r0 + PACK_ROWS:r0 + 2 * PACK_ROWS, :]
                for k in range(GATE_ROWS):
                    term = jnp.minimum(jnp.maximum(cntb[k][h] - rank2, 0.0), w2) * w1b[k][h]
                    g[k] = term if g[k] is None else g[k] + term
            for k, il in enumerate(rows):
                r1 = il * PEER_KEYS + ib * PACK_ROWS
                act = act_ref[oth, r1:r1 + PACK_ROWS, ls]
                gat_ref[oth, r1:r1 + PACK_ROWS, ls] = jax.nn.gelu(act.astype(BF16)) * g[k]

    def matmul_chunks(cur):
        chunks = []
        for c0 in range(0, tn, MXU_COLS):
            cs = slice(c0, c0 + MXU_COLS)
            for r0 in range(0, acc_ref.shape[0], DOT_ROWS):
                rs = slice(r0, r0 + DOT_ROWS)

                def second(rs=rs, cs=cs):
                    acc_ref[rs, cs] += _dot(vt_ref[0, rs, :], gat_ref[cur, :, cs])
                chunks.append(second)
        for c0 in range(0, tn, MXU_COLS):
            cs = slice(c0, c0 + MXU_COLS)
            for r0 in range(0, eb, DOT_ROWS):
                rs = slice(r0, r0 + DOT_ROWS)

                def first(rs=rs, cs=cs):
                    act_ref[cur, rs, cs] = _dot(u_ref[rs, :], h2t_ref[:, cs])
                chunks.append(first)
        return chunks

    def sections(cur, oth):
        groups = [(ig, lb) for ig in range(rows_per_step // GATE_ROWS) for lb in range(n_lb)]
        chunks = matmul_chunks(cur)
        per_section = len(groups) // DENSE_SECTIONS
        for sec in range(DENSE_SECTIONS):
            @pl.when(flag_ref[0] == 0)
            def _(sec=sec):
                for gi in range(sec * per_section, (sec + 1) * per_section):
                    for ci in range(gi * len(chunks) // len(groups), (gi + 1) * len(chunks) // len(groups)):
                        chunks[ci]()
                    gate_group(*groups[gi], oth)

    for parity in range(2):
        @pl.when(s % 2 == parity)
        def _(parity=parity):
            sections(parity, 1 - parity)

    @pl.when(jnp.logical_and(jc == nblk - 1, s >= 2))
    def _():
        x2 = x1_ref[...] + mod_ref[0, 5:6, :] * acc_ref[...].T
        o_ref[...] = _rms(x2, fg_ref[...])


def _dense(h2t, u, vt3, cnt, w1, packed, x1, mod3, fg, seq):
    d, t = h2t.shape
    ne = vt3.shape[0]
    per_seq = seq // TN
    n_items = (t // TN) * ne
    item = lambda s, lag: jnp.clip(s - lag, 0, n_items - 1)
    tile = lambda s, lag: item(s, lag) // ne
    block = lambda s, lag: item(s, lag) % ne
    by_key = pl.BlockSpec((PEER_KEYS, PEER_HEADS, TN), lambda s: (0, 0, tile(s, 1)))
    return pl.pallas_call(
        functools.partial(_dense_kernel, nblk=ne),
        grid=(n_items + 2,),
        in_specs=[pl.BlockSpec(memory_space=pltpu.SMEM),
                  pl.BlockSpec((d, TN), lambda s: (0, tile(s, 0))),
                  pl.BlockSpec((EB, d), lambda s: (block(s, 0), 0)),
                  pl.BlockSpec((1, d, EB), lambda s: (block(s, 2), 0, 0)),
                  by_key, by_key,
                  pl.BlockSpec((TN // LANES,) + packed.shape[1:], lambda s: (tile(s, 1), 0, 0, 0)),
                  pl.BlockSpec((TN, d), lambda s: (tile(s, 2), 0)),
                  pl.BlockSpec((1, 6, d), lambda s: (tile(s, 2) // per_seq, 0, 0)),
                  pl.BlockSpec(fg.shape, lambda s: (0, 0))],
        out_specs=pl.BlockSpec((TN, d), lambda s: (tile(s, 2), 0)),
        out_shape=jax.ShapeDtypeStruct((t, d), F32),
        scratch_shapes=[pltpu.VMEM((d, TN), F32), pltpu.VMEM((2, EB, TN), F32),
                        pltpu.VMEM((2, EB, TN), BF16)],
        compiler_params=_cparams("arbitrary"),
        name="dense",
    )(jnp.zeros((1,), jnp.int32), h2t, u, vt3, cnt, w1, packed, x1, mod3, fg)


def _rot_half_cols(w):
    half = w.shape[-1] // 2
    return jnp.concatenate([-w[..., half:], w[..., :half]], axis=-1)


def _pad_cols(w, before, total):
    return jnp.pad(w, [(0, 0)] * (w.ndim - 1) + [(before, total - before - w.shape[-1])])


def kernel(x, c, positions, w_ada, b_ada, norm1_g, w_in, conv_w, conv_b, lru_wa, lru_ba, lru_wx, lru_bx,
           lru_lambda, q_norm_g, w_uq, kv_norm_g, w_ukv, lru_out_g, mla_out_g, w_out, norm2_g, peer_wq,
           peer_keys, peer_u, peer_v, final_g):
    bsz, seq, d = x.shape
    depth = w_ada.shape[0]
    lw = d // 2
    qr = w_uq.shape[1]
    kvr = w_ukv.shape[1]
    assert depth == 1, "the final norm is fused into the (single) layer's PEER kernel"
    assert seq % TM == 0 and seq % TN == 0 and seq % TQ == 0 and seq % TS == 0
    assert d == MLA_HEADS * LANES and lw == MLA_HEADS * V_DIM

    inv_freq = 1.0 / (ROPE_THETA ** (jnp.arange(0, QK_ROPE, 2, dtype=F32) / QK_ROPE))
    invf = _pad_cols(jnp.concatenate([inv_freq, inv_freq])[None, :], QK_NOPE, LANES)
    pos3 = positions.reshape(bsz, seq, 1)
    row = lambda v: v.reshape(1, -1)

    for l in range(depth):
        mod3 = _ada(c, w_ada[l], b_ada[l]).reshape(bsz, 6, d)

        wi = w_in[l]
        o0, o1, o2, o3 = 2 * lw, 2 * lw + qr, 2 * lw + qr + kvr, 2 * lw + qr + kvr + QK_ROPE
        wkr = wi[:, o2:o3]
        wkr2 = jnp.concatenate([_pad_cols(wkr, QK_NOPE, LANES),
                                _pad_cols(_rot_half_cols(wkr), QK_NOPE, LANES)], axis=1).astype(BF16)
        uq = w_uq[l].reshape(qr, MLA_HEADS, QK_NOPE + QK_ROPE)
        uq_main = _pad_cols(uq, 0, LANES).reshape(qr, d)
        uq_rot = _pad_cols(_rot_half_cols(uq[..., QK_NOPE:]), QK_NOPE, LANES).reshape(qr, d)
        wuq2 = jnp.concatenate([uq_main, uq_rot], axis=1).astype(BF16)
        ukv = w_ukv[l].reshape(kvr, MLA_HEADS, QK_NOPE + V_DIM)
        wuk = _pad_cols(ukv[..., :QK_NOPE], 0, LANES).reshape(kvr, d).astype(BF16)
        uv = ukv[..., QK_NOPE:]
        odd = (jnp.arange(MLA_HEADS) % 2 == 1)[None, :, None]
        wuv = jnp.where(odd, _pad_cols(uv, V_DIM, LANES), _pad_cols(uv, 0, LANES)).reshape(kvr, d).astype(BF16)
        half = jnp.arange(LANES)[None, :] >= V_DIM
        vones = (half != odd[0]).astype(F32).reshape(1, d)

        lru_in, q, k, v = _pre(x, mod3, row(norm1_g[l]), pos3, invf, wi[:, :o0].astype(BF16),
                               wi[:, o0:o1].astype(BF16), wi[:, o1:o2].astype(BF16), wkr2,
                               row(q_norm_g[l]), wuq2, row(kv_norm_g[l]), wuk, wuv, vones)

        eye = jnp.eye(LRU_BLOCKS, dtype=F32)
        blockdiag = lambda w: jnp.einsum("hij,hg->higj", w, eye).reshape(lw, lw)
        wg = jnp.concatenate([blockdiag(lru_wa[l]), blockdiag(lru_wx[l])], axis=1).astype(BF16)
        bg = jnp.concatenate([lru_ba[l], lru_bx[l]])[None, :]
        yl = _lru(lru_in, conv_w[l], row(conv_b[l]), wg, bg, row(lru_lambda[l]), row(lru_out_g[l]))

        ym = _attn(q, k, v, row(mla_out_g[l]))

        wo = w_out[l].astype(BF16)
        x1, h2t = _post(yl, ym, x, mod3, wo[:lw], wo[lw:], row(norm2_g[l]))

        wqt = (peer_wq[l].reshape(d, PEER_HEADS, 2, PEER_HALF).transpose(2, 1, 3, 0)
               .reshape(2 * PEER_HEADS * PEER_HALF, d).astype(BF16))
        keys = peer_keys[l].astype(BF16)
        kbig = jnp.einsum("hkd,hg->khgd", keys[:, 0], jnp.eye(PEER_HEADS, dtype=BF16)).reshape(
            PEER_KEYS * PEER_HEADS, PEER_HEADS * PEER_HALF)
        cnt, w1, packed = _route(h2t, wqt, kbig, keys[:, 1])

        ne = peer_u.shape[1] // EB
        vt3 = peer_v[l].astype(BF16).reshape(ne, EB, d).transpose(0, 2, 1)
        out = _dense(h2t, peer_u[l].astype(BF16), vt3, cnt, w1, packed,
                     x1.reshape(bsz * seq, d), mod3, row(final_g), seq)
        x = out.reshape(bsz, seq, d)
    return x
```

```python
import functools

import jax
import jax.numpy as jnp
from jax import lax
from jax.experimental import pallas as pl
from jax.experimental.pallas import tpu as pltpu

F32 = jnp.float32
BF16 = jnp.bfloat16

LRU_BLOCKS = 8
CONV_WIDTH = 4
LRU_C = 8.0
MLA_HEADS = 8
QK_NOPE = 64
QK_ROPE = 32
V_DIM = 64
ROPE_THETA = 10000.0
PEER_HEADS = 8
PEER_KEYS = 128
PEER_HALF = 128
PEER_TOPK = 16
EPS = 1e-6
LOG2_E = 1.4426950408889634

LANES = 128
SUBLANES = 8
VMEM_LIMIT = 56 * 1024 * 1024

TM = 512
TS = 512
TQ = 512
TN_ROUTE = 512
TN = 512
EB = 2048
MXU_COLS = 512
DOT_ROWS = 2048
GATE_ROWS = 2
DENSE_SECTIONS = 1


def _cparams(*sem, flags=None):
    return pltpu.CompilerParams(dimension_semantics=sem, vmem_limit_bytes=VMEM_LIMIT, flags=flags)


def _rms(x, g):
    return x * lax.rsqrt(jnp.mean(x * x, axis=-1, keepdims=True) + EPS) * g


def _dot(a, b):
    return jnp.dot(a, b, preferred_element_type=F32)


def _ada_kernel(c_ref, w_ref, b_ref, o_ref):
    ca = jax.nn.silu(c_ref[...])
    o_ref[...] = jnp.dot(ca, w_ref[...], preferred_element_type=F32,
                         precision=lax.Precision.HIGHEST) + b_ref[...]


def _ada(c, w, b):
    bsz, d = c.shape
    n = w.shape[1]
    return pl.pallas_call(
        _ada_kernel,
        grid=(n // d,),
        in_specs=[pl.BlockSpec((bsz, d), lambda j: (0, 0)),
                  pl.BlockSpec((d, d), lambda j: (0, j)),
                  pl.BlockSpec((1, d), lambda j: (0, j))],
        out_specs=pl.BlockSpec((bsz, d), lambda j: (0, j)),
        out_shape=jax.ShapeDtypeStruct((bsz, n), F32),
        compiler_params=_cparams("arbitrary"),
        name="ada",
    )(c, w, b.reshape(1, n))


def _pre_kernel(x_ref, mod_ref, g1_ref, pos_ref, invf_ref, wlru_ref, wq_ref, wkv_ref, wkr_ref,
                qg_ref, wuq_ref, kvg_ref, wuk_ref, wuv_ref, vones_ref, lru_ref, q_ref, k_ref, v_ref):
    d = x_ref.shape[-1]
    x = x_ref[0]
    shift = mod_ref[0, 0:1, :]
    scale = mod_ref[0, 1:2, :]
    hb = (_rms(x, g1_ref[...]) * (1.0 + scale) + shift).astype(BF16)
    lru_ref[0] = _dot(hb, wlru_ref[...])

    ang = pos_ref[0].astype(F32) * invf_ref[...]
    cos = jnp.cos(ang)
    sin = jnp.sin(ang)

    qn = _rms(_dot(hb, wq_ref[...]), qg_ref[...]).astype(BF16)
    q2 = _dot(qn, wuq_ref[...])
    q_scale = (QK_NOPE + QK_ROPE) ** -0.5 * LOG2_E
    for h in range(MLA_HEADS):
        lo = h * LANES
        q_ref[0, :, lo:lo + LANES] = ((q2[:, lo:lo + LANES] * cos
                                       + q2[:, d + lo:d + lo + LANES] * sin) * q_scale).astype(BF16)

    kvn = _rms(_dot(hb, wkv_ref[...]), kvg_ref[...]).astype(BF16)
    kn = _dot(kvn, wuk_ref[...])
    v_ref[0] = (_dot(kvn, wuv_ref[...]) + vones_ref[...]).astype(BF16)
    kr2 = _dot(hb, wkr_ref[...])
    krot = kr2[:, :LANES] * cos + kr2[:, LANES:] * sin
    for h in range(MLA_HEADS):
        lo = h * LANES
        k_ref[0, :, lo:lo + LANES] = (kn[:, lo:lo + LANES] + krot).astype(BF16)


def _pre(x, mod3, g1, pos3, invf, wlru, wq, wkv, wkr2, qg, wuq2, kvg, wuk, wuv, vones):
    bsz, s, d = x.shape
    full = lambda a: pl.BlockSpec(a.shape, lambda b, i: (0,) * a.ndim)
    tok = lambda w: pl.BlockSpec((1, TM, w), lambda b, i: (b, i, 0))
    return pl.pallas_call(
        _pre_kernel,
        grid=(bsz, s // TM),
        in_specs=[tok(d), pl.BlockSpec((1, 6, d), lambda b, i: (b, 0, 0)), full(g1), tok(1), full(invf),
                  full(wlru), full(wq), full(wkv), full(wkr2), full(qg), full(wuq2), full(kvg),
                  full(wuk), full(wuv), full(vones)],
        out_specs=[tok(d), tok(d), tok(d), tok(d)],
        out_shape=[jax.ShapeDtypeStruct((bsz, s, d), F32),
                   jax.ShapeDtypeStruct((bsz, s, d), BF16),
                   jax.ShapeDtypeStruct((bsz, s, d), BF16),
                   jax.ShapeDtypeStruct((bsz, s, d), BF16)],
        compiler_params=_cparams("arbitrary", "arbitrary"),
        name="pre",
    )(x, mod3, g1, pos3, invf, wlru, wq, wkv, wkr2, qg, wuq2, kvg, wuk, wuv, vones)


def _lru_kernel(x_ref, cw_ref, cb_ref, wg_ref, bg_ref, lam_ref, og_ref, o_ref, prev_ref, h_ref):
    w = o_ref.shape[-1]

    @pl.when(pl.program_id(1) == 0)
    def _():
        prev_ref[...] = jnp.zeros_like(prev_ref)
        h_ref[...] = jnp.zeros_like(h_ref)

    x = x_ref[0, :, :w]
    gate = x_ref[0, :, w:]
    tail = prev_ref[...]
    row = lax.broadcasted_iota(jnp.int32, x.shape, 0)
    row8 = lax.broadcasted_iota(jnp.int32, tail.shape, 0)

    xc = cb_ref[...]
    for k in range(CONV_WIDTH - 1):
        back = CONV_WIDTH - 1 - k
        shifted = pltpu.roll(x, back, 0)
        first = jnp.where(row8 < back, pltpu.roll(tail, back, 0), shifted[:SUBLANES, :])
        shifted = jnp.concatenate([first, shifted[SUBLANES:, :]], axis=0)
        xc = xc + cw_ref[k:k + 1, :] * shifted
    xc = xc + cw_ref[CONV_WIDTH - 1:CONV_WIDTH, :] * x
    prev_ref[...] = x[TS - SUBLANES:, :]

    pre = _dot(xc.astype(BF16), wg_ref[...]) + bg_ref[...]
    r = jax.nn.sigmoid(pre[:, :w])
    i = jax.nn.sigmoid(pre[:, w:])
    log_a = -LRU_C * r * jax.nn.softplus(-lam_ref[...])
    a = jnp.exp(log_a)
    one_minus = 1.0 - a * a
    root = jnp.where(one_minus > 0.0, one_minus * lax.rsqrt(one_minus), 0.0)
    b = root * (i * xc)

    sub = row % SUBLANES
    dist = 1
    while dist < SUBLANES:
        keep = sub >= dist
        b = jnp.where(keep, a * pltpu.roll(b, dist, 0) + b, b)
        a = jnp.where(keep, a * pltpu.roll(a, dist, 0), a)
        dist *= 2
    state = h_ref[0:1, :]
    groups = []
    for g0 in range(0, TS, SUBLANES):
        hg = b[g0:g0 + SUBLANES, :] + a[g0:g0 + SUBLANES, :] * state
        groups.append(hg)
        state = hg[SUBLANES - 1:SUBLANES, :]
    h = jnp.concatenate(groups, axis=0)
    h_ref[...] = jnp.broadcast_to(state, h_ref.shape)

    y = jax.nn.gelu(gate) * h
    o_ref[0] = _rms(y, og_ref[...]).astype(BF16)


def _lru(lru_in, cw, cb, wg, bg, lam, og):
    bsz, s, d = lru_in.shape
    w = d // 2
    full = lambda a: pl.BlockSpec(a.shape, lambda b, t: (0,) * a.ndim)
    return pl.pallas_call(
        _lru_kernel,
        grid=(bsz, s // TS),
        in_specs=[pl.BlockSpec((1, TS, d), lambda b, t: (b, t, 0)),
                  full(cw), full(cb), full(wg), full(bg), full(lam), full(og)],
        out_specs=pl.BlockSpec((1, TS, w), lambda b, t: (b, t, 0)),
        out_shape=jax.ShapeDtypeStruct((bsz, s, w), BF16),
        scratch_shapes=[pltpu.VMEM((SUBLANES, w), F32), pltpu.VMEM((SUBLANES, w), F32)],
        compiler_params=_cparams("arbitrary", "arbitrary"),
        name="lru",
    )(lru_in, cw, cb, wg, bg, lam, og)


def _attn_kernel(q_ref, k_ref, v_ref, g_ref, o_ref, m_ref, acc_ref, bias_ref):
    i = pl.program_id(1)
    nt = (((1,), (1,)), ((), ()))
    row = lax.broadcasted_iota(jnp.int32, (TQ, TQ), 0)
    col = lax.broadcasted_iota(jnp.int32, (TQ, TQ), 1)
    lane = lax.broadcasted_iota(jnp.int32, (TQ, LANES), 1)

    m_ref[...] = jnp.full(m_ref.shape, -jnp.inf, F32)
    acc_ref[...] = jnp.zeros(acc_ref.shape, F32)
    bias_ref[...] = jnp.where(col <= row, 0.0, -jnp.inf)

    def block(j, masked):
        start = pl.multiple_of(j * TQ, TQ)

        def scores(h):
            lo = h * LANES
            s = lax.dot_general(q_ref[0, :, lo:lo + LANES], k_ref[0, pl.ds(start, TQ), lo:lo + LANES], nt,
                                preferred_element_type=F32)
            return s + bias_ref[...] if masked else s

        def softmax_step(h, s):
            m_old = m_ref[h]
            m_new = jnp.maximum(m_old, jnp.max(s, axis=-1, keepdims=True))
            m_ref[h] = m_new
            alpha = jnp.exp2(m_old - m_new)
            p = [jnp.exp2(s[:, c:c + LANES] - m_new) for c in range(0, TQ, LANES)]
            return alpha, jnp.concatenate(p, axis=-1).astype(BF16)

        def weighted_values(h, alpha, p):
            lo = h * LANES
            acc_ref[h] = alpha * acc_ref[h] + _dot(p, v_ref[0, pl.ds(start, TQ), lo:lo + LANES])

        pending_s, pending_p = {}, {}
        for t in range(MLA_HEADS + 2):
            if t < MLA_HEADS:
                pending_s[t] = scores(t)
            if 1 <= t <= MLA_HEADS:
                pending_p[t - 1] = softmax_step(t - 1, pending_s.pop(t - 1))
            if t >= 2:
                weighted_values(t - 2, *pending_p.pop(t - 2))

    def unmasked(j, carry):
        block(j, False)
        return carry

    def diagonal(j, carry):
        block(j, True)
        return carry

    lax.fori_loop(0, i, unmasked, 0)
    lax.fori_loop(i, i + 1, diagonal, 0)

    pairs = []
    for hp in range(MLA_HEADS // 2):
        even, odd = acc_ref[2 * hp], acc_ref[2 * hp + 1]
        num = jnp.where(lane < V_DIM, even, odd)
        den = jnp.where(lane < V_DIM, pltpu.roll(even, V_DIM, 1), pltpu.roll(odd, V_DIM, 1))
        pairs.append(num / den)
    y = jnp.concatenate(pairs, axis=-1)
    o_ref[0] = _rms(y, g_ref[...]).astype(BF16)


def _attn(q, k, v, g):
    bsz, s, d = q.shape
    w = g.shape[-1]
    return pl.pallas_call(
        _attn_kernel,
        grid=(bsz, s // TQ),
        in_specs=[pl.BlockSpec((1, TQ, d), lambda b, i: (b, i, 0)),
                  pl.BlockSpec((1, s, d), lambda b, i: (b, 0, 0)),
                  pl.BlockSpec((1, s, d), lambda b, i: (b, 0, 0)),
                  pl.BlockSpec(g.shape, lambda b, i: (0, 0))],
        out_specs=pl.BlockSpec((1, TQ, w), lambda b, i: (b, i, 0)),
        out_shape=jax.ShapeDtypeStruct((bsz, s, w), BF16),
        scratch_shapes=[pltpu.VMEM((MLA_HEADS, TQ, LANES), F32), pltpu.VMEM((MLA_HEADS, TQ, LANES), F32),
                        pltpu.VMEM((TQ, TQ), F32)],
        compiler_params=_cparams("arbitrary", "arbitrary"),
        name="attn",
    )(q, k, v, g)


def _post_kernel(yl_ref, ym_ref, x_ref, mod_ref, wol_ref, wom_ref, g2_ref, x1_ref, h2t_ref):
    mix = _dot(yl_ref[0], wol_ref[...]) + _dot(ym_ref[0], wom_ref[...])
    x1 = x_ref[0] + mod_ref[0, 2:3, :] * mix
    x1_ref[0] = x1
    h2 = _rms(x1, g2_ref[...]) * (1.0 + mod_ref[0, 4:5, :]) + mod_ref[0, 3:4, :]
    h2t_ref[...] = h2.T.astype(BF16)


def _post(yl, ym, x, mod3, wol, wom, g2):
    bsz, s, d = x.shape
    w = yl.shape[-1]
    nt = s // TM
    full = lambda a: pl.BlockSpec(a.shape, lambda b, i: (0,) * a.ndim)
    tok = lambda c: pl.BlockSpec((1, TM, c), lambda b, i: (b, i, 0))
    return pl.pallas_call(
        _post_kernel,
        grid=(bsz, nt),
        in_specs=[tok(w), tok(w), tok(d), pl.BlockSpec((1, 6, d), lambda b, i: (b, 0, 0)),
                  full(wol), full(wom), full(g2)],
        out_specs=[tok(d), pl.BlockSpec((d, TM), lambda b, i: (0, b * nt + i))],
        out_shape=[jax.ShapeDtypeStruct((bsz, s, d), F32),
                   jax.ShapeDtypeStruct((d, bsz * s), BF16)],
        compiler_params=_cparams("arbitrary", "arbitrary"),
        name="post",
    )(yl, ym, x, mod3, wol, wom, g2)


def _sort_pairs(n):
    pairs = []

    def merge(lo, hi, r):
        step = r * 2
        if step < hi - lo:
            merge(lo, hi, step)
            merge(lo + r, hi, step)
            pairs.extend((i, i + r) for i in range(lo + r, hi - r, step))
        else:
            pairs.append((lo, lo + r))

    def sort(lo, hi):
        if hi - lo >= 1:
            mid = lo + (hi - lo) // 2
            sort(lo, mid)
            sort(mid + 1, hi)
            merge(lo, hi, 1)

    sort(0, n - 1)
    return pairs


def _sort_desc(xs):
    xs = list(xs)
    for i, j in _sort_pairs(len(xs)):
        a, b = xs[i], xs[j]
        xs[i], xs[j] = jnp.maximum(a, b), jnp.minimum(a, b)
    return xs


def _merge_top(a, b):
    n = len(a)
    xs = [jnp.maximum(a[i], b[n - 1 - i]) for i in range(n)]
    dist = n // 2
    while dist >= 1:
        for i in range(n):
            if i & dist == 0:
                p, q = xs[i], xs[i + dist]
                xs[i], xs[i + dist] = jnp.maximum(p, q), jnp.minimum(p, q)
        dist //= 2
    return xs


def _kth_largest(vals, k):
    n = 1
    while n < len(vals):
        n *= 2
    present = [True] * len(vals) + [False] * (n - len(vals))
    ops = []
    for i, j in _sort_pairs(n):
        if present[i] and present[j]:
            ops.append(("cmp", i, j))
        elif present[j]:
            ops.append(("mov", i, j))
            present[i], present[j] = True, False
    need = {k}
    live = []
    for op in reversed(ops):
        kind, i, j = op
        if kind == "mov":
            if i in need:
                need.discard(i)
                need.add(j)
                live.append((kind, i, j, True, False))
        else:
            hi, lo = i in need, j in need
            if hi or lo:
                need.update((i, j))
                live.append((kind, i, j, hi, lo))
    xs = list(vals) + [None] * (n - len(vals))
    for kind, i, j, hi, lo in reversed(live):
        a, b = xs[i], xs[j]
        if kind == "mov":
            xs[i], xs[j] = b, None
        else:
            xs[i] = jnp.maximum(a, b) if hi else None
            xs[j] = jnp.minimum(a, b) if lo else None
    return xs[k]


def _top_values(slabs, n):
    groups = [_sort_desc(slabs[g:g + n]) for g in range(0, len(slabs), n)]
    while len(groups) > 1:
        groups = [_merge_top(groups[g], groups[g + 1]) for g in range(0, len(groups), 2)]
    return groups[0]


PACK_ROWS = 2 * SUBLANES


def _pair_bits(x):
    bits = pltpu.bitcast(x.astype(BF16).astype(F32), jnp.uint32)
    return (bits & jnp.uint32(0xFFFF0000)) | (bits >> 16)


def _route_kernel(h2t_ref, wqt_ref, kbig_ref, keys2_ref, c_ref, w1_ref, p_ref):
    tn = h2t_ref.shape[-1]
    nk = PEER_KEYS
    half_rows = PEER_HEADS * PEER_HALF
    qt = _dot(wqt_ref[...], h2t_ref[...])
    sub = lax.broadcasted_iota(jnp.int32, (SUBLANES, tn), 0)

    sc1 = _dot(kbig_ref[...], qt[:half_rows, :].astype(BF16))
    slabs = [sc1[PEER_HEADS * k:PEER_HEADS * (k + 1), :] for k in range(nk)]
    a = _top_values(slabs, PEER_TOPK)

    b = [None] * PEER_TOPK
    for h in range(PEER_HEADS):
        r0 = half_rows + h * PEER_HALF
        sc = _dot(keys2_ref[h], qt[r0:r0 + PEER_HALF, :].astype(BF16))
        tiles = [sc[SUBLANES * t:SUBLANES * (t + 1), :] for t in range(nk // SUBLANES)]
        xs = _sort_desc(tiles)
        shift = SUBLANES // 2
        while shift >= 1:
            xs = _merge_top(xs, [pltpu.roll(v, shift, 0) for v in xs])
            shift //= 2
        for ib in range(nk // PACK_ROWS):
            ranks, weights = [], []
            for tile in tiles[2 * ib:2 * ib + 2]:
                rk = jnp.full_like(tile, float(PEER_TOPK))
                for r in reversed(range(PEER_TOPK)):
                    rk = jnp.where(tile >= xs[r], float(r), rk)
                ranks.append(rk)
                weights.append(jnp.exp(tile - xs[0]))
            rk16 = jnp.concatenate(ranks, axis=0).astype(BF16)
            w16 = jnp.concatenate(weights, axis=0).astype(BF16)
            for lb in range(tn // LANES):
                r0 = 2 * h * PACK_ROWS
                p_ref[lb, ib, r0:r0 + PACK_ROWS, :] = rk16[:, lb * LANES:(lb + 1) * LANES]
                p_ref[lb, ib, r0 + PACK_ROWS:r0 + 2 * PACK_ROWS, :] = w16[:, lb * LANES:(lb + 1) * LANES]
        for r in range(PEER_TOPK):
            b[r] = xs[r] if h == 0 else jnp.where(sub == h, xs[r], b[r])

    pairs = [(i, j) for i in range(PEER_TOPK) for j in range(PEER_TOPK) if (i + 1) * (j + 1) <= PEER_TOPK]
    cands = {ij: a[ij[0]] + b[ij[1]] for ij in pairs}
    tau = _kth_largest(list(cands.values()), PEER_TOPK - 1)
    top = a[0] + b[0]
    z = jnp.zeros_like(tau)
    for c in cands.values():
        z = z + jnp.where(c >= tau, jnp.exp(c - top), 0.0)
    inv_z = 1.0 / z
    need = []
    for j in range(PEER_TOPK):
        t = None
        for i in range(PEER_TOPK):
            if (i, j) in cands:
                v = jnp.where(cands[(i, j)] >= tau, a[i], jnp.inf)
                t = v if t is None else jnp.minimum(t, v)
        need.append(t)
    for k in range(nk):
        cnt = jnp.zeros_like(tau)
        for j in range(PEER_TOPK):
            cnt = jnp.where(slabs[k] >= need[j], float(j + 1), cnt)
        c_ref[k] = _pair_bits(cnt)
        w1_ref[k] = _pair_bits(jnp.exp(slabs[k] - a[0]) * inv_z)


def _route(h2t, wqt, kbig, keys2):
    d, t = h2t.shape
    tn = TN_ROUTE
    by_key = pl.BlockSpec((PEER_KEYS, PEER_HEADS, tn), lambda i: (0, 0, i))
    packed = (PEER_KEYS // PACK_ROWS, 2 * PEER_HEADS * PACK_ROWS, LANES)
    return pl.pallas_call(
        _route_kernel,
        grid=(t // tn,),
        in_specs=[pl.BlockSpec((d, tn), lambda i: (0, i)),
                  pl.BlockSpec(wqt.shape, lambda i: (0, 0)),
                  pl.BlockSpec(kbig.shape, lambda i: (0, 0)),
                  pl.BlockSpec(keys2.shape, lambda i: (0, 0, 0))],
        out_specs=[by_key, by_key, pl.BlockSpec((tn // LANES,) + packed, lambda i: (i, 0, 0, 0))],
        out_shape=[jax.ShapeDtypeStruct((PEER_KEYS, PEER_HEADS, t), jnp.uint32),
                   jax.ShapeDtypeStruct((PEER_KEYS, PEER_HEADS, t), jnp.uint32),
                   jax.ShapeDtypeStruct((t // LANES,) + packed, BF16)],
        compiler_params=_cparams("arbitrary"),
        name="route",
    )(h2t, wqt, kbig, keys2)


def _dense_kernel(flag_ref, h2t_ref, u_ref, vt_ref, c_ref, w1_ref, p_ref, x1_ref, mod_ref,
                  fg_ref, o_ref, acc_ref, act_ref, gat_ref, *, nblk):
    s = pl.program_id(0)
    n_items = pl.num_programs(0) - 2
    tn = h2t_ref.shape[-1]
    eb = u_ref.shape[0]
    rows_per_step = eb // PEER_KEYS

    @pl.when(s == 0)
    def _():
        act_ref[1] = jnp.zeros(act_ref.shape[1:], F32)
        gat_ref[0] = jnp.zeros(gat_ref.shape[1:], BF16)

    jb = jnp.clip(s - 1, 0, n_items - 1) % nblk
    jc = jnp.clip(s - 2, 0, n_items - 1) % nblk

    @pl.when(jc == 0)
    def _():
        acc_ref[...] = jnp.zeros_like(acc_ref)

    n_lb = tn // LANES

    def row_tile(ref, i1, h, ls):
        return pltpu.bitcast(jnp.broadcast_to(ref[i1, h:h + 1, ls], (SUBLANES, LANES)), BF16)

    def gate_group(ig, lb, oth):
        rows = range(ig * GATE_ROWS, (ig + 1) * GATE_ROWS)
        ls = slice(lb * LANES, (lb + 1) * LANES)
        cntb = [[row_tile(c_ref, jb * rows_per_step + il, h, ls) for h in range(PEER_HEADS)] for il in rows]
        w1b = [[row_tile(w1_ref, jb * rows_per_step + il, h, ls) for h in range(PEER_HEADS)] for il in rows]
        for ib in range(PEER_KEYS // PACK_ROWS):
            g = [None] * GATE_ROWS
            for h in range(PEER_HEADS):
                r0 = 2 * h * PACK_ROWS
                rank2 = p_ref[lb, ib, r0:r0 + PACK_ROWS, :]
                w2 = p_ref[lb, ib, r0 + PACK_ROWS:r0 + 2 * PACK_ROWS, :]
                for k in range(GATE_ROWS):
                    term = jnp.minimum(jnp.maximum(cntb[k][h] - rank2, 0.0), w2) * w1b[k][h]
                    g[k] = term if g[k] is None else g[k] + term
            for k, il in enumerate(rows):
                r1 = il * PEER_KEYS + ib * PACK_ROWS
                act = act_ref[oth, r1:r1 + PACK_ROWS, ls]
                gat_ref[oth, r1:r1 + PACK_ROWS, ls] = jax.nn.gelu(act.astype(BF16)) * g[k]

    def matmul_chunks(cur):
        chunks = []
        for c0 in range(0, tn, MXU_COLS):
            cs = slice(c0, c0 + MXU_COLS)
            rows = min(DOT_ROWS, acc_ref.shape[0])
            for r0 in range(0, acc_ref.shape[0], rows):
                rs = slice(r0, r0 + rows)

                def second(rs=rs, cs=cs):
                    acc_ref[rs, cs] += _dot(vt_ref[0, rs, :], gat_ref[cur, :, cs])
                chunks.append(second)
        for c0 in range(0, tn, MXU_COLS):
            cs = slice(c0, c0 + MXU_COLS)
            rows = min(DOT_ROWS, eb)
            for r0 in range(0, eb, rows):
                rs = slice(r0, r0 + rows)

                def first(rs=rs, cs=cs):
                    act_ref[cur, rs, cs] = _dot(u_ref[rs, :], h2t_ref[:, cs])
                chunks.append(first)
        return chunks

    def sections(cur, oth):
        groups = [(ig, lb) for ig in range(rows_per_step // GATE_ROWS) for lb in range(n_lb)]
        chunks = matmul_chunks(cur)
        per_section = len(groups) // DENSE_SECTIONS
        for sec in range(DENSE_SECTIONS):
            @pl.when(flag_ref[0] == 0)
            def _(sec=sec):
                for gi in range(sec * per_section, (sec + 1) * per_section):
                    for ci in range(gi * len(chunks) // len(groups), (gi + 1) * len(chunks) // len(groups)):
                        chunks[ci]()
                    gate_group(*groups[gi], oth)

    for parity in range(2):
        @pl.when(s % 2 == parity)
        def _(parity=parity):
            sections(parity, 1 - parity)

    @pl.when(jnp.logical_and(jc == nblk - 1, s >= 2))
    def _():
        x2 = x1_ref[...] + mod_ref[0, 5:6, :] * acc_ref[...].T
        o_ref[...] = _rms(x2, fg_ref[...])


def _dense(h2t, u, vt3, cnt, w1, packed, x1, mod3, fg, seq):
    d, t = h2t.shape
    ne = vt3.shape[0]
    per_seq = seq // TN
    n_items = (t // TN) * ne
    item = lambda s, lag: jnp.clip(s - lag, 0, n_items - 1)
    tile = lambda s, lag: item(s, lag) // ne
    block = lambda s, lag: item(s, lag) % ne
    by_key = pl.BlockSpec((PEER_KEYS, PEER_HEADS, TN), lambda s: (0, 0, tile(s, 1)))
    return pl.pallas_call(
        functools.partial(_dense_kernel, nblk=ne),
        grid=(n_items + 2,),
        in_specs=[pl.BlockSpec(memory_space=pltpu.SMEM),
                  pl.BlockSpec((d, TN), lambda s: (0, tile(s, 0))),
                  pl.BlockSpec((EB, d), lambda s: (block(s, 0), 0)),
                  pl.BlockSpec((1, d, EB), lambda s: (block(s, 2), 0, 0)),
                  by_key, by_key,
                  pl.BlockSpec((TN // LANES,) + packed.shape[1:], lambda s: (tile(s, 1), 0, 0, 0)),
                  pl.BlockSpec((TN, d), lambda s: (tile(s, 2), 0)),
                  pl.BlockSpec((1, 6, d), lambda s: (tile(s, 2) // per_seq, 0, 0)),
                  pl.BlockSpec(fg.shape, lambda s: (0, 0))],
        out_specs=pl.BlockSpec((TN, d), lambda s: (tile(s, 2), 0)),
        out_shape=jax.ShapeDtypeStruct((t, d), F32),
        scratch_shapes=[pltpu.VMEM((d, TN), F32), pltpu.VMEM((2, EB, TN), F32),
                        pltpu.VMEM((2, EB, TN), BF16)],
        compiler_params=_cparams("arbitrary"),
        name="dense",
    )(jnp.zeros((1,), jnp.int32), h2t, u, vt3, cnt, w1, packed, x1, mod3, fg)


def _rot_half_cols(w):
    half = w.shape[-1] // 2
    return jnp.concatenate([-w[..., half:], w[..., :half]], axis=-1)


def _pad_cols(w, before, total):
    return jnp.pad(w, [(0, 0)] * (w.ndim - 1) + [(before, total - before - w.shape[-1])])


def kernel(x, c, positions, w_ada, b_ada, norm1_g, w_in, conv_w, conv_b, lru_wa, lru_ba, lru_wx, lru_bx,
           lru_lambda, q_norm_g, w_uq, kv_norm_g, w_ukv, lru_out_g, mla_out_g, w_out, norm2_g, peer_wq,
           peer_keys, peer_u, peer_v, final_g):
    bsz, seq, d = x.shape
    depth = w_ada.shape[0]
    lw = d // 2
    qr = w_uq.shape[1]
    kvr = w_ukv.shape[1]
    assert depth == 1, "the final norm is fused into the (single) layer's PEER kernel"
    assert seq % TM == 0 and seq % TN == 0 and seq % TQ == 0 and seq % TS == 0
    assert d == MLA_HEADS * LANES and lw == MLA_HEADS * V_DIM

    inv_freq = 1.0 / (ROPE_THETA ** (jnp.arange(0, QK_ROPE, 2, dtype=F32) / QK_ROPE))
    invf = _pad_cols(jnp.concatenate([inv_freq, inv_freq])[None, :], QK_NOPE, LANES)
    pos3 = positions.reshape(bsz, seq, 1)
    row = lambda v: v.reshape(1, -1)

    for l in range(depth):
        mod3 = _ada(c, w_ada[l], b_ada[l]).reshape(bsz, 6, d)

        wi = w_in[l]
        o0, o1, o2, o3 = 2 * lw, 2 * lw + qr, 2 * lw + qr + kvr, 2 * lw + qr + kvr + QK_ROPE
        wkr = wi[:, o2:o3]
        wkr2 = jnp.concatenate([_pad_cols(wkr, QK_NOPE, LANES),
                                _pad_cols(_rot_half_cols(wkr), QK_NOPE, LANES)], axis=1).astype(BF16)
        uq = w_uq[l].reshape(qr, MLA_HEADS, QK_NOPE + QK_ROPE)
        uq_main = _pad_cols(uq, 0, LANES).reshape(qr, d)
        uq_rot = _pad_cols(_rot_half_cols(uq[..., QK_NOPE:]), QK_NOPE, LANES).reshape(qr, d)
        wuq2 = jnp.concatenate([uq_main, uq_rot], axis=1).astype(BF16)
        ukv = w_ukv[l].reshape(kvr, MLA_HEADS, QK_NOPE + V_DIM)
        wuk = _pad_cols(ukv[..., :QK_NOPE], 0, LANES).reshape(kvr, d).astype(BF16)
        uv = ukv[..., QK_NOPE:]
        odd = (jnp.arange(MLA_HEADS) % 2 == 1)[None, :, None]
        wuv = jnp.where(odd, _pad_cols(uv, V_DIM, LANES), _pad_cols(uv, 0, LANES)).reshape(kvr, d).astype(BF16)
        half = jnp.arange(LANES)[None, :] >= V_DIM
        vones = (half != odd[0]).astype(F32).reshape(1, d)

        lru_in, q, k, v = _pre(x, mod3, row(norm1_g[l]), pos3, invf, wi[:, :o0].astype(BF16),
                               wi[:, o0:o1].astype(BF16), wi[:, o1:o2].astype(BF16), wkr2,
                               row(q_norm_g[l]), wuq2, row(kv_norm_g[l]), wuk, wuv, vones)

        eye = jnp.eye(LRU_BLOCKS, dtype=F32)
        blockdiag = lambda w: jnp.einsum("hij,hg->higj", w, eye).reshape(lw, lw)
        wg = jnp.concatenate([blockdiag(lru_wa[l]), blockdiag(lru_wx[l])], axis=1).astype(BF16)
        bg = jnp.concatenate([lru_ba[l], lru_bx[l]])[None, :]
        yl = _lru(lru_in, conv_w[l], row(conv_b[l]), wg, bg, row(lru_lambda[l]), row(lru_out_g[l]))

        ym = _attn(q, k, v, row(mla_out_g[l]))

        wo = w_out[l].astype(BF16)
        x1, h2t = _post(yl, ym, x, mod3, wo[:lw], wo[lw:], row(norm2_g[l]))

        wqt = (peer_wq[l].reshape(d, PEER_HEADS, 2, PEER_HALF).transpose(2, 1, 3, 0)
               .reshape(2 * PEER_HEADS * PEER_HALF, d).astype(BF16))
        keys = peer_keys[l].astype(BF16)
        kbig = jnp.einsum("hkd,hg->khgd", keys[:, 0], jnp.eye(PEER_HEADS, dtype=BF16)).reshape(
            PEER_KEYS * PEER_HEADS, PEER_HEADS * PEER_HALF)
        cnt, w1, packed = _route(h2t, wqt, kbig, keys[:, 1])

        ne = peer_u.shape[1] // EB
        vt3 = peer_v[l].astype(BF16).reshape(ne, EB, d).transpose(0, 2, 1)
        out = _dense(h2t, peer_u[l].astype(BF16), vt3, cnt, w1, packed,
                     x1.reshape(bsz * seq, d), mod3, row(final_g), seq)
        x = out.reshape(bsz, seq, d)
    return x
```

```python
import functools

import jax
import jax.numpy as jnp
from jax import lax
from jax.experimental import pallas as pl
from jax.experimental.pallas import tpu as pltpu

F32 = jnp.float32
BF16 = jnp.bfloat16

LRU_BLOCKS = 8
CONV_WIDTH = 4
LRU_C = 8.0
MLA_HEADS = 8
QK_NOPE = 64
QK_ROPE = 32
V_DIM = 64
ROPE_THETA = 10000.0
PEER_HEADS = 8
PEER_KEYS = 128
PEER_HALF = 128
PEER_TOPK = 16
EPS = 1e-6
LOG2_E = 1.4426950408889634

LANES = 128
SUBLANES = 8
VMEM_LIMIT = 56 * 1024 * 1024

TM = 1024
TS = 512
TQ = 512
TN_ROUTE = 512
TN = 512
EB = 2048
GATE_ROWS = 2


def _cparams(*sem):
    return pltpu.CompilerParams(dimension_semantics=sem, vmem_limit_bytes=VMEM_LIMIT)


def _rms(x, g):
    return x * lax.rsqrt(jnp.mean(x * x, axis=-1, keepdims=True) + EPS) * g


def _dot(a, b):
    return jnp.dot(a, b, preferred_element_type=F32)


def _ada_kernel(c_ref, w_ref, b_ref, o_ref):
    ca = jax.nn.silu(c_ref[...])
    o_ref[...] = jnp.dot(ca, w_ref[...], preferred_element_type=F32,
                         precision=lax.Precision.HIGHEST) + b_ref[...]


def _ada(c, w, b):
    bsz, d = c.shape
    n = w.shape[1]
    return pl.pallas_call(
        _ada_kernel,
        grid=(n // d,),
        in_specs=[pl.BlockSpec((bsz, d), lambda j: (0, 0)),
                  pl.BlockSpec((d, d), lambda j: (0, j)),
                  pl.BlockSpec((1, d), lambda j: (0, j))],
        out_specs=pl.BlockSpec((bsz, d), lambda j: (0, j)),
        out_shape=jax.ShapeDtypeStruct((bsz, n), F32),
        compiler_params=_cparams("arbitrary"),
        name="ada",
    )(c, w, b.reshape(1, n))


def _pre_kernel(x_ref, mod_ref, g1_ref, pos_ref, invf_ref, wlru_ref, wq_ref, wkv_ref, wkr_ref,
                qg_ref, wuq_ref, kvg_ref, wuk_ref, wuv_ref, vones_ref, lru_ref, q_ref, k_ref, v_ref):
    d = x_ref.shape[-1]
    x = x_ref[0]
    shift = mod_ref[0, 0:1, :]
    scale = mod_ref[0, 1:2, :]
    hb = (_rms(x, g1_ref[...]) * (1.0 + scale) + shift).astype(BF16)
    lru_ref[0] = _dot(hb, wlru_ref[...])

    ang = pos_ref[0].astype(F32) * invf_ref[...]
    cos = jnp.cos(ang)
    sin = jnp.sin(ang)

    qn = _rms(_dot(hb, wq_ref[...]), qg_ref[...]).astype(BF16)
    q2 = _dot(qn, wuq_ref[...])
    q_scale = (QK_NOPE + QK_ROPE) ** -0.5 * LOG2_E
    for h in range(MLA_HEADS):
        lo = h * LANES
        q_ref[0, :, lo:lo + LANES] = ((q2[:, lo:lo + LANES] * cos
                                       + q2[:, d + lo:d + lo + LANES] * sin) * q_scale).astype(BF16)

    kvn = _rms(_dot(hb, wkv_ref[...]), kvg_ref[...]).astype(BF16)
    kn = _dot(kvn, wuk_ref[...])
    v_ref[0] = (_dot(kvn, wuv_ref[...]) + vones_ref[...]).astype(BF16)
    kr2 = _dot(hb, wkr_ref[...])
    krot = kr2[:, :LANES] * cos + kr2[:, LANES:] * sin
    for h in range(MLA_HEADS):
        lo = h * LANES
        k_ref[0, :, lo:lo + LANES] = (kn[:, lo:lo + LANES] + krot).astype(BF16)


def _pre(x, mod3, g1, pos3, invf, wlru, wq, wkv, wkr2, qg, wuq2, kvg, wuk, wuv, vones):
    bsz, s, d = x.shape
    full = lambda a: pl.BlockSpec(a.shape, lambda b, i: (0,) * a.ndim)
    tok = lambda w: pl.BlockSpec((1, TM, w), lambda b, i: (b, i, 0))
    return pl.pallas_call(
        _pre_kernel,
        grid=(bsz, s // TM),
        in_specs=[tok(d), pl.BlockSpec((1, 6, d), lambda b, i: (b, 0, 0)), full(g1), tok(1), full(invf),
                  full(wlru), full(wq), full(wkv), full(wkr2), full(qg), full(wuq2), full(kvg),
                  full(wuk), full(wuv), full(vones)],
        out_specs=[tok(d), tok(d), tok(d), tok(d)],
        out_shape=[jax.ShapeDtypeStruct((bsz, s, d), F32),
                   jax.ShapeDtypeStruct((bsz, s, d), BF16),
                   jax.ShapeDtypeStruct((bsz, s, d), BF16),
                   jax.ShapeDtypeStruct((bsz, s, d), BF16)],
        compiler_params=_cparams("arbitrary", "arbitrary"),
        name="pre",
    )(x, mod3, g1, pos3, invf, wlru, wq, wkv, wkr2, qg, wuq2, kvg, wuk, wuv, vones)


def _lru_kernel(x_ref, cw_ref, cb_ref, wg_ref, bg_ref, lam_ref, og_ref, o_ref, prev_ref, h_ref):
    w = o_ref.shape[-1]

    @pl.when(pl.program_id(1) == 0)
    def _():
        prev_ref[...] = jnp.zeros_like(prev_ref)
        h_ref[...] = jnp.zeros_like(h_ref)

    x = x_ref[0, :, :w]
    gate = x_ref[0, :, w:]
    tail = prev_ref[...]
    row = lax.broadcasted_iota(jnp.int32, x.shape, 0)
    row8 = lax.broadcasted_iota(jnp.int32, tail.shape, 0)

    xc = cb_ref[...]
    for k in range(CONV_WIDTH - 1):
        back = CONV_WIDTH - 1 - k
        shifted = pltpu.roll(x, back, 0)
        first = jnp.where(row8 < back, pltpu.roll(tail, back, 0), shifted[:SUBLANES, :])
        shifted = jnp.concatenate([first, shifted[SUBLANES:, :]], axis=0)
        xc = xc + cw_ref[k:k + 1, :] * shifted
    xc = xc + cw_ref[CONV_WIDTH - 1:CONV_WIDTH, :] * x
    prev_ref[...] = x[TS - SUBLANES:, :]

    pre = _dot(xc.astype(BF16), wg_ref[...]) + bg_ref[...]
    r = jax.nn.sigmoid(pre[:, :w])
    i = jax.nn.sigmoid(pre[:, w:])
    log_a = -LRU_C * r * jax.nn.softplus(-lam_ref[...])
    a = jnp.exp(log_a)
    one_minus = 1.0 - a * a
    root = jnp.where(one_minus > 0.0, one_minus * lax.rsqrt(one_minus), 0.0)
    b = root * (i * xc)

    sub = row % SUBLANES
    dist = 1
    while dist < SUBLANES:
        keep = sub >= dist
        b = jnp.where(keep, a * pltpu.roll(b, dist, 0) + b, b)
        a = jnp.where(keep, a * pltpu.roll(a, dist, 0), a)
        dist *= 2
    state = h_ref[0:1, :]
    groups = []
    for g0 in range(0, TS, SUBLANES):
        hg = b[g0:g0 + SUBLANES, :] + a[g0:g0 + SUBLANES, :] * state
        groups.append(hg)
        state = hg[SUBLANES - 1:SUBLANES, :]
    h = jnp.concatenate(groups, axis=0)
    h_ref[...] = jnp.broadcast_to(state, h_ref.shape)

    y = jax.nn.gelu(gate) * h
    o_ref[0] = _rms(y, og_ref[...]).astype(BF16)


def _lru(lru_in, cw, cb, wg, bg, lam, og):
    bsz, s, d = lru_in.shape
    w = d // 2
    full = lambda a: pl.BlockSpec(a.shape, lambda b, t: (0,) * a.ndim)
    return pl.pallas_call(
        _lru_kernel,
        grid=(bsz, s // TS),
        in_specs=[pl.BlockSpec((1, TS, d), lambda b, t: (b, t, 0)),
                  full(cw), full(cb), full(wg), full(bg), full(lam), full(og)],
        out_specs=pl.BlockSpec((1, TS, w), lambda b, t: (b, t, 0)),
        out_shape=jax.ShapeDtypeStruct((bsz, s, w), BF16),
        scratch_shapes=[pltpu.VMEM((SUBLANES, w), F32), pltpu.VMEM((SUBLANES, w), F32)],
        compiler_params=_cparams("arbitrary", "arbitrary"),
        name="lru",
    )(lru_in, cw, cb, wg, bg, lam, og)


def _attn_kernel(q_ref, k_ref, v_ref, g_ref, o_ref, m_ref, acc_ref, bias_ref):
    i = pl.program_id(1)
    nt = (((1,), (1,)), ((), ()))
    row = lax.broadcasted_iota(jnp.int32, (TQ, TQ), 0)
    col = lax.broadcasted_iota(jnp.int32, (TQ, TQ), 1)
    lane = lax.broadcasted_iota(jnp.int32, (TQ, LANES), 1)

    m_ref[...] = jnp.full(m_ref.shape, -jnp.inf, F32)
    acc_ref[...] = jnp.zeros(acc_ref.shape, F32)
    bias_ref[...] = jnp.where(col <= row, 0.0, -jnp.inf)

    def block(j, masked):
        start = pl.multiple_of(j * TQ, TQ)

        def scores(h):
            lo = h * LANES
            s = lax.dot_general(q_ref[0, :, lo:lo + LANES], k_ref[0, pl.ds(start, TQ), lo:lo + LANES], nt,
                                preferred_element_type=F32)
            return s + bias_ref[...] if masked else s

        def softmax_step(h, s):
            m_old = m_ref[h]
            m_new = jnp.maximum(m_old, jnp.max(s, axis=-1, keepdims=True))
            m_ref[h] = m_new
            alpha = jnp.exp2(m_old - m_new)
            p = [jnp.exp2(s[:, c:c + LANES] - m_new) for c in range(0, TQ, LANES)]
            return alpha, jnp.concatenate(p, axis=-1).astype(BF16)

        def weighted_values(h, alpha, p):
            lo = h * LANES
            acc_ref[h] = alpha * acc_ref[h] + _dot(p, v_ref[0, pl.ds(start, TQ), lo:lo + LANES])

        pending_s, pending_p = {}, {}
        for t in range(MLA_HEADS + 2):
            if t < MLA_HEADS:
                pending_s[t] = scores(t)
            if 1 <= t <= MLA_HEADS:
                pending_p[t - 1] = softmax_step(t - 1, pending_s.pop(t - 1))
            if t >= 2:
                weighted_values(t - 2, *pending_p.pop(t - 2))

    def unmasked(j, carry):
        block(j, False)
        return carry

    def diagonal(j, carry):
        block(j, True)
        return carry

    lax.fori_loop(0, i, unmasked, 0)
    lax.fori_loop(i, i + 1, diagonal, 0)

    pairs = []
    for hp in range(MLA_HEADS // 2):
        even, odd = acc_ref[2 * hp], acc_ref[2 * hp + 1]
        num = jnp.where(lane < V_DIM, even, odd)
        den = jnp.where(lane < V_DIM, pltpu.roll(even, V_DIM, 1), pltpu.roll(odd, V_DIM, 1))
        pairs.append(num / den)
    y = jnp.concatenate(pairs, axis=-1)
    o_ref[0] = _rms(y, g_ref[...]).astype(BF16)


def _attn(q, k, v, g):
    bsz, s, d = q.shape
    w = g.shape[-1]
    return pl.pallas_call(
        _attn_kernel,
        grid=(bsz, s // TQ),
        in_specs=[pl.BlockSpec((1, TQ, d), lambda b, i: (b, i, 0)),
                  pl.BlockSpec((1, s, d), lambda b, i: (b, 0, 0)),
                  pl.BlockSpec((1, s, d), lambda b, i: (b, 0, 0)),
                  pl.BlockSpec(g.shape, lambda b, i: (0, 0))],
        out_specs=pl.BlockSpec((1, TQ, w), lambda b, i: (b, i, 0)),
        out_shape=jax.ShapeDtypeStruct((bsz, s, w), BF16),
        scratch_shapes=[pltpu.VMEM((MLA_HEADS, TQ, LANES), F32), pltpu.VMEM((MLA_HEADS, TQ, LANES), F32),
                        pltpu.VMEM((TQ, TQ), F32)],
        compiler_params=_cparams("arbitrary", "arbitrary"),
        name="attn",
    )(q, k, v, g)


def _post_kernel(yl_ref, ym_ref, x_ref, mod_ref, wol_ref, wom_ref, g2_ref, x1_ref, h2t_ref):
    mix = _dot(yl_ref[0], wol_ref[...]) + _dot(ym_ref[0], wom_ref[...])
    x1 = x_ref[0] + mod_ref[0, 2:3, :] * mix
    x1_ref[0] = x1
    h2 = _rms(x1, g2_ref[...]) * (1.0 + mod_ref[0, 4:5, :]) + mod_ref[0, 3:4, :]
    h2t_ref[...] = h2.T.astype(BF16)


def _post(yl, ym, x, mod3, wol, wom, g2):
    bsz, s, d = x.shape
    w = yl.shape[-1]
    nt = s // TM
    full = lambda a: pl.BlockSpec(a.shape, lambda b, i: (0,) * a.ndim)
    tok = lambda c: pl.BlockSpec((1, TM, c), lambda b, i: (b, i, 0))
    return pl.pallas_call(
        _post_kernel,
        grid=(bsz, nt),
        in_specs=[tok(w), tok(w), tok(d), pl.BlockSpec((1, 6, d), lambda b, i: (b, 0, 0)),
                  full(wol), full(wom), full(g2)],
        out_specs=[tok(d), pl.BlockSpec((d, TM), lambda b, i: (0, b * nt + i))],
        out_shape=[jax.ShapeDtypeStruct((bsz, s, d), F32),
                   jax.ShapeDtypeStruct((d, bsz * s), BF16)],
        compiler_params=_cparams("arbitrary", "arbitrary"),
        name="post",
    )(yl, ym, x, mod3, wol, wom, g2)


def _sort_pairs(n):
    pairs = []

    def merge(lo, hi, r):
        step = r * 2
        if step < hi - lo:
            merge(lo, hi, step)
            merge(lo + r, hi, step)
            pairs.extend((i, i + r) for i in range(lo + r, hi - r, step))
        else:
            pairs.append((lo, lo + r))

    def sort(lo, hi):
        if hi - lo >= 1:
            mid = lo + (hi - lo) // 2
            sort(lo, mid)
            sort(mid + 1, hi)
            merge(lo, hi, 1)

    sort(0, n - 1)
    return pairs


def _sort_desc(xs):
    xs = list(xs)
    for i, j in _sort_pairs(len(xs)):
        a, b = xs[i], xs[j]
        xs[i], xs[j] = jnp.maximum(a, b), jnp.minimum(a, b)
    return xs


def _merge_top(a, b):
    n = len(a)
    xs = [jnp.maximum(a[i], b[n - 1 - i]) for i in range(n)]
    dist = n // 2
    while dist >= 1:
        for i in range(n):
            if i & dist == 0:
                p, q = xs[i], xs[i + dist]
                xs[i], xs[i + dist] = jnp.maximum(p, q), jnp.minimum(p, q)
        dist //= 2
    return xs


def _kth_largest(vals, k):
    n = 1
    while n < len(vals):
        n *= 2
    present = [True] * len(vals) + [False] * (n - len(vals))
    ops = []
    for i, j in _sort_pairs(n):
        if present[i] and present[j]:
            ops.append(("cmp", i, j))
        elif present[j]:
            ops.append(("mov", i, j))
            present[i], present[j] = True, False
    need = {k}
    live = []
    for op in reversed(ops):
        kind, i, j = op
        if kind == "mov":
            if i in need:
                need.discard(i)
                need.add(j)
                live.append((kind, i, j, True, False))
        else:
            hi, lo = i in need, j in need
            if hi or lo:
                need.update((i, j))
                live.append((kind, i, j, hi, lo))
    xs = list(vals) + [None] * (n - len(vals))
    for kind, i, j, hi, lo in reversed(live):
        a, b = xs[i], xs[j]
        if kind == "mov":
            xs[i], xs[j] = b, None
        else:
            xs[i] = jnp.maximum(a, b) if hi else None
            xs[j] = jnp.minimum(a, b) if lo else None
    return xs[k]


def _top_values(slabs, n):
    groups = [_sort_desc(slabs[g:g + n]) for g in range(0, len(slabs), n)]
    while len(groups) > 1:
        groups = [_merge_top(groups[g], groups[g + 1]) for g in range(0, len(groups), 2)]
    return groups[0]


PACK_ROWS = 2 * SUBLANES


def _pair_bits(x):
    bits = pltpu.bitcast(x.astype(BF16).astype(F32), jnp.uint32)
    return (bits & jnp.uint32(0xFFFF0000)) | (bits >> 16)


def _route_kernel(h2t_ref, wqt_ref, kbig_ref, keys2_ref, c_ref, w1_ref, p_ref):
    tn = h2t_ref.shape[-1]
    nk = PEER_KEYS
    half_rows = PEER_HEADS * PEER_HALF
    qt = _dot(wqt_ref[...], h2t_ref[...])
    sub = lax.broadcasted_iota(jnp.int32, (SUBLANES, tn), 0)

    sc1 = _dot(kbig_ref[...], qt[:half_rows, :].astype(BF16))
    slabs = [sc1[PEER_HEADS * k:PEER_HEADS * (k + 1), :] for k in range(nk)]
    a = _top_values(slabs, PEER_TOPK)

    b = [None] * PEER_TOPK
    for h in range(PEER_HEADS):
        r0 = half_rows + h * PEER_HALF
        sc = _dot(keys2_ref[h], qt[r0:r0 + PEER_HALF, :].astype(BF16))
        tiles = [sc[SUBLANES * t:SUBLANES * (t + 1), :] for t in range(nk // SUBLANES)]
        xs = _sort_desc(tiles)
        shift = SUBLANES // 2
        while shift >= 1:
            xs = _merge_top(xs, [pltpu.roll(v, shift, 0) for v in xs])
            shift //= 2
        for ib in range(nk // PACK_ROWS):
            ranks, weights = [], []
            for tile in tiles[2 * ib:2 * ib + 2]:
                rk = jnp.full_like(tile, float(PEER_TOPK))
                for r in reversed(range(PEER_TOPK)):
                    rk = jnp.where(tile >= xs[r], float(r), rk)
                ranks.append(rk)
                weights.append(jnp.exp(tile - xs[0]))
            rk16 = jnp.concatenate(ranks, axis=0).astype(BF16)
            w16 = jnp.concatenate(weights, axis=0).astype(BF16)
            for lb in range(tn // LANES):
                r0 = 2 * h * PACK_ROWS
                p_ref[lb, ib, r0:r0 + PACK_ROWS, :] = rk16[:, lb * LANES:(lb + 1) * LANES]
                p_ref[lb, ib, r0 + PACK_ROWS:r0 + 2 * PACK_ROWS, :] = w16[:, lb * LANES:(lb + 1) * LANES]
        for r in range(PEER_TOPK):
            b[r] = xs[r] if h == 0 else jnp.where(sub == h, xs[r], b[r])

    pairs = [(i, j) for i in range(PEER_TOPK) for j in range(PEER_TOPK) if (i + 1) * (j + 1) <= PEER_TOPK]
    cands = {ij: a[ij[0]] + b[ij[1]] for ij in pairs}
    tau = _kth_largest(list(cands.values()), PEER_TOPK - 1)
    top = a[0] + b[0]
    z = jnp.zeros_like(tau)
    for c in cands.values():
        z = z + jnp.where(c >= tau, jnp.exp(c - top), 0.0)
    inv_z = 1.0 / z
    need = []
    for j in range(PEER_TOPK):
        t = None
        for i in range(PEER_TOPK):
            if (i, j) in cands:
                v = jnp.where(cands[(i, j)] >= tau, a[i], jnp.inf)
                t = v if t is None else jnp.minimum(t, v)
        need.append(t)
    for k in range(nk):
        cnt = jnp.zeros_like(tau)
        for j in range(PEER_TOPK):
            cnt = jnp.where(slabs[k] >= need[j], float(j + 1), cnt)
        c_ref[k] = _pair_bits(cnt)
        w1_ref[k] = _pair_bits(jnp.exp(slabs[k] - a[0]) * inv_z)


def _route(h2t, wqt, kbig, keys2):
    d, t = h2t.shape
    tn = TN_ROUTE
    by_key = pl.BlockSpec((PEER_KEYS, PEER_HEADS, tn), lambda i: (0, 0, i))
    packed = (PEER_KEYS // PACK_ROWS, 2 * PEER_HEADS * PACK_ROWS, LANES)
    return pl.pallas_call(
        _route_kernel,
        grid=(t // tn,),
        in_specs=[pl.BlockSpec((d, tn), lambda i: (0, i)),
                  pl.BlockSpec(wqt.shape, lambda i: (0, 0)),
                  pl.BlockSpec(kbig.shape, lambda i: (0, 0)),
                  pl.BlockSpec(keys2.shape, lambda i: (0, 0, 0))],
        out_specs=[by_key, by_key, pl.BlockSpec((tn // LANES,) + packed, lambda i: (i, 0, 0, 0))],
        out_shape=[jax.ShapeDtypeStruct((PEER_KEYS, PEER_HEADS, t), jnp.uint32),
                   jax.ShapeDtypeStruct((PEER_KEYS, PEER_HEADS, t), jnp.uint32),
                   jax.ShapeDtypeStruct((t // LANES,) + packed, BF16)],
        compiler_params=_cparams("arbitrary"),
        name="route",
    )(h2t, wqt, kbig, keys2)


def _dense_kernel(h2t_ref, u_ref, vt_ref, c_ref, w1_ref, p_ref, x1_ref, mod_ref,
                  fg_ref, o_ref, acc_ref, act_ref, gat_ref, *, nblk):
    s = pl.program_id(0)
    n_items = pl.num_programs(0) - 2
    tn = h2t_ref.shape[-1]
    eb = u_ref.shape[0]
    rows_per_step = eb // PEER_KEYS

    @pl.when(s == 0)
    def _():
        act_ref[1] = jnp.zeros(act_ref.shape[1:], F32)
        gat_ref[0] = jnp.zeros(gat_ref.shape[1:], BF16)

    jb = jnp.clip(s - 1, 0, n_items - 1) % nblk
    jc = jnp.clip(s - 2, 0, n_items - 1) % nblk

    @pl.when(jc == 0)
    def _():
        acc_ref[...] = jnp.zeros_like(acc_ref)

    n_lb = tn // LANES

    def row_tile(ref, i1, h, ls):
        return pltpu.bitcast(jnp.broadcast_to(ref[i1, h:h + 1, ls], (SUBLANES, LANES)), BF16)

    def gate_group(ig, lb, oth):
        rows = range(ig * GATE_ROWS, (ig + 1) * GATE_ROWS)
        ls = slice(lb * LANES, (lb + 1) * LANES)
        cntb = [[row_tile(c_ref, jb * rows_per_step + il, h, ls) for h in range(PEER_HEADS)] for il in rows]
        w1b = [[row_tile(w1_ref, jb * rows_per_step + il, h, ls) for h in range(PEER_HEADS)] for il in rows]
        for ib in range(PEER_KEYS // PACK_ROWS):
            g = [None] * GATE_ROWS
            for h in range(PEER_HEADS):
                r0 = 2 * h * PACK_ROWS
                rank2 = p_ref[lb, ib, r0:r0 + PACK_ROWS, :]
                w2 = p_ref[lb, ib, r0 + PACK_ROWS:r0 + 2 * PACK_ROWS, :]
                for k in range(GATE_ROWS):
                    term = jnp.minimum(jnp.maximum(cntb[k][h] - rank2, 0.0), w2) * w1b[k][h]
                    g[k] = term if g[k] is None else g[k] + term
            for k, il in enumerate(rows):
                r1 = il * PEER_KEYS + ib * PACK_ROWS
                act = act_ref[oth, r1:r1 + PACK_ROWS, ls]
                gat_ref[oth, r1:r1 + PACK_ROWS, ls] = jax.nn.gelu(act.astype(BF16)) * g[k]

    def step(cur, oth):
        groups = [(ig, lb) for ig in range(rows_per_step // GATE_ROWS) for lb in range(n_lb)]
        for gi, group in enumerate(groups):
            if gi == len(groups) // 2 - 1:
                acc_ref[...] += _dot(vt_ref[0], gat_ref[cur])
            if gi == len(groups) - 1:
                act_ref[cur] = _dot(u_ref[...], h2t_ref[...])
            gate_group(*group, oth)

    for parity in range(2):
        @pl.when(s % 2 == parity)
        def _(parity=parity):
            step(parity, 1 - parity)

    @pl.when(jnp.logical_and(jc == nblk - 1, s >= 2))
    def _():
        x2 = x1_ref[...] + mod_ref[0, 5:6, :] * acc_ref[...].T
        o_ref[...] = _rms(x2, fg_ref[...])


def _dense(h2t, u, vt3, cnt, w1, packed, x1, mod3, fg, seq):
    d, t = h2t.shape
    ne = vt3.shape[0]
    per_seq = seq // TN
    n_items = (t // TN) * ne
    item = lambda s, lag: jnp.clip(s - lag, 0, n_items - 1)
    tile = lambda s, lag: item(s, lag) // ne
    block = lambda s, lag: item(s, lag) % ne
    by_key = pl.BlockSpec((PEER_KEYS, PEER_HEADS, TN), lambda s: (0, 0, tile(s, 1)))
    return pl.pallas_call(
        functools.partial(_dense_kernel, nblk=ne),
        grid=(n_items + 2,),
        in_specs=[pl.BlockSpec((d, TN), lambda s: (0, tile(s, 0))),
                  pl.BlockSpec((EB, d), lambda s: (block(s, 0), 0)),
                  pl.BlockSpec((1, d, EB), lambda s: (block(s, 2), 0, 0)),
                  by_key, by_key,
                  pl.BlockSpec((TN // LANES,) + packed.shape[1:], lambda s: (tile(s, 1), 0, 0, 0)),
                  pl.BlockSpec((TN, d), lambda s: (tile(s, 2), 0)),
                  pl.BlockSpec((1, 6, d), lambda s: (tile(s, 2) // per_seq, 0, 0)),
                  pl.BlockSpec(fg.shape, lambda s: (0, 0))],
        out_specs=pl.BlockSpec((TN, d), lambda s: (tile(s, 2), 0)),
        out_shape=jax.ShapeDtypeStruct((t, d), F32),
        scratch_shapes=[pltpu.VMEM((d, TN), F32), pltpu.VMEM((2, EB, TN), F32),
                        pltpu.VMEM((2, EB, TN), BF16)],
        compiler_params=_cparams("arbitrary"),
        name="dense",
    )(h2t, u, vt3, cnt, w1, packed, x1, mod3, fg)


def _rot_half_cols(w):
    half = w.shape[-1] // 2
    return jnp.concatenate([-w[..., half:], w[..., :half]], axis=-1)


def _pad_cols(w, before, total):
    return jnp.pad(w, [(0, 0)] * (w.ndim - 1) + [(before, total - before - w.shape[-1])])


def kernel(x, c, positions, w_ada, b_ada, norm1_g, w_in, conv_w, conv_b, lru_wa, lru_ba, lru_wx, lru_bx,
           lru_lambda, q_norm_g, w_uq, kv_norm_g, w_ukv, lru_out_g, mla_out_g, w_out, norm2_g, peer_wq,
           peer_keys, peer_u, peer_v, final_g):
    bsz, seq, d = x.shape
    depth = w_ada.shape[0]
    lw = d // 2
    qr = w_uq.shape[1]
    kvr = w_ukv.shape[1]
    assert depth == 1, "the final norm is fused into the (single) layer's PEER kernel"
    assert seq % TM == 0 and seq % TN == 0 and seq % TQ == 0 and seq % TS == 0
    assert d == MLA_HEADS * LANES and lw == MLA_HEADS * V_DIM

    inv_freq = 1.0 / (ROPE_THETA ** (jnp.arange(0, QK_ROPE, 2, dtype=F32) / QK_ROPE))
    invf = _pad_cols(jnp.concatenate([inv_freq, inv_freq])[None, :], QK_NOPE, LANES)
    pos3 = positions.reshape(bsz, seq, 1)
    row = lambda v: v.reshape(1, -1)

    for l in range(depth):
        mod3 = _ada(c, w_ada[l], b_ada[l]).reshape(bsz, 6, d)

        wi = w_in[l]
        o0, o1, o2, o3 = 2 * lw, 2 * lw + qr, 2 * lw + qr + kvr, 2 * lw + qr + kvr + QK_ROPE
        wkr = wi[:, o2:o3]
        wkr2 = jnp.concatenate([_pad_cols(wkr, QK_NOPE, LANES),
                                _pad_cols(_rot_half_cols(wkr), QK_NOPE, LANES)], axis=1).astype(BF16)
        uq = w_uq[l].reshape(qr, MLA_HEADS, QK_NOPE + QK_ROPE)
        uq_main = _pad_cols(uq, 0, LANES).reshape(qr, d)
        uq_rot = _pad_cols(_rot_half_cols(uq[..., QK_NOPE:]), QK_NOPE, LANES).reshape(qr, d)
        wuq2 = jnp.concatenate([uq_main, uq_rot], axis=1).astype(BF16)
        ukv = w_ukv[l].reshape(kvr, MLA_HEADS, QK_NOPE + V_DIM)
        wuk = _pad_cols(ukv[..., :QK_NOPE], 0, LANES).reshape(kvr, d).astype(BF16)
        uv = ukv[..., QK_NOPE:]
        odd = (jnp.arange(MLA_HEADS) % 2 == 1)[None, :, None]
        wuv = jnp.where(odd, _pad_cols(uv, V_DIM, LANES), _pad_cols(uv, 0, LANES)).reshape(kvr, d).astype(BF16)
        half = jnp.arange(LANES)[None, :] >= V_DIM
        vones = (half != odd[0]).astype(F32).reshape(1, d)

        lru_in, q, k, v = _pre(x, mod3, row(norm1_g[l]), pos3, invf, wi[:, :o0].astype(BF16),
                               wi[:, o0:o1].astype(BF16), wi[:, o1:o2].astype(BF16), wkr2,
                               row(q_norm_g[l]), wuq2, row(kv_norm_g[l]), wuk, wuv, vones)

        eye = jnp.eye(LRU_BLOCKS, dtype=F32)
        blockdiag = lambda w: jnp.einsum("hij,hg->higj", w, eye).reshape(lw, lw)
        wg = jnp.concatenate([blockdiag(lru_wa[l]), blockdiag(lru_wx[l])], axis=1).astype(BF16)
        bg = jnp.concatenate([lru_ba[l], lru_bx[l]])[None, :]
        yl = _lru(lru_in, conv_w[l], row(conv_b[l]), wg, bg, row(lru_lambda[l]), row(lru_out_g[l]))

        ym = _attn(q, k, v, row(mla_out_g[l]))

        wo = w_out[l].astype(BF16)
        x1, h2t = _post(yl, ym, x, mod3, wo[:lw], wo[lw:], row(norm2_g[l]))

        wqt = (peer_wq[l].reshape(d, PEER_HEADS, 2, PEER_HALF).transpose(2, 1, 3, 0)
               .reshape(2 * PEER_HEADS * PEER_HALF, d).astype(BF16))
        keys = peer_keys[l].astype(BF16)
        kbig = jnp.einsum("hkd,hg->khgd", keys[:, 0], jnp.eye(PEER_HEADS, dtype=BF16)).reshape(
            PEER_KEYS * PEER_HEADS, PEER_HEADS * PEER_HALF)
        cnt, w1, packed = _route(h2t, wqt, kbig, keys[:, 1])

        ne = peer_u.shape[1] // EB
        vt3 = peer_v[l].astype(BF16).reshape(ne, EB, d).transpose(0, 2, 1)
        out = _dense(h2t, peer_u[l].astype(BF16), vt3, cnt, w1, packed,
                     x1.reshape(bsz * seq, d), mod3, row(final_g), seq)
        x = out.reshape(bsz, seq, d)
    return x
```

```python
import functools

import jax
import jax.numpy as jnp
from jax import lax
from jax.experimental import pallas as pl
from jax.experimental.pallas import tpu as pltpu

F32 = jnp.float32
BF16 = jnp.bfloat16

LRU_BLOCKS = 8
CONV_WIDTH = 4
LRU_C = 8.0
MLA_HEADS = 8
QK_NOPE = 64
QK_ROPE = 32
V_DIM = 64
ROPE_THETA = 10000.0
PEER_HEADS = 8
PEER_KEYS = 128
PEER_HALF = 128
PEER_TOPK = 16
EPS = 1e-6
LOG2_E = 1.4426950408889634

LANES = 128
SUBLANES = 8
VMEM_LIMIT = 56 * 1024 * 1024

TM = 1024
TS = 512
TQ = 512
TN_ROUTE = 512
TN = 512
EB = 2048
GATE_ROWS = 2


def _cparams(*sem):
    return pltpu.CompilerParams(dimension_semantics=sem, vmem_limit_bytes=VMEM_LIMIT)


def _rms(x, g):
    return x * lax.rsqrt(jnp.mean(x * x, axis=-1, keepdims=True) + EPS) * g


def _dot(a, b):
    return jnp.dot(a, b, preferred_element_type=F32)


def _ada_kernel(c_ref, w_ref, b_ref, o_ref):
    ca = jax.nn.silu(c_ref[...])
    o_ref[...] = jnp.dot(ca, w_ref[...], preferred_element_type=F32,
                         precision=lax.Precision.HIGHEST) + b_ref[...]


def _ada(c, w, b):
    bsz, d = c.shape
    n = w.shape[1]
    return pl.pallas_call(
        _ada_kernel,
        grid=(n // d,),
        in_specs=[pl.BlockSpec((bsz, d), lambda j: (0, 0)),
                  pl.BlockSpec((d, d), lambda j: (0, j)),
                  pl.BlockSpec((1, d), lambda j: (0, j))],
        out_specs=pl.BlockSpec((bsz, d), lambda j: (0, j)),
        out_shape=jax.ShapeDtypeStruct((bsz, n), F32),
        compiler_params=_cparams("arbitrary"),
        name="ada",
    )(c, w, b.reshape(1, n))


def _pre_kernel(x_ref, mod_ref, g1_ref, pos_ref, invf_ref, wlru_ref, wq_ref, wkv_ref, wkr_ref,
                qg_ref, wuq_ref, kvg_ref, wuk_ref, wuv_ref, vones_ref, lru_ref, q_ref, k_ref, v_ref):
    d = x_ref.shape[-1]
    x = x_ref[0]
    shift = mod_ref[0, 0:1, :]
    scale = mod_ref[0, 1:2, :]
    hb = (_rms(x, g1_ref[...]) * (1.0 + scale) + shift).astype(BF16)
    lru_ref[0] = _dot(hb, wlru_ref[...])

    ang = pos_ref[0].astype(F32) * invf_ref[...]
    cos = jnp.cos(ang)
    sin = jnp.sin(ang)

    qn = _rms(_dot(hb, wq_ref[...]), qg_ref[...]).astype(BF16)
    q2 = _dot(qn, wuq_ref[...])
    q_scale = (QK_NOPE + QK_ROPE) ** -0.5 * LOG2_E
    for h in range(MLA_HEADS):
        lo = h * LANES
        q_ref[0, :, lo:lo + LANES] = ((q2[:, lo:lo + LANES] * cos
                                       + q2[:, d + lo:d + lo + LANES] * sin) * q_scale).astype(BF16)

    kvn = _rms(_dot(hb, wkv_ref[...]), kvg_ref[...]).astype(BF16)
    kn = _dot(kvn, wuk_ref[...])
    v_ref[0] = (_dot(kvn, wuv_ref[...]) + vones_ref[...]).astype(BF16)
    kr2 = _dot(hb, wkr_ref[...])
    krot = kr2[:, :LANES] * cos + kr2[:, LANES:] * sin
    for h in range(MLA_HEADS):
        lo = h * LANES
        k_ref[0, :, lo:lo + LANES] = (kn[:, lo:lo + LANES] + krot).astype(BF16)


def _pre(x, mod3, g1, pos3, invf, wlru, wq, wkv, wkr2, qg, wuq2, kvg, wuk, wuv, vones):
    bsz, s, d = x.shape
    full = lambda a: pl.BlockSpec(a.shape, lambda b, i: (0,) * a.ndim)
    tok = lambda w: pl.BlockSpec((1, TM, w), lambda b, i: (b, i, 0))
    return pl.pallas_call(
        _pre_kernel,
        grid=(bsz, s // TM),
        in_specs=[tok(d), pl.BlockSpec((1, 6, d), lambda b, i: (b, 0, 0)), full(g1), tok(1), full(invf),
                  full(wlru), full(wq), full(wkv), full(wkr2), full(qg), full(wuq2), full(kvg),
                  full(wuk), full(wuv), full(vones)],
        out_specs=[tok(d), tok(d), tok(d), tok(d)],
        out_shape=[jax.ShapeDtypeStruct((bsz, s, d), F32),
                   jax.ShapeDtypeStruct((bsz, s, d), BF16),
                   jax.ShapeDtypeStruct((bsz, s, d), BF16),
                   jax.ShapeDtypeStruct((bsz, s, d), BF16)],
        compiler_params=_cparams("arbitrary", "arbitrary"),
        name="pre",
    )(x, mod3, g1, pos3, invf, wlru, wq, wkv, wkr2, qg, wuq2, kvg, wuk, wuv, vones)


def _lru_kernel(x_ref, cw_ref, cb_ref, wg_ref, bg_ref, lam_ref, og_ref, o_ref, prev_ref, h_ref):
    w = o_ref.shape[-1]

    @pl.when(pl.program_id(1) == 0)
    def _():
        prev_ref[...] = jnp.zeros_like(prev_ref)
        h_ref[...] = jnp.zeros_like(h_ref)

    x = x_ref[0, :, :w]
    gate = x_ref[0, :, w:]
    tail = prev_ref[...]
    row = lax.broadcasted_iota(jnp.int32, x.shape, 0)
    row8 = lax.broadcasted_iota(jnp.int32, tail.shape, 0)

    xc = cb_ref[...]
    for k in range(CONV_WIDTH - 1):
        back = CONV_WIDTH - 1 - k
        shifted = pltpu.roll(x, back, 0)
        first = jnp.where(row8 < back, pltpu.roll(tail, back, 0), shifted[:SUBLANES, :])
        shifted = jnp.concatenate([first, shifted[SUBLANES:, :]], axis=0)
        xc = xc + cw_ref[k:k + 1, :] * shifted
    xc = xc + cw_ref[CONV_WIDTH - 1:CONV_WIDTH, :] * x
    prev_ref[...] = x[TS - SUBLANES:, :]

    pre = _dot(xc.astype(BF16), wg_ref[...]) + bg_ref[...]
    r = jax.nn.sigmoid(pre[:, :w])
    i = jax.nn.sigmoid(pre[:, w:])
    log_a = -LRU_C * r * jax.nn.softplus(-lam_ref[...])
    a = jnp.exp(log_a)
    one_minus = 1.0 - a * a
    root = jnp.where(one_minus > 0.0, one_minus * lax.rsqrt(one_minus), 0.0)
    b = root * (i * xc)

    sub = row % SUBLANES
    dist = 1
    while dist < SUBLANES:
        keep = sub >= dist
        b = jnp.where(keep, a * pltpu.roll(b, dist, 0) + b, b)
        a = jnp.where(keep, a * pltpu.roll(a, dist, 0), a)
        dist *= 2
    state = h_ref[0:1, :]
    groups = []
    for g0 in range(0, TS, SUBLANES):
        hg = b[g0:g0 + SUBLANES, :] + a[g0:g0 + SUBLANES, :] * state
        groups.append(hg)
        state = hg[SUBLANES - 1:SUBLANES, :]
    h = jnp.concatenate(groups, axis=0)
    h_ref[...] = jnp.broadcast_to(state, h_ref.shape)

    y = jax.nn.gelu(gate) * h
    o_ref[0] = _rms(y, og_ref[...]).astype(BF16)


def _lru(lru_in, cw, cb, wg, bg, lam, og):
    bsz, s, d = lru_in.shape
    w = d // 2
    full = lambda a: pl.BlockSpec(a.shape, lambda b, t: (0,) * a.ndim)
    return pl.pallas_call(
        _lru_kernel,
        grid=(bsz, s // TS),
        in_specs=[pl.BlockSpec((1, TS, d), lambda b, t: (b, t, 0)),
                  full(cw), full(cb), full(wg), full(bg), full(lam), full(og)],
        out_specs=pl.BlockSpec((1, TS, w), lambda b, t: (b, t, 0)),
        out_shape=jax.ShapeDtypeStruct((bsz, s, w), BF16),
        scratch_shapes=[pltpu.VMEM((SUBLANES, w), F32), pltpu.VMEM((SUBLANES, w), F32)],
        compiler_params=_cparams("arbitrary", "arbitrary"),
        name="lru",
    )(lru_in, cw, cb, wg, bg, lam, og)


def _attn_kernel(q_ref, k_ref, v_ref, g_ref, o_ref, m_ref, acc_ref, bias_ref):
    i = pl.program_id(1)
    nt = (((1,), (1,)), ((), ()))
    row = lax.broadcasted_iota(jnp.int32, (TQ, TQ), 0)
    col = lax.broadcasted_iota(jnp.int32, (TQ, TQ), 1)
    lane = lax.broadcasted_iota(jnp.int32, (TQ, LANES), 1)

    m_ref[...] = jnp.full(m_ref.shape, -jnp.inf, F32)
    acc_ref[...] = jnp.zeros(acc_ref.shape, F32)
    bias_ref[...] = jnp.where(col <= row, 0.0, -jnp.inf)

    def block(j, masked):
        start = pl.multiple_of(j * TQ, TQ)

        def scores(h):
            lo = h * LANES
            s = lax.dot_general(q_ref[0, :, lo:lo + LANES], k_ref[0, pl.ds(start, TQ), lo:lo + LANES], nt,
                                preferred_element_type=F32)
            return s + bias_ref[...] if masked else s

        def softmax_step(h, s):
            m_old = m_ref[h]
            m_new = jnp.maximum(m_old, jnp.max(s, axis=-1, keepdims=True))
            m_ref[h] = m_new
            alpha = jnp.exp2(m_old - m_new)
            p = [jnp.exp2(s[:, c:c + LANES] - m_new) for c in range(0, TQ, LANES)]
            return alpha, jnp.concatenate(p, axis=-1).astype(BF16)

        def weighted_values(h, alpha, p):
            lo = h * LANES
            acc_ref[h] = alpha * acc_ref[h] + _dot(p, v_ref[0, pl.ds(start, TQ), lo:lo + LANES])

        pending_s, pending_p = {}, {}
        for t in range(MLA_HEADS + 2):
            if t < MLA_HEADS:
                pending_s[t] = scores(t)
            if 1 <= t <= MLA_HEADS:
                pending_p[t - 1] = softmax_step(t - 1, pending_s.pop(t - 1))
            if t >= 2:
                weighted_values(t - 2, *pending_p.pop(t - 2))

    def unmasked(j, carry):
        block(j, False)
        return carry

    def diagonal(j, carry):
        block(j, True)
        return carry

    lax.fori_loop(0, i, unmasked, 0)
    lax.fori_loop(i, i + 1, diagonal, 0)

    pairs = []
    for hp in range(MLA_HEADS // 2):
        even, odd = acc_ref[2 * hp], acc_ref[2 * hp + 1]
        num = jnp.where(lane < V_DIM, even, odd)
        den = jnp.where(lane < V_DIM, pltpu.roll(even, V_DIM, 1), pltpu.roll(odd, V_DIM, 1))
        pairs.append(num / den)
    y = jnp.concatenate(pairs, axis=-1)
    o_ref[0] = _rms(y, g_ref[...]).astype(BF16)


def _attn(q, k, v, g):
    bsz, s, d = q.shape
    w = g.shape[-1]
    return pl.pallas_call(
        _attn_kernel,
        grid=(bsz, s // TQ),
        in_specs=[pl.BlockSpec((1, TQ, d), lambda b, i: (b, i, 0)),
                  pl.BlockSpec((1, s, d), lambda b, i: (b, 0, 0)),
                  pl.BlockSpec((1, s, d), lambda b, i: (b, 0, 0)),
                  pl.BlockSpec(g.shape, lambda b, i: (0, 0))],
        out_specs=pl.BlockSpec((1, TQ, w), lambda b, i: (b, i, 0)),
        out_shape=jax.ShapeDtypeStruct((bsz, s, w), BF16),
        scratch_shapes=[pltpu.VMEM((MLA_HEADS, TQ, LANES), F32), pltpu.VMEM((MLA_HEADS, TQ, LANES), F32),
                        pltpu.VMEM((TQ, TQ), F32)],
        compiler_params=_cparams("arbitrary", "arbitrary"),
        name="attn",
    )(q, k, v, g)


def _post_kernel(yl_ref, ym_ref, x_ref, mod_ref, wol_ref, wom_ref, g2_ref, x1_ref, h2t_ref):
    mix = _dot(yl_ref[0], wol_ref[...]) + _dot(ym_ref[0], wom_ref[...])
    x1 = x_ref[0] + mod_ref[0, 2:3, :] * mix
    x1_ref[0] = x1
    h2 = _rms(x1, g2_ref[...]) * (1.0 + mod_ref[0, 4:5, :]) + mod_ref[0, 3:4, :]
    h2t_ref[...] = h2.T.astype(BF16)


def _post(yl, ym, x, mod3, wol, wom, g2):
    bsz, s, d = x.shape
    w = yl.shape[-1]
    nt = s // TM
    full = lambda a: pl.BlockSpec(a.shape, lambda b, i: (0,) * a.ndim)
    tok = lambda c: pl.BlockSpec((1, TM, c), lambda b, i: (b, i, 0))
    return pl.pallas_call(
        _post_kernel,
        grid=(bsz, nt),
        in_specs=[tok(w), tok(w), tok(d), pl.BlockSpec((1, 6, d), lambda b, i: (b, 0, 0)),
                  full(wol), full(wom), full(g2)],
        out_specs=[tok(d), pl.BlockSpec((d, TM), lambda b, i: (0, b * nt + i))],
        out_shape=[jax.ShapeDtypeStruct((bsz, s, d), F32),
                   jax.ShapeDtypeStruct((d, bsz * s), BF16)],
        compiler_params=_cparams("arbitrary", "arbitrary"),
        name="post",
    )(yl, ym, x, mod3, wol, wom, g2)


def _sort_pairs(n):
    pairs = []

    def merge(lo, hi, r):
        step = r * 2
        if step < hi - lo:
            merge(lo, hi, step)
            merge(lo + r, hi, step)
            pairs.extend((i, i + r) for i in range(lo + r, hi - r, step))
        else:
            pairs.append((lo, lo + r))

    def sort(lo, hi):
        if hi - lo >= 1:
            mid = lo + (hi - lo) // 2
            sort(lo, mid)
            sort(mid + 1, hi)
            merge(lo, hi, 1)

    sort(0, n - 1)
    return pairs


def _sort_desc(xs):
    xs = list(xs)
    for i, j in _sort_pairs(len(xs)):
        a, b = xs[i], xs[j]
        xs[i], xs[j] = jnp.maximum(a, b), jnp.minimum(a, b)
    return xs


def _merge_top(a, b):
    n = len(a)
    xs = [jnp.maximum(a[i], b[n - 1 - i]) for i in range(n)]
    dist = n // 2
    while dist >= 1:
        for i in range(n):
            if i & dist == 0:
                p, q = xs[i], xs[i + dist]
                xs[i], xs[i + dist] = jnp.maximum(p, q), jnp.minimum(p, q)
        dist //= 2
    return xs


def _kth_largest(vals, k):
    n = 1
    while n < len(vals):
        n *= 2
    present = [True] * len(vals) + [False] * (n - len(vals))
    ops = []
    for i, j in _sort_pairs(n):
        if present[i] and present[j]:
            ops.append(("cmp", i, j))
        elif present[j]:
            ops.append(("mov", i, j))
            present[i], present[j] = True, False
    need = {k}
    live = []
    for op in reversed(ops):
        kind, i, j = op
        if kind == "mov":
            if i in need:
                need.discard(i)
                need.add(j)
                live.append((kind, i, j, True, False))
        else:
            hi, lo = i in need, j in need
            if hi or lo:
                need.update((i, j))
                live.append((kind, i, j, hi, lo))
    xs = list(vals) + [None] * (n - len(vals))
    for kind, i, j, hi, lo in reversed(live):
        a, b = xs[i], xs[j]
        if kind == "mov":
            xs[i], xs[j] = b, None
        else:
            xs[i] = jnp.maximum(a, b) if hi else None
            xs[j] = jnp.minimum(a, b) if lo else None
    return xs[k]


def _top_values(slabs, n):
    groups = [_sort_desc(slabs[g:g + n]) for g in range(0, len(slabs), n)]
    while len(groups) > 1:
        groups = [_merge_top(groups[g], groups[g + 1]) for g in range(0, len(groups), 2)]
    return groups[0]


PACK_ROWS = 2 * SUBLANES


def _pair_bits(x):
    bits = pltpu.bitcast(x.astype(BF16).astype(F32), jnp.uint32)
    return (bits & jnp.uint32(0xFFFF0000)) | (bits >> 16)


def _route_kernel(h2t_ref, wqt_ref, kbig_ref, keys2_ref, c_ref, w1_ref, p_ref):
    tn = h2t_ref.shape[-1]
    nk = PEER_KEYS
    half_rows = PEER_HEADS * PEER_HALF
    qt = _dot(wqt_ref[...], h2t_ref[...])
    sc1 = _dot(kbig_ref[...], qt[:half_rows, :].astype(BF16))
    sc2 = [_dot(keys2_ref[h], qt[half_rows + h * PEER_HALF:half_rows + (h + 1) * PEER_HALF, :].astype(BF16))
           for h in range(PEER_HEADS)]
    for lb in range(tn // LANES):
        _route_lane_block(sc1, sc2, lb, c_ref, w1_ref, p_ref)


def _route_lane_block(sc1, sc2, lb, c_ref, w1_ref, p_ref):
    nk = PEER_KEYS
    ls = slice(lb * LANES, (lb + 1) * LANES)
    sub = lax.broadcasted_iota(jnp.int32, (SUBLANES, LANES), 0)

    slabs = [sc1[PEER_HEADS * k:PEER_HEADS * (k + 1), ls] for k in range(nk)]
    a = _top_values(slabs, PEER_TOPK)

    b = [None] * PEER_TOPK
    for h in range(PEER_HEADS):
        tiles = [sc2[h][SUBLANES * t:SUBLANES * (t + 1), ls] for t in range(nk // SUBLANES)]
        xs = _sort_desc(tiles)
        shift = SUBLANES // 2
        while shift >= 1:
            xs = _merge_top(xs, [pltpu.roll(v, shift, 0) for v in xs])
            shift //= 2
        for ib in range(nk // PACK_ROWS):
            ranks, weights = [], []
            for tile in tiles[2 * ib:2 * ib + 2]:
                rk = jnp.full_like(tile, float(PEER_TOPK))
                for r in reversed(range(PEER_TOPK)):
                    rk = jnp.where(tile >= xs[r], float(r), rk)
                ranks.append(rk)
                weights.append(jnp.exp(tile - xs[0]))
            r0 = 2 * h * PACK_ROWS
            p_ref[lb, ib, r0:r0 + PACK_ROWS, :] = jnp.concatenate(ranks, axis=0).astype(BF16)
            p_ref[lb, ib, r0 + PACK_ROWS:r0 + 2 * PACK_ROWS, :] = jnp.concatenate(weights, axis=0).astype(BF16)
        for r in range(PEER_TOPK):
            b[r] = xs[r] if h == 0 else jnp.where(sub == h, xs[r], b[r])

    pairs = [(i, j) for i in range(PEER_TOPK) for j in range(PEER_TOPK) if (i + 1) * (j + 1) <= PEER_TOPK]
    cands = {ij: a[ij[0]] + b[ij[1]] for ij in pairs}
    tau = _kth_largest(list(cands.values()), PEER_TOPK - 1)
    top = a[0] + b[0]
    z = jnp.zeros_like(tau)
    for c in cands.values():
        z = z + jnp.where(c >= tau, jnp.exp(c - top), 0.0)
    inv_z = 1.0 / z
    need = []
    for j in range(PEER_TOPK):
        t = None
        for i in range(PEER_TOPK):
            if (i, j) in cands:
                v = jnp.where(cands[(i, j)] >= tau, a[i], jnp.inf)
                t = v if t is None else jnp.minimum(t, v)
        need.append(t)
    for k in range(nk):
        cnt = jnp.zeros_like(tau)
        for j in range(PEER_TOPK):
            cnt = jnp.where(slabs[k] >= need[j], float(j + 1), cnt)
        c_ref[k, :, ls] = _pair_bits(cnt)
        w1_ref[k, :, ls] = _pair_bits(jnp.exp(slabs[k] - a[0]) * inv_z)


def _route(h2t, wqt, kbig, keys2):
    d, t = h2t.shape
    tn = TN_ROUTE
    by_key = pl.BlockSpec((PEER_KEYS, PEER_HEADS, tn), lambda i: (0, 0, i))
    packed = (PEER_KEYS // PACK_ROWS, 2 * PEER_HEADS * PACK_ROWS, LANES)
    return pl.pallas_call(
        _route_kernel,
        grid=(t // tn,),
        in_specs=[pl.BlockSpec((d, tn), lambda i: (0, i)),
                  pl.BlockSpec(wqt.shape, lambda i: (0, 0)),
                  pl.BlockSpec(kbig.shape, lambda i: (0, 0)),
                  pl.BlockSpec(keys2.shape, lambda i: (0, 0, 0))],
        out_specs=[by_key, by_key, pl.BlockSpec((tn // LANES,) + packed, lambda i: (i, 0, 0, 0))],
        out_shape=[jax.ShapeDtypeStruct((PEER_KEYS, PEER_HEADS, t), jnp.uint32),
                   jax.ShapeDtypeStruct((PEER_KEYS, PEER_HEADS, t), jnp.uint32),
                   jax.ShapeDtypeStruct((t // LANES,) + packed, BF16)],
        compiler_params=_cparams("arbitrary"),
        name="route",
    )(h2t, wqt, kbig, keys2)


def _dense_kernel(h2t_ref, u_ref, vt_ref, c_ref, w1_ref, p_ref, x1_ref, mod_ref,
                  fg_ref, o_ref, acc_ref, act_ref, gat_ref, *, nblk):
    s = pl.program_id(0)
    n_items = pl.num_programs(0) - 2
    tn = h2t_ref.shape[-1]
    eb = u_ref.shape[0]
    rows_per_step = eb // PEER_KEYS

    @pl.when(s == 0)
    def _():
        act_ref[1] = jnp.zeros(act_ref.shape[1:], F32)
        gat_ref[0] = jnp.zeros(gat_ref.shape[1:], BF16)

    jb = jnp.clip(s - 1, 0, n_items - 1) % nblk
    jc = jnp.clip(s - 2, 0, n_items - 1) % nblk

    @pl.when(jc == 0)
    def _():
        acc_ref[...] = jnp.zeros_like(acc_ref)

    n_lb = tn // LANES

    def row_tile(ref, i1, h, ls):
        return pltpu.bitcast(jnp.broadcast_to(ref[i1, h:h + 1, ls], (SUBLANES, LANES)), BF16)

    def gate_group(ig, lb, oth):
        rows = range(ig * GATE_ROWS, (ig + 1) * GATE_ROWS)
        ls = slice(lb * LANES, (lb + 1) * LANES)
        cntb = [[row_tile(c_ref, jb * rows_per_step + il, h, ls) for h in range(PEER_HEADS)] for il in rows]
        w1b = [[row_tile(w1_ref, jb * rows_per_step + il, h, ls) for h in range(PEER_HEADS)] for il in rows]
        for ib in range(PEER_KEYS // PACK_ROWS):
            g = [None] * GATE_ROWS
            for h in range(PEER_HEADS):
                r0 = 2 * h * PACK_ROWS
                rank2 = p_ref[lb, ib, r0:r0 + PACK_ROWS, :]
                w2 = p_ref[lb, ib, r0 + PACK_ROWS:r0 + 2 * PACK_ROWS, :]
                for k in range(GATE_ROWS):
                    term = jnp.minimum(jnp.maximum(cntb[k][h] - rank2, 0.0), w2) * w1b[k][h]
                    g[k] = term if g[k] is None else g[k] + term
            for k, il in enumerate(rows):
                r1 = il * PEER_KEYS + ib * PACK_ROWS
                act = act_ref[oth, r1:r1 + PACK_ROWS, ls]
                gat_ref[oth, r1:r1 + PACK_ROWS, ls] = jax.nn.gelu(act.astype(BF16)) * g[k]

    def step(cur, oth):
        groups = [(ig, lb) for ig in range(rows_per_step // GATE_ROWS) for lb in range(n_lb)]
        for gi, group in enumerate(groups):
            if gi == len(groups) // 2 - 1:
                acc_ref[...] += _dot(vt_ref[0], gat_ref[cur])
            if gi == len(groups) - 1:
                act_ref[cur] = _dot(u_ref[...], h2t_ref[...])
            gate_group(*group, oth)

    for parity in range(2):
        @pl.when(s % 2 == parity)
        def _(parity=parity):
            step(parity, 1 - parity)

    @pl.when(jnp.logical_and(jc == nblk - 1, s >= 2))
    def _():
        x2 = x1_ref[...] + mod_ref[0, 5:6, :] * acc_ref[...].T
        o_ref[...] = _rms(x2, fg_ref[...])


def _dense(h2t, u, vt3, cnt, w1, packed, x1, mod3, fg, seq):
    d, t = h2t.shape
    ne = vt3.shape[0]
    per_seq = seq // TN
    n_items = (t // TN) * ne
    item = lambda s, lag: jnp.clip(s - lag, 0, n_items - 1)
    tile = lambda s, lag: item(s, lag) // ne
    block = lambda s, lag: item(s, lag) % ne
    by_key = pl.BlockSpec((PEER_KEYS, PEER_HEADS, TN), lambda s: (0, 0, tile(s, 1)))
    return pl.pallas_call(
        functools.partial(_dense_kernel, nblk=ne),
        grid=(n_items + 2,),
        in_specs=[pl.BlockSpec((d, TN), lambda s: (0, tile(s, 0))),
                  pl.BlockSpec((EB, d), lambda s: (block(s, 0), 0)),
                  pl.BlockSpec((1, d, EB), lambda s: (block(s, 2), 0, 0)),
                  by_key, by_key,
                  pl.BlockSpec((TN // LANES,) + packed.shape[1:], lambda s: (tile(s, 1), 0, 0, 0)),
                  pl.BlockSpec((TN, d), lambda s: (tile(s, 2), 0)),
                  pl.BlockSpec((1, 6, d), lambda s: (tile(s, 2) // per_seq, 0, 0)),
                  pl.BlockSpec(fg.shape, lambda s: (0, 0))],
        out_specs=pl.BlockSpec((TN, d), lambda s: (tile(s, 2), 0)),
        out_shape=jax.ShapeDtypeStruct((t, d), F32),
        scratch_shapes=[pltpu.VMEM((d, TN), F32), pltpu.VMEM((2, EB, TN), F32),
                        pltpu.VMEM((2, EB, TN), BF16)],
        compiler_params=_cparams("arbitrary"),
        name="dense",
    )(h2t, u, vt3, cnt, w1, packed, x1, mod3, fg)


def _rot_half_cols(w):
    half = w.shape[-1] // 2
    return jnp.concatenate([-w[..., half:], w[..., :half]], axis=-1)


def _pad_cols(w, before, total):
    return jnp.pad(w, [(0, 0)] * (w.ndim - 1) + [(before, total - before - w.shape[-1])])


def kernel(x, c, positions, w_ada, b_ada, norm1_g, w_in, conv_w, conv_b, lru_wa, lru_ba, lru_wx, lru_bx,
           lru_lambda, q_norm_g, w_uq, kv_norm_g, w_ukv, lru_out_g, mla_out_g, w_out, norm2_g, peer_wq,
           peer_keys, peer_u, peer_v, final_g):
    bsz, seq, d = x.shape
    depth = w_ada.shape[0]
    lw = d // 2
    qr = w_uq.shape[1]
    kvr = w_ukv.shape[1]
    assert depth == 1, "the final norm is fused into the (single) layer's PEER kernel"
    assert seq % TM == 0 and seq % TN == 0 and seq % TQ == 0 and seq % TS == 0
    assert d == MLA_HEADS * LANES and lw == MLA_HEADS * V_DIM

    inv_freq = 1.0 / (ROPE_THETA ** (jnp.arange(0, QK_ROPE, 2, dtype=F32) / QK_ROPE))
    invf = _pad_cols(jnp.concatenate([inv_freq, inv_freq])[None, :], QK_NOPE, LANES)
    pos3 = positions.reshape(bsz, seq, 1)
    row = lambda v: v.reshape(1, -1)

    for l in range(depth):
        mod3 = _ada(c, w_ada[l], b_ada[l]).reshape(bsz, 6, d)

        wi = w_in[l]
        o0, o1, o2, o3 = 2 * lw, 2 * lw + qr, 2 * lw + qr + kvr, 2 * lw + qr + kvr + QK_ROPE
        wkr = wi[:, o2:o3]
        wkr2 = jnp.concatenate([_pad_cols(wkr, QK_NOPE, LANES),
                                _pad_cols(_rot_half_cols(wkr), QK_NOPE, LANES)], axis=1).astype(BF16)
        uq = w_uq[l].reshape(qr, MLA_HEADS, QK_NOPE + QK_ROPE)
        uq_main = _pad_cols(uq, 0, LANES).reshape(qr, d)
        uq_rot = _pad_cols(_rot_half_cols(uq[..., QK_NOPE:]), QK_NOPE, LANES).reshape(qr, d)
        wuq2 = jnp.concatenate([uq_main, uq_rot], axis=1).astype(BF16)
        ukv = w_ukv[l].reshape(kvr, MLA_HEADS, QK_NOPE + V_DIM)
        wuk = _pad_cols(ukv[..., :QK_NOPE], 0, LANES).reshape(kvr, d).astype(BF16)
        uv = ukv[..., QK_NOPE:]
        odd = (jnp.arange(MLA_HEADS) % 2 == 1)[None, :, None]
        wuv = jnp.where(odd, _pad_cols(uv, V_DIM, LANES), _pad_cols(uv, 0, LANES)).reshape(kvr, d).astype(BF16)
        half = jnp.arange(LANES)[None, :] >= V_DIM
        vones = (half != odd[0]).astype(F32).reshape(1, d)

        lru_in, q, k, v = _pre(x, mod3, row(norm1_g[l]), pos3, invf, wi[:, :o0].astype(BF16),
                               wi[:, o0:o1].astype(BF16), wi[:, o1:o2].astype(BF16), wkr2,
                               row(q_norm_g[l]), wuq2, row(kv_norm_g[l]), wuk, wuv, vones)

        eye = jnp.eye(LRU_BLOCKS, dtype=F32)
        blockdiag = lambda w: jnp.einsum("hij,hg->higj", w, eye).reshape(lw, lw)
        wg = jnp.concatenate([blockdiag(lru_wa[l]), blockdiag(lru_wx[l])], axis=1).astype(BF16)
        bg = jnp.concatenate([lru_ba[l], lru_bx[l]])[None, :]
        yl = _lru(lru_in, conv_w[l], row(conv_b[l]), wg, bg, row(lru_lambda[l]), row(lru_out_g[l]))

        ym = _attn(q, k, v, row(mla_out_g[l]))

        wo = w_out[l].astype(BF16)
        x1, h2t = _post(yl, ym, x, mod3, wo[:lw], wo[lw:], row(norm2_g[l]))

        wqt = (peer_wq[l].reshape(d, PEER_HEADS, 2, PEER_HALF).transpose(2, 1, 3, 0)
               .reshape(2 * PEER_HEADS * PEER_HALF, d).astype(BF16))
        keys = peer_keys[l].astype(BF16)
        kbig = jnp.einsum("hkd,hg->khgd", keys[:, 0], jnp.eye(PEER_HEADS, dtype=BF16)).reshape(
            PEER_KEYS * PEER_HEADS, PEER_HEADS * PEER_HALF)
        cnt, w1, packed = _route(h2t, wqt, kbig, keys[:, 1])

        ne = peer_u.shape[1] // EB
        vt3 = peer_v[l].astype(BF16).reshape(ne, EB, d).transpose(0, 2, 1)
        out = _dense(h2t, peer_u[l].astype(BF16), vt3, cnt, w1, packed,
                     x1.reshape(bsz * seq, d), mod3, row(final_g), seq)
        x = out.reshape(bsz, seq, d)
    return x
```
